```python
import math
import jax
import jax.numpy as jnp
from jax import lax
import numpy as np

D_MODEL = 1024
BATCH = 8
SEQ = 2048
DEPTH = 1

CHUNK = 64
N_META = 16
D_MIX = D_MODEL
A_WIDTH = D_MIX // 2
A_HEAD_DIM = 64
A_HEADS = A_WIDTH // (2 * A_HEAD_DIM)
B_WIDTH = D_MIX - A_WIDTH
B_KEY_DIM = 128
B_VAL_DIM = 128
B_HEADS = B_WIDTH // B_VAL_DIM
ROPE_THETA = 10000.0
Q_BLOCK = 128
A_QK = A_HEADS * 2 * A_HEAD_DIM
A_V = A_HEADS * 2 * A_HEAD_DIM
B_QF = B_HEADS * B_KEY_DIM
B_IG = B_HEADS * B_VAL_DIM
IN_SPLITS = (A_QK, A_QK, A_V, B_QF, B_QF, B_IG, B_IG)
D_IN = A_QK * 2 + A_V + B_QF * 2 + B_IG * 2
N_EXPERTS = 32
TOP_K = 4
D_EXPERT = D_MODEL
SWIGLU_ALPHA = 1.702
SWIGLU_LIMIT = 7.0
EXPERT_BLOCK = 256
DEEPNORM_ALPHA = (2 * DEPTH) ** 0.25
DEEPNORM_BETA = (8 * DEPTH) ** -0.25
LN_EPS = 1e-5
RMS_EPS = 1e-5

kernel_name = 'hybrid_diffattn_hgrn2_moe_block'


def layer_norm(x, g, b):
    xf = x.astype(jnp.float32)
    mu = jnp.mean(xf, axis=-1, keepdims=True)
    var = jnp.mean(jnp.square(xf - mu), axis=-1, keepdims=True)
    return ((xf - mu) * lax.rsqrt(var + LN_EPS) * g.astype(jnp.float32) + b.astype(jnp.float32)).astype(x.dtype)


def rms_norm(x, g):
    xf = x.astype(jnp.float32)
    ms = jnp.mean(jnp.square(xf), axis=-1, keepdims=True)
    return (xf * lax.rsqrt(ms + RMS_EPS) * g.astype(jnp.float32)).astype(x.dtype)


def chunk_ids(L):
    p = np.arange(L)
    return np.where(p < N_META, 0, 1 + (p - N_META) // CHUNK)


def chunk_end(p, L):
    if p < N_META:
        end = N_META
    else:
        end = N_META + CHUNK * ((p - N_META) // CHUNK + 1)
    return min(end, L)


def rope_tables(L, dim):
    pos = jnp.arange(L, dtype=jnp.float32)
    inv = 1.0 / (ROPE_THETA ** (jnp.arange(0, dim, 2, dtype=jnp.float32) / dim))
    ang = pos[:, None] * inv[None, :]
    ang = jnp.concatenate([ang, ang], axis=-1)
    return jnp.cos(ang), jnp.sin(ang)


def apply_rope(x, cos, sin):
    xf = x.astype(jnp.float32)
    half = xf.shape[-1] // 2
    rot = jnp.concatenate([-xf[..., half:], xf[..., :half]], axis=-1)
    return (xf * cos + rot * sin).astype(x.dtype)


def diff_attention(q, k, v, lam, subln_g, lam_init, cos, sin):
    B, L, _ = q.shape
    q = q.reshape(B, L, A_HEADS, 2, A_HEAD_DIM).transpose(0, 2, 3, 1, 4)
    k = k.reshape(B, L, A_HEADS, 2, A_HEAD_DIM).transpose(0, 2, 3, 1, 4)
    v = v.reshape(B, L, A_HEADS, 2 * A_HEAD_DIM).transpose(0, 2, 1, 3)
    q = apply_rope(q, cos, sin) * (A_HEAD_DIM ** -0.5)
    k = apply_rope(k, cos, sin)
    cid = chunk_ids(L)
    outs = []
    for qs in range(0, L, Q_BLOCK):
        qe = min(qs + Q_BLOCK, L)
        ke = chunk_end(qe - 1, L)
        s = jnp.einsum('bhmqd,bhmkd->bhmqk', q[:, :, :, qs:qe], k[:, :, :, :ke]).astype(jnp.float32)
        mask = jnp.asarray(cid[qs:qe, None] >= cid[None, :ke])
        p = jax.nn.softmax(jnp.where(mask, s, -jnp.inf), axis=-1)
        a = p[:, :, 0] - lam * p[:, :, 1]
        outs.append(jnp.einsum('bhqk,bhkd->bhqd', a.astype(v.dtype), v[:, :, :ke]))
    o = jnp.concatenate(outs, axis=2)
    o = rms_norm(o, subln_g) * (1.0 - lam_init)
    return o.transpose(0, 2, 1, 3).reshape(B, L, A_V)


def hgrn2(q, f, i, g, lb, norm_g):
    B, L, _ = q.shape

    def heads(t, d):
        return t.reshape(B, L, B_HEADS, d).transpose(0, 2, 1, 3).astype(jnp.float32)

    qh = jax.nn.silu(heads(q, B_KEY_DIM)) * (B_KEY_DIM ** -0.5)
    z = heads(f, B_KEY_DIM)
    lbh = lb.astype(jnp.float32).reshape(B_HEADS, 1, B_KEY_DIM)
    log_f = jnp.log(lbh + (1.0 - lbh) * jax.nn.sigmoid(z))
    kh = (1.0 - lbh) * jax.nn.sigmoid(-z)
    vh = heads(i, B_VAL_DIM)
    pad = (-L) % CHUNK
    padw = ((0, 0), (0, 0), (pad, 0), (0, 0))
    qh, kh, vh, log_f = (jnp.pad(t, padw) for t in (qh, kh, vh, log_f))
    n = (L + pad) // CHUNK

    def to_chunks(t):
        return t.reshape(B, B_HEADS, n, CHUNK, t.shape[-1]).transpose(2, 0, 1, 3, 4)

    causal = jnp.tril(jnp.ones((CHUNK, CHUNK), dtype=bool))

    def step(S, inp):
        qc, kc, vc, lfc = inp
        b = jnp.cumsum(lfc, axis=2)
        decay = jnp.exp(jnp.where(causal[:, :, None], b[:, :, :, None, :] - b[:, :, None, :, :], -jnp.inf))
        scores = jnp.einsum('bhtk,bhsk,bhtsk->bhts', qc, kc, decay)
        o = jnp.einsum('bhts,bhsv->bhtv', scores, vc) + jnp.einsum('bhtk,bhkv->bhtv', qc * jnp.exp(b), S)
        b_last = b[:, :, -1:, :]
        S = S * jnp.exp(b_last[:, :, 0, :, None]) + jnp.einsum('bhsk,bhsv->bhkv', kc * jnp.exp(b_last - b), vc)
        return S, o

    S0 = jnp.zeros((B, B_HEADS, B_KEY_DIM, B_VAL_DIM), jnp.float32)
    _, o = lax.scan(step, S0, (to_chunks(qh), to_chunks(kh), to_chunks(vh), to_chunks(log_f)))
    o = o.transpose(1, 2, 0, 3, 4).reshape(B, B_HEADS, n * CHUNK, B_VAL_DIM)[:, :, pad:]
    o = rms_norm(o, norm_g) * jax.nn.silu(heads(g, B_VAL_DIM))
    return o.transpose(0, 2, 1, 3).reshape(B, L, B_WIDTH).astype(q.dtype)


def moe(h, w_router, b_router, w_gu, b_gu, w_dn, b_dn):
    T, D = h.shape
    logits = (h @ w_router).astype(jnp.float32) + b_router.astype(jnp.float32)
    top_val, top_idx = lax.top_k(logits, TOP_K)
    gates = jax.nn.softmax(top_val, axis=-1)
    A = T * TOP_K
    e_flat = top_idx.reshape(A).astype(jnp.int32)
    order = jnp.argsort(e_flat)
    e_sorted = e_flat[order]
    counts = jnp.bincount(e_flat, length=N_EXPERTS)
    padded = ((counts + EXPERT_BLOCK - 1) // EXPERT_BLOCK) * EXPERT_BLOCK
    start_sorted = jnp.cumsum(counts) - counts
    cum_pad = jnp.cumsum(padded)
    start_pad = cum_pad - padded
    dest_sorted = (start_pad[e_sorted] + (jnp.arange(A) - start_sorted[e_sorted])).astype(jnp.int32)
    n_blocks = -(-A // EXPERT_BLOCK) + N_EXPERTS
    P = n_blocks * EXPERT_BLOCK
    row_token = jnp.full((P,), T, dtype=jnp.int32).at[dest_sorted].set((order // TOP_K).astype(jnp.int32))
    h_pad = jnp.concatenate([h, jnp.zeros((1, D), h.dtype)], axis=0)
    x_rows = h_pad[row_token].reshape(n_blocks, EXPERT_BLOCK, D)
    block_expert = jnp.clip(jnp.searchsorted(cum_pad, jnp.arange(n_blocks) * EXPERT_BLOCK, side='right'), 0, N_EXPERTS - 1)

    def expert_block(args):
        xb, e = args
        gu = xb @ w_gu[e] + b_gu[e]
        gate = jnp.minimum(gu[:, 0::2], SWIGLU_LIMIT)
        up = jnp.clip(gu[:, 1::2], -SWIGLU_LIMIT, SWIGLU_LIMIT)
        act = (up + 1.0) * gate * jax.nn.sigmoid(gate * SWIGLU_ALPHA)
        return act @ w_dn[e] + b_dn[e]

    y = lax.map(expert_block, (x_rows, block_expert)).reshape(P, D)
    dest = jnp.zeros((A,), jnp.int32).at[order].set(dest_sorted)
    y_assign = y[dest].reshape(T, TOP_K, D)
    return jnp.einsum('tkd,tk->td', y_assign, gates.astype(y.dtype))


def setup_inputs(seed: int = 0) -> dict:
    key = jax.random.key(seed)
    ks = jax.random.split(key, 24)
    f32 = jnp.float32
    nrm = lambda k, s: jax.random.normal(k, s, f32)
    return {
        'x': nrm(ks[0], (BATCH, SEQ, D_MODEL)),
        'meta_tokens': nrm(ks[1], (N_META, D_MODEL)),
        'ln_emb_g': 1.0 + 0.02 * nrm(ks[2], (D_MODEL,)),
        'ln_emb_b': 0.02 * nrm(ks[3], (D_MODEL,)),
        'w_in': nrm(ks[4], (DEPTH, D_MODEL, D_IN)) * D_MODEL ** -0.5,
        'lambda_q1': 0.1 * nrm(ks[5], (DEPTH, A_HEAD_DIM)),
        'lambda_k1': 0.1 * nrm(ks[6], (DEPTH, A_HEAD_DIM)),
        'lambda_q2': 0.1 * nrm(ks[7], (DEPTH, A_HEAD_DIM)),
        'lambda_k2': 0.1 * nrm(ks[8], (DEPTH, A_HEAD_DIM)),
        'subln_g': 1.0 + 0.02 * nrm(ks[9], (DEPTH, 2 * A_HEAD_DIM)),
        'hgrn_lb_table': 0.5 * nrm(ks[10], (DEPTH + 1, B_QF)),
        'hgrn_norm_g': 1.0 + 0.02 * nrm(ks[11], (DEPTH, B_VAL_DIM)),
        'w_out': nrm(ks[12], (DEPTH, D_MIX, D_MODEL)) * (D_MIX ** -0.5) * DEEPNORM_BETA,
        'ln1_g': 1.0 + 0.02 * nrm(ks[13], (DEPTH, D_MODEL)),
        'ln1_b': 0.02 * nrm(ks[14], (DEPTH, D_MODEL)),
        'w_router': nrm(ks[15], (DEPTH, D_MODEL, N_EXPERTS)) * D_MODEL ** -0.5,
        'b_router': 0.01 * nrm(ks[16], (DEPTH, N_EXPERTS)),
        'w_gate_up': nrm(ks[17], (DEPTH, N_EXPERTS, D_MODEL, 2 * D_EXPERT)) * D_MODEL ** -0.5,
        'b_gate_up': 0.01 * nrm(ks[18], (DEPTH, N_EXPERTS, 2 * D_EXPERT)),
        'w_down': nrm(ks[19], (DEPTH, N_EXPERTS, D_EXPERT, D_MODEL)) * (D_EXPERT ** -0.5) * DEEPNORM_BETA,
        'b_down': 0.01 * nrm(ks[20], (DEPTH, N_EXPERTS, D_MODEL)),
        'ln2_g': 1.0 + 0.02 * nrm(ks[21], (DEPTH, D_MODEL)),
        'ln2_b': 0.02 * nrm(ks[22], (DEPTH, D_MODEL)),
    }


def reference(x, meta_tokens, ln_emb_g, ln_emb_b, w_in, lambda_q1, lambda_k1, lambda_q2, lambda_k2,
              subln_g, hgrn_lb_table, hgrn_norm_g, w_out, ln1_g, ln1_b, w_router, b_router,
              w_gate_up, b_gate_up, w_down, b_down, ln2_g, ln2_b):
    B = x.shape[0]
    meta = jnp.broadcast_to(meta_tokens[None].astype(x.dtype), (B, N_META, D_MODEL))
    h = layer_norm(jnp.concatenate([meta, x], axis=1), ln_emb_g, ln_emb_b)
    L = h.shape[1]
    cos, sin = rope_tables(L, A_HEAD_DIM)
    lb_all = jnp.cumsum(jax.nn.softmax(hgrn_lb_table.astype(jnp.float32), axis=0), axis=0)
    split_at = [int(c) for c in np.cumsum(IN_SPLITS)[:-1]]
    for l in range(DEPTH):
        lam_init = 0.8 - 0.6 * math.exp(-0.3 * l)
        proj = jnp.einsum('bld,de->ble', h, w_in[l])
        qa, ka, va, qb, fb, ib, gb = jnp.split(proj, split_at, axis=-1)
        lam = (jnp.exp(jnp.sum(lambda_q1[l].astype(jnp.float32) * lambda_k1[l].astype(jnp.float32)))
               - jnp.exp(jnp.sum(lambda_q2[l].astype(jnp.float32) * lambda_k2[l].astype(jnp.float32)))
               + lam_init)
        ya = diff_attention(qa, ka, va, lam, subln_g[l], lam_init, cos, sin)
        yb = hgrn2(qb, fb, ib, gb, lb_all[l], hgrn_norm_g[l])
        mix = jnp.einsum('ble,ed->bld', jnp.concatenate([ya, yb], axis=-1), w_out[l])
        h = layer_norm(DEEPNORM_ALPHA * h + mix, ln1_g[l], ln1_b[l])
        ffn = moe(h.reshape(-1, D_MODEL), w_router[l], b_router[l], w_gate_up[l], b_gate_up[l],
                  w_down[l], b_down[l]).reshape(h.shape)
        h = layer_norm(DEEPNORM_ALPHA * h + ffn, ln2_g[l], ln2_b[l])
    return h[:, N_META:]
```

```python
import functools
import math

import jax
import jax.numpy as jnp
import numpy as np
from jax import lax
from jax.experimental import pallas as pl
from jax.experimental.pallas import tpu as pltpu

D_MODEL = 1024
N_META = 16
CHUNK = 64
A_HEADS = 4
A_HEAD_DIM = 64
B_HEADS = 4
B_KEY_DIM = 128
ROPE_THETA = 10000.0
A_W = 512
B_W = 512
D_IN = 3 * A_W + 4 * B_W
N_EXPERTS = 32
TOP_K = 4
D_EXPERT = 1024
SWIGLU_ALPHA = 1.702
SWIGLU_LIMIT = 7.0
EXPERT_BLOCK = 256
DEEPNORM_ALPHA = 2.0 ** 0.25
LN_EPS = 1e-5
RMS_EPS = 1e-5
LAM_INIT = 0.8 - 0.6 * math.exp(0.0)

LANES = 128
HALF = D_MODEL // 2
ROW_TILE = 256
PROJ_TILE = 512
ATT_TILE = 256
HG_TILE = 256
VMEM_LIMIT = 56 * 1024 * 1024

_NT = (((1,), (1,)), ((), ()))
_TN = (((0,), (0,)), ((), ()))

f32 = jnp.float32
bf16 = jnp.bfloat16
u32 = jnp.uint32
i32 = jnp.int32


def _layer_norm(x, g, b):
    mu = jnp.mean(x, axis=-1, keepdims=True)
    xc = x - mu
    var = jnp.mean(xc * xc, axis=-1, keepdims=True)
    return xc * lax.rsqrt(var + LN_EPS) * g + b


def _pack_rows(h):
    hb = h.astype(bf16).astype(f32)
    lo = lax.bitcast_convert_type(hb[:, :HALF], u32) >> 16
    hi = lax.bitcast_convert_type(hb[:, HALF:], u32)
    return hi | lo


def _unpack_rows(w):
    left = lax.bitcast_convert_type(w << 16, f32)
    right = lax.bitcast_convert_type(w & jnp.uint32(0xFFFF0000), f32)
    return left, right


def _rope(x, cos, sin_signed, first_half):
    fwd = pltpu.roll(x, 32, axis=1)
    bwd = pltpu.roll(x, 96, axis=1)
    return x * cos + jnp.where(first_half, bwd, fwd) * sin_signed


def _in_proj_kernel(x_ref, g_ref, b_ref, w_ref, cq_ref, sq_ref, ck_ref, sk_ref, qkv_ref, hg_ref):
    h = _layer_norm(x_ref[...], g_ref[...], b_ref[...]).astype(bf16)
    lane = lax.broadcasted_iota(i32, (h.shape[0], LANES), 1)
    first_half = (lane & 63) < 32
    for c in range(2 * A_HEADS):
        acc = jnp.dot(h, w_ref[:, c * LANES:(c + 1) * LANES], preferred_element_type=f32)
        if c < A_HEADS:
            r = _rope(acc, cq_ref[...], sq_ref[...], first_half)
        else:
            r = _rope(acc, ck_ref[...], sk_ref[...], first_half)
        qkv_ref[:, c * LANES:(c + 1) * LANES] = r.astype(bf16)
    qkv_ref[:, 2 * A_W:3 * A_W] = jnp.dot(
        h, w_ref[:, 2 * A_W:3 * A_W], preferred_element_type=f32).astype(bf16)
    for c in range(4):
        lo = 3 * A_W + c * B_W
        hg_ref[:, c * B_W:(c + 1) * B_W] = jnp.dot(h, w_ref[:, lo:lo + B_W], preferred_element_type=f32)


def _in_proj(x2, g, b, w_bf, tabs, tile, tab_blocks):
    n = x2.shape[0]
    tab_spec = pl.BlockSpec((tile, LANES), lambda i: (i % tab_blocks, 0))
    vec = pl.BlockSpec((1, D_MODEL), lambda i: (0, 0))
    return pl.pallas_call(
        _in_proj_kernel,
        grid=(n // tile,),
        in_specs=[
            pl.BlockSpec((tile, D_MODEL), lambda i: (i, 0)),
            vec, vec,
            pl.BlockSpec((D_MODEL, D_IN), lambda i: (0, 0)),
            tab_spec, tab_spec, tab_spec, tab_spec,
        ],
        out_specs=[
            pl.BlockSpec((tile, 3 * A_W), lambda i: (i, 0)),
            pl.BlockSpec((tile, 4 * B_W), lambda i: (i, 0)),
        ],
        out_shape=[
            jax.ShapeDtypeStruct((n, 3 * A_W), bf16),
            jax.ShapeDtypeStruct((n, 4 * B_W), f32),
        ],
        compiler_params=pltpu.CompilerParams(
            dimension_semantics=("arbitrary",), vmem_limit_bytes=VMEM_LIMIT),
        name="in_proj",
    )(x2, g, b, w_bf, *tabs)


def _attn_kernel(lam_ref, q_ref, k_ref, v_ref, km_ref, vm_ref, g_ref, o_ref, m_scr, l_scr, acc_scr):
    qi = pl.program_id(2)
    tq = ATT_TILE
    q = q_ref[...]
    lane = lax.broadcasted_iota(i32, (tq, LANES), 1)
    zero = jnp.zeros_like(q)
    qs = jnp.concatenate([jnp.where(lane < A_HEAD_DIM, q, zero),
                          jnp.where(lane >= A_HEAD_DIM, q, zero)], axis=0)

    def update(s, v):
        m_old = m_scr[...]
        m_new = jnp.maximum(m_old, jnp.max(s, axis=1, keepdims=True))
        corr = jnp.exp(m_old - m_new)
        p = jnp.exp(s - m_new)
        l_scr[...] = l_scr[...] * corr + jnp.sum(p, axis=1, keepdims=True)
        acc_scr[...] = acc_scr[...] * corr + jnp.dot(p.astype(bf16), v, preferred_element_type=f32)
        m_scr[...] = m_new

    s = lax.dot_general(qs, km_ref[...], _NT, preferred_element_type=f32)
    col = lax.broadcasted_iota(i32, s.shape, 1)
    s = jnp.where(col < N_META, s, -jnp.inf)
    m0 = jnp.max(s, axis=1, keepdims=True)
    p0 = jnp.exp(s - m0)
    m_scr[...] = m0
    l_scr[...] = jnp.sum(p0, axis=1, keepdims=True)
    acc_scr[...] = jnp.dot(p0.astype(bf16), vm_ref[...], preferred_element_type=f32)

    def full_tile(j, carry):
        start = pl.multiple_of(j * tq, tq)
        kj = k_ref[pl.ds(start, tq), :]
        update(lax.dot_general(qs, kj, _NT, preferred_element_type=f32), v_ref[pl.ds(start, tq), :])
        return carry

    lax.fori_loop(0, qi, full_tile, 0)

    start = pl.multiple_of(qi * tq, tq)
    s = lax.dot_general(qs, k_ref[pl.ds(start, tq), :], _NT, preferred_element_type=f32)
    row = lax.broadcasted_iota(i32, s.shape, 0)
    col = lax.broadcasted_iota(i32, s.shape, 1)
    visible = ((row & (tq - 1)) >> 6) >= (col >> 6)
    update(jnp.where(visible, s, -jnp.inf), v_ref[pl.ds(start, tq), :])

    acc = acc_scr[...]
    l = l_scr[...]
    o = acc[:tq] / l[:tq] - lam_ref[0] * (acc[tq:] / l[tq:])
    ms = jnp.mean(o * o, axis=-1, keepdims=True)
    o_ref[...] = (o * lax.rsqrt(ms + RMS_EPS) * g_ref[...] * (1.0 - LAM_INIT)).astype(bf16)


def _attention(lam, qkv, qkv_meta, subln_g, batch, seq):
    nq = seq // ATT_TILE
    return pl.pallas_call(
        _attn_kernel,
        grid_spec=pltpu.PrefetchScalarGridSpec(
            num_scalar_prefetch=1,
            grid=(batch, A_HEADS, nq),
            in_specs=[
                pl.BlockSpec((ATT_TILE, LANES), lambda b, h, i, lam: (b * nq + i, h)),
                pl.BlockSpec((seq, LANES), lambda b, h, i, lam: (b, A_HEADS + h)),
                pl.BlockSpec((seq, LANES), lambda b, h, i, lam: (b, 2 * A_HEADS + h)),
                pl.BlockSpec((LANES, LANES), lambda b, h, i, lam: (0, A_HEADS + h)),
                pl.BlockSpec((LANES, LANES), lambda b, h, i, lam: (0, 2 * A_HEADS + h)),
                pl.BlockSpec((1, LANES), lambda b, h, i, lam: (0, 0)),
            ],
            out_specs=pl.BlockSpec((ATT_TILE, LANES), lambda b, h, i, lam: (b * nq + i, h)),
            scratch_shapes=[
                pltpu.VMEM((2 * ATT_TILE, 1), f32),
                pltpu.VMEM((2 * ATT_TILE, 1), f32),
                pltpu.VMEM((2 * ATT_TILE, LANES), f32),
            ],
        ),
        out_shape=jax.ShapeDtypeStruct((batch * seq, A_W), bf16),
        compiler_params=pltpu.CompilerParams(
            dimension_semantics=("arbitrary", "arbitrary", "arbitrary"), vmem_limit_bytes=VMEM_LIMIT),
        name="attn",
    )(lam, qkv, qkv, qkv, qkv_meta, qkv_meta, subln_g)


def _split3(x):
    a = x.astype(bf16)
    r = x - a.astype(f32)
    b = r.astype(bf16)
    c = (r - b.astype(f32)).astype(bf16)
    return a, b, c


def _chunk_cumsum(tri, x):
    a, b, c = _split3(x)
    out = jnp.dot(tri, a, preferred_element_type=f32)
    out = out + jnp.dot(tri, b, preferred_element_type=f32)
    return out + jnp.dot(tri, c, preferred_element_type=f32)


def _gates(z, lb):
    log_f = jnp.log(lb + (1.0 - lb) * jax.nn.sigmoid(z))
    key = (1.0 - lb) * jax.nn.sigmoid(-z)
    return log_f, key


def _hgrn_kernel(q_ref, f_ref, i_ref, g_ref, fm_ref, im_ref, lb_ref, ng_ref, o_ref,
                 oin_scr, qhat_scr, ut_scr, dec_scr):
    lb = lb_ref[...]
    n_tiles = q_ref.shape[0] // HG_TILE
    per_tile = HG_TILE // CHUNK

    lfm, km = _gates(fm_ref[...], lb)
    r16 = lax.broadcasted_iota(i32, (N_META, N_META), 0)
    c16 = lax.broadcasted_iota(i32, (N_META, N_META), 1)
    bm = _chunk_cumsum((c16 <= r16).astype(bf16), lfm)
    kdm = km * jnp.exp(bm[N_META - 1:N_META, :] - bm)
    st = lax.dot_general(im_ref[...].astype(bf16), kdm.astype(bf16), _TN, preferred_element_type=f32)

    row = lax.broadcasted_iota(i32, (HG_TILE, HG_TILE), 0)
    col = lax.broadcasted_iota(i32, (HG_TILE, HG_TILE), 1)
    causal = ((row >> 6) == (col >> 6)) & (col <= row)
    tri = causal.astype(bf16)

    for t in range(n_tiles):
        rows = slice(t * HG_TILE, (t + 1) * HG_TILE)
        qv = q_ref[rows, :]
        qh = qv * jax.nn.sigmoid(qv) * (B_KEY_DIM ** -0.5)
        log_f, kh = _gates(f_ref[rows, :], lb)
        b = _chunk_cumsum(tri, log_f)
        b3 = b.reshape(per_tile, CHUNK, LANES)
        b_mid = jnp.broadcast_to(b3[:, CHUNK // 2:CHUNK // 2 + 1, :], b3.shape).reshape(HG_TILE, LANES)
        b_last3 = b3[:, CHUNK - 1:CHUNK, :]
        b_last = jnp.broadcast_to(b_last3, b3.shape).reshape(HG_TILE, LANES)
        qt = (qh * jnp.exp(b - b_mid)).astype(bf16)
        kt = (kh * jnp.exp(jnp.minimum(b_mid - b, 80.0))).astype(bf16)
        a = lax.dot_general(qt, kt, _NT, preferred_element_type=f32)
        a = jnp.where(causal, a, 0.0).astype(bf16)
        vv = i_ref[rows, :].astype(bf16)
        oin_scr[rows, :] = jnp.dot(a, vv, preferred_element_type=f32)
        qhat_scr[rows, :] = (qh * jnp.exp(b)).astype(bf16)
        kd = (kh * jnp.exp(b_last - b)).astype(bf16)
        for c in range(per_tile):
            cr = slice(c * CHUNK, (c + 1) * CHUNK)
            n = t * per_tile + c
            ut_scr[n] = lax.dot_general(vv[cr], kd[cr], _TN, preferred_element_type=f32)
            dec_scr[n] = jnp.exp(b_last3[c])

    ng = ng_ref[...]
    for n in range(n_tiles * per_tile):
        rows = slice(n * CHUNK, (n + 1) * CHUNK)
        o = oin_scr[rows, :] + lax.dot_general(qhat_scr[rows, :], st.astype(bf16), _NT,
                                               preferred_element_type=f32)
        ms = jnp.mean(o * o, axis=-1, keepdims=True)
        gv = g_ref[rows, :]
        o_ref[rows, :] = (o * lax.rsqrt(ms + RMS_EPS) * ng * (gv * jax.nn.sigmoid(gv))).astype(bf16)
        st = st * dec_scr[n] + ut_scr[n]


def _hgrn(hg, hg_meta, lb, norm_g, batch, seq):
    n_chunks = seq // CHUNK

    def col(c):
        return pl.BlockSpec((seq, LANES), lambda b, h: (b, c * B_HEADS + h))

    def mcol(c):
        return pl.BlockSpec((N_META, LANES), lambda b, h: (0, c * B_HEADS + h))

    return pl.pallas_call(
        _hgrn_kernel,
        grid=(batch, B_HEADS),
        in_specs=[
            col(0), col(1), col(2), col(3), mcol(1), mcol(2),
            pl.BlockSpec((None, 1, LANES), lambda b, h: (h, 0, 0)),
            pl.BlockSpec((1, LANES), lambda b, h: (0, 0)),
        ],
        out_specs=pl.BlockSpec((seq, LANES), lambda b, h: (b, h)),
        out_shape=jax.ShapeDtypeStruct((batch * seq, B_W), bf16),
        scratch_shapes=[
            pltpu.VMEM((seq, LANES), f32),
            pltpu.VMEM((seq, LANES), bf16),
            pltpu.VMEM((n_chunks, LANES, LANES), f32),
            pltpu.VMEM((n_chunks, 1, LANES), f32),
        ],
        compiler_params=pltpu.CompilerParams(
            dimension_semantics=("arbitrary", "arbitrary"), vmem_limit_bytes=VMEM_LIMIT),
        name="hgrn",
    )(hg, hg, hg, hg, hg_meta, hg_meta, lb, norm_g)


def _mix_route_kernel(x_ref, g0_ref, b0_ref, ya_ref, yb_ref, wo_ref, g1_ref, b1_ref,
                      wr_hi_ref, wr_lo_ref, br_ref,
                      h_ref, hp_ref, idx_ref, gate_ref, rank_ref, cnt_ref, carry_scr):
    step = pl.program_id(0)

    @pl.when(step == 0)
    def _():
        carry_scr[...] = jnp.zeros_like(carry_scr)

    h0 = _layer_norm(x_ref[...], g0_ref[...], b0_ref[...])
    mix = jnp.dot(ya_ref[...], wo_ref[:A_W, :], preferred_element_type=f32)
    mix = mix + jnp.dot(yb_ref[...], wo_ref[A_W:, :], preferred_element_type=f32)
    h1 = _layer_norm(DEEPNORM_ALPHA * h0 + mix, g1_ref[...], b1_ref[...])
    h_ref[...] = h1
    hp_ref[...] = _pack_rows(h1)

    h_hi = h1.astype(bf16)
    h_lo = (h1 - h_hi.astype(f32)).astype(bf16)
    logits = jnp.dot(h_hi, wr_hi_ref[...], preferred_element_type=f32)
    logits = logits + jnp.dot(h_lo, wr_hi_ref[...], preferred_element_type=f32)
    logits = logits + jnp.dot(h_hi, wr_lo_ref[...], preferred_element_type=f32)
    logits = logits + br_ref[...]

    tm = logits.shape[0]
    lane = lax.broadcasted_iota(i32, (tm, N_EXPERTS), 1)
    work = logits
    vals, hots = [], []
    sel = jnp.zeros((tm, N_EXPERTS), f32)
    for k in range(TOP_K):
        m = jnp.max(work, axis=1, keepdims=True)
        first = jnp.min(jnp.where(work == m, lane, N_EXPERTS), axis=1, keepdims=True)
        hot = lane == first
        vals.append(m)
        hots.append(hot)
        idx_ref[:, k:k + 1] = first
        sel = sel + hot.astype(f32)
        work = jnp.where(hot, -jnp.inf, work)

    es = [jnp.exp(v - vals[0]) for v in vals]
    denom = es[0] + es[1] + es[2] + es[3]
    for k in range(TOP_K):
        gate_ref[:, k:k + 1] = es[k] / denom

    r = lax.broadcasted_iota(i32, (tm, tm), 0)
    c = lax.broadcasted_iota(i32, (tm, tm), 1)
    incl = jnp.dot((c <= r).astype(bf16), sel.astype(bf16), preferred_element_type=f32)
    carry = carry_scr[...]
    excl = incl - sel + carry
    for k in range(TOP_K):
        rank_ref[:, k:k + 1] = jnp.sum(jnp.where(hots[k], excl, 0.0), axis=1, keepdims=True).astype(i32)
    carry = carry + incl[tm - 1:tm, :]
    carry_scr[...] = carry
    cnt_ref[...] = carry.astype(i32)


def _mix_route(x2, g0, b0, ya, yb, wo, g1, b1, wr_hi, wr_lo, br):
    n = x2.shape[0]
    tm = ROW_TILE

    def full(shape):
        return pl.BlockSpec(shape, lambda i: (0,) * len(shape))

    vec = full((1, D_MODEL))
    small = pl.BlockSpec((tm, TOP_K), lambda i: (i, 0))
    return pl.pallas_call(
        _mix_route_kernel,
        grid=(n // tm,),
        in_specs=[
            pl.BlockSpec((tm, D_MODEL), lambda i: (i, 0)), vec, vec,
            pl.BlockSpec((tm, A_W), lambda i: (i, 0)),
            pl.BlockSpec((tm, B_W), lambda i: (i, 0)),
            full((D_MODEL, D_MODEL)), vec, vec,
            full((D_MODEL, N_EXPERTS)), full((D_MODEL, N_EXPERTS)), full((1, N_EXPERTS)),
        ],
        out_specs=[
            pl.BlockSpec((tm, D_MODEL), lambda i: (i, 0)),
            pl.BlockSpec((tm, HALF), lambda i: (i, 0)),
            small, small, small,
            full((1, N_EXPERTS)),
        ],
        out_shape=[
            jax.ShapeDtypeStruct((n, D_MODEL), f32),
            jax.ShapeDtypeStruct((n, HALF), u32),
            jax.ShapeDtypeStruct((n, TOP_K), i32),
            jax.ShapeDtypeStruct((n, TOP_K), f32),
            jax.ShapeDtypeStruct((n, TOP_K), i32),
            jax.ShapeDtypeStruct((1, N_EXPERTS), i32),
        ],
        scratch_shapes=[pltpu.VMEM((1, N_EXPERTS), f32)],
        compiler_params=pltpu.CompilerParams(
            dimension_semantics=("arbitrary",), vmem_limit_bytes=VMEM_LIMIT),
        name="mix_route",
    )(x2, g0, b0, ya, yb, wo, g1, b1, wr_hi, wr_lo, br)


def _row_copy(src, src_row, dst, dst_row, sem):
    return pltpu.make_async_copy(src.at[pl.ds(src_row, 1), :], dst.at[pl.ds(dst_row, 1), :], sem)


def _dispatch_kernel(hp_ref, dest_hbm, zero_hbm, rows_hbm, idx_smem, idx_sem, row_sem):
    del zero_hbm
    step = pl.program_id(0)
    n_assign = ROW_TILE * TOP_K
    idx_copy = pltpu.make_async_copy(dest_hbm.at[step], idx_smem, idx_sem)
    idx_copy.start()
    idx_copy.wait()

    def issue(a, carry):
        _row_copy(hp_ref, lax.shift_right_logical(a, 2), rows_hbm, idx_smem[a], row_sem).start()
        return carry

    lax.fori_loop(0, n_assign, issue, 0)

    def drain(a, carry):
        _row_copy(hp_ref, 0, rows_hbm, 0, row_sem).wait()
        return carry

    lax.fori_loop(0, n_assign, drain, 0)


def _dispatch(hp, dest_tiles, n_rows):
    n = hp.shape[0]
    zeros = jnp.zeros((n_rows, HALF), u32)
    return pl.pallas_call(
        _dispatch_kernel,
        grid=(n // ROW_TILE,),
        in_specs=[
            pl.BlockSpec((ROW_TILE, HALF), lambda i: (i, 0)),
            pl.BlockSpec(memory_space=pl.ANY),
            pl.BlockSpec(memory_space=pl.ANY),
        ],
        out_specs=pl.BlockSpec(memory_space=pl.ANY),
        out_shape=jax.ShapeDtypeStruct((n_rows, HALF), u32),
        scratch_shapes=[
            pltpu.SMEM((ROW_TILE * TOP_K,), i32),
            pltpu.SemaphoreType.DMA(()),
            pltpu.SemaphoreType.DMA(()),
        ],
        input_output_aliases={2: 0},
        compiler_params=pltpu.CompilerParams(
            dimension_semantics=("arbitrary",), vmem_limit_bytes=VMEM_LIMIT),
        name="dispatch",
    )(hp, dest_tiles, zeros)


def _experts_kernel(be_ref, na_ref, x_ref, wgu_ref, bg_ref, bu_ref, wdn_ref, bdn_ref, y_ref,
                    wg_scr, wu_scr, wd_scr, act_scr):
    j = pl.program_id(0)
    prev = be_ref[jnp.maximum(j - 1, 0)]
    fresh = jnp.logical_or(j == 0, be_ref[j] != prev)

    @pl.when(jnp.logical_and(fresh, j < na_ref[0]))
    def _():
        src = lax.broadcasted_iota(i32, (2 * LANES, 2 * LANES), 0)
        dst = lax.broadcasted_iota(i32, (2 * LANES, 2 * LANES), 1)
        perm = (src == jnp.where(dst < LANES, 2 * dst, 2 * (dst - LANES) + 1)).astype(bf16)
        for t in range(2 * D_EXPERT // (2 * LANES)):
            w = wgu_ref[:, t * 2 * LANES:(t + 1) * 2 * LANES].astype(bf16)
            sep = jnp.dot(w, perm, preferred_element_type=f32)
            wg_scr[:, t * LANES:(t + 1) * LANES] = sep[:, :LANES].astype(bf16)
            wu_scr[:, t * LANES:(t + 1) * LANES] = sep[:, LANES:].astype(bf16)
        wd_scr[...] = wdn_ref[...].astype(bf16)

    @pl.when(j < na_ref[0])
    def _():
        left, right = _unpack_rows(x_ref[...])
        xl = left.astype(bf16)
        xr = right.astype(bf16)
        fc = 256
        for c in range(D_EXPERT // fc):
            cols = slice(c * fc, (c + 1) * fc)
            gate = jnp.dot(xl, wg_scr[:HALF, cols], preferred_element_type=f32)
            gate = gate + jnp.dot(xr, wg_scr[HALF:, cols], preferred_element_type=f32) + bg_ref[:, cols]
            up = jnp.dot(xl, wu_scr[:HALF, cols], preferred_element_type=f32)
            up = up + jnp.dot(xr, wu_scr[HALF:, cols], preferred_element_type=f32) + bu_ref[:, cols]
            gate = jnp.minimum(gate, SWIGLU_LIMIT)
            up = jnp.clip(up, -SWIGLU_LIMIT, SWIGLU_LIMIT)
            act_scr[:, cols] = ((up + 1.0) * gate * jax.nn.sigmoid(gate * SWIGLU_ALPHA)).astype(bf16)
        y = jnp.dot(act_scr[...], wd_scr[...], preferred_element_type=f32) + bdn_ref[...]
        y_ref[...] = _pack_rows(y)

    @pl.when(j >= na_ref[0])
    def _():
        y_ref[...] = jnp.zeros_like(y_ref)


def _experts(block_expert, n_active, rows, w_gu, bg, bu, w_dn, b_dn):
    n_blocks = rows.shape[0] // EXPERT_BLOCK

    def blk(j, be, na):
        return (jnp.minimum(j, na[0] - 1), 0)

    def per_expert(shape):
        return pl.BlockSpec((None,) + shape, lambda j, be, na: (be[j], 0, 0))

    return pl.pallas_call(
        _experts_kernel,
        grid_spec=pltpu.PrefetchScalarGridSpec(
            num_scalar_prefetch=2,
            grid=(n_blocks,),
            in_specs=[
                pl.BlockSpec((EXPERT_BLOCK, HALF), blk),
                per_expert((D_MODEL, 2 * D_EXPERT)),
                per_expert((1, D_EXPERT)),
                per_expert((1, D_EXPERT)),
                per_expert((D_EXPERT, D_MODEL)),
                per_expert((1, D_MODEL)),
            ],
            out_specs=pl.BlockSpec((EXPERT_BLOCK, HALF), lambda j, be, na: (j, 0)),
            scratch_shapes=[
                pltpu.VMEM((D_MODEL, D_EXPERT), bf16),
                pltpu.VMEM((D_MODEL, D_EXPERT), bf16),
                pltpu.VMEM((D_EXPERT, D_MODEL), bf16),
                pltpu.VMEM((EXPERT_BLOCK, D_EXPERT), bf16),
            ],
        ),
        out_shape=jax.ShapeDtypeStruct(rows.shape, u32),
        compiler_params=pltpu.CompilerParams(
            dimension_semantics=("arbitrary",), vmem_limit_bytes=VMEM_LIMIT),
        name="experts",
    )(block_expert, n_active, rows, w_gu, bg, bu, w_dn, b_dn)


def _combine_kernel(h_ref, gate_ref, g2_ref, b2_ref, dest_hbm, y_hbm, o_ref,
                    idx_smem, ybuf, idx_sem, row_sem):
    step = pl.program_id(0)
    n_assign = ROW_TILE * TOP_K
    idx_copy = pltpu.make_async_copy(dest_hbm.at[step], idx_smem, idx_sem)
    idx_copy.start()
    idx_copy.wait()

    def issue(a, carry):
        slot = (a & (TOP_K - 1)) * ROW_TILE + lax.shift_right_logical(a, 2)
        _row_copy(y_hbm, idx_smem[a], ybuf, slot, row_sem).start()
        return carry

    lax.fori_loop(0, n_assign, issue, 0)

    def drain(a, carry):
        _row_copy(y_hbm, 0, ybuf, 0, row_sem).wait()
        return carry

    lax.fori_loop(0, n_assign, drain, 0)

    gates = gate_ref[...]
    left = jnp.zeros((ROW_TILE, HALF), f32)
    right = jnp.zeros((ROW_TILE, HALF), f32)
    for k in range(TOP_K):
        yl, yr = _unpack_rows(ybuf[k * ROW_TILE:(k + 1) * ROW_TILE, :])
        gk = gates[:, k:k + 1]
        left = left + yl * gk
        right = right + yr * gk
    ffn = jnp.concatenate([left, right], axis=1)
    o_ref[...] = _layer_norm(DEEPNORM_ALPHA * h_ref[...] + ffn, g2_ref[...], b2_ref[...])


def _combine(h1, gates, g2, b2, dest_tiles, y_rows):
    n = h1.shape[0]
    vec = pl.BlockSpec((1, D_MODEL), lambda i: (0, 0))
    return pl.pallas_call(
        _combine_kernel,
        grid=(n // ROW_TILE,),
        in_specs=[
            pl.BlockSpec((ROW_TILE, D_MODEL), lambda i: (i, 0)),
            pl.BlockSpec((ROW_TILE, TOP_K), lambda i: (i, 0)),
            vec, vec,
            pl.BlockSpec(memory_space=pl.ANY),
            pl.BlockSpec(memory_space=pl.ANY),
        ],
        out_specs=pl.BlockSpec((ROW_TILE, D_MODEL), lambda i: (i, 0)),
        out_shape=jax.ShapeDtypeStruct((n, D_MODEL), f32),
        scratch_shapes=[
            pltpu.SMEM((ROW_TILE * TOP_K,), i32),
            pltpu.VMEM((TOP_K * ROW_TILE, HALF), u32),
            pltpu.SemaphoreType.DMA(()),
            pltpu.SemaphoreType.DMA(()),
        ],
        compiler_params=pltpu.CompilerParams(
            dimension_semantics=("arbitrary",), vmem_limit_bytes=VMEM_LIMIT),
        name="combine",
    )(h1, gates, g2, b2, dest_tiles, y_rows)


def _rope_tables(pos, scale):
    inv = 1.0 / (ROPE_THETA ** (jnp.arange(0, A_HEAD_DIM, 2, dtype=f32) / A_HEAD_DIM))
    ang = pos.astype(f32)[:, None] * inv[None, :]
    ang = jnp.concatenate([ang, ang, ang, ang], axis=-1)
    sign = jnp.where((jnp.arange(LANES) & 63) < 32, -1.0, 1.0).astype(f32)
    return jnp.cos(ang) * scale, jnp.sin(ang) * sign * scale


def kernel(x, meta_tokens, ln_emb_g, ln_emb_b, w_in, lambda_q1, lambda_k1, lambda_q2, lambda_k2,
           subln_g, hgrn_lb_table, hgrn_norm_g, w_out, ln1_g, ln1_b, w_router, b_router,
           w_gate_up, b_gate_up, w_down, b_down, ln2_g, ln2_b):
    batch, seq, _ = x.shape
    n_tok = batch * seq
    x2 = x.reshape(n_tok, D_MODEL)
    row = lambda v: v.reshape(1, -1).astype(f32)

    w_in_bf = w_in[0].astype(bf16)
    lam = (jnp.exp(jnp.sum(lambda_q1[0].astype(f32) * lambda_k1[0].astype(f32)))
           - jnp.exp(jnp.sum(lambda_q2[0].astype(f32) * lambda_k2[0].astype(f32))) + LAM_INIT).reshape(1)
    lb = jnp.cumsum(jax.nn.softmax(hgrn_lb_table.astype(f32), axis=0), axis=0)[0].reshape(B_HEADS, 1, LANES)
    q_scale = A_HEAD_DIM ** -0.5
    pos_x = N_META + jnp.arange(seq)
    pos_m = jnp.arange(N_META)
    cq, sq = _rope_tables(pos_x, q_scale)
    ck, sk = _rope_tables(pos_x, 1.0)
    cqm, sqm = _rope_tables(pos_m, q_scale)
    ckm, skm = _rope_tables(pos_m, 1.0)
    wr = w_router[0].astype(f32)
    wr_hi = wr.astype(bf16)
    wr_lo = (wr - wr_hi.astype(f32)).astype(bf16)
    bg = b_gate_up[0][:, 0::2].reshape(N_EXPERTS, 1, D_EXPERT)
    bu = b_gate_up[0][:, 1::2].reshape(N_EXPERTS, 1, D_EXPERT)
    bdn = b_down[0].reshape(N_EXPERTS, 1, D_MODEL)

    g0, b0 = row(ln_emb_g), row(ln_emb_b)
    qkv, hg = _in_proj(x2, g0, b0, w_in_bf, (cq, sq, ck, sk), PROJ_TILE, seq // PROJ_TILE)
    qkv_m, hg_m = _in_proj(meta_tokens, g0, b0, w_in_bf, (cqm, sqm, ckm, skm), N_META, 1)
    qkv_m = jnp.pad(qkv_m, ((0, LANES - N_META), (0, 0)))

    ya = _attention(lam, qkv, qkv_m, row(subln_g[0]), batch, seq)
    yb = _hgrn(hg, hg_m, lb, row(hgrn_norm_g[0]), batch, seq)

    h1, hp, top_idx, gates, rank, counts = _mix_route(
        x2, g0, b0, ya, yb, w_out[0].astype(bf16), row(ln1_g[0]), row(ln1_b[0]),
        wr_hi, wr_lo, row(b_router[0]))

    counts = counts.reshape(N_EXPERTS)
    blocks = (counts + EXPERT_BLOCK - 1) // EXPERT_BLOCK
    cum_blocks = jnp.cumsum(blocks)
    start_pad = (cum_blocks - blocks) * EXPERT_BLOCK
    n_blocks = n_tok * TOP_K // EXPERT_BLOCK + N_EXPERTS
    n_active = cum_blocks[-1:].astype(i32)
    jb = jnp.arange(n_blocks, dtype=i32)
    block_expert = jnp.sum((cum_blocks[None, :] <= jnp.minimum(jb, n_active - 1)[:, None]).astype(i32), axis=1)
    block_expert = jnp.minimum(block_expert, N_EXPERTS - 1).astype(i32)
    hot = top_idx[:, :, None] == jnp.arange(N_EXPERTS, dtype=i32)[None, None, :]
    dest = jnp.sum(jnp.where(hot, start_pad.astype(i32)[None, None, :], 0), axis=-1) + rank
    dest_tiles = dest.reshape(n_tok // ROW_TILE, ROW_TILE * TOP_K)

    x_rows = _dispatch(hp, dest_tiles, n_blocks * EXPERT_BLOCK)
    y_rows = _experts(block_expert, n_active, x_rows, w_gate_up[0], bg, bu, w_down[0], bdn)
    out = _combine(h1, gates, row(ln2_g[0]), row(ln2_b[0]), dest_tiles, y_rows)
    return out.reshape(batch, seq, D_MODEL)
```

```python
import functools
import math

import jax
import jax.numpy as jnp
import numpy as np
from jax import lax
from jax.experimental import pallas as pl
from jax.experimental.pallas import tpu as pltpu

D_MODEL = 1024
N_META = 16
CHUNK = 64
A_HEADS = 4
A_HEAD_DIM = 64
B_HEADS = 4
B_KEY_DIM = 128
ROPE_THETA = 10000.0
A_W = 512
B_W = 512
D_IN = 3 * A_W + 4 * B_W
N_EXPERTS = 32
TOP_K = 4
D_EXPERT = 1024
SWIGLU_ALPHA = 1.702
SWIGLU_LIMIT = 7.0
EXPERT_BLOCK = 256
DEEPNORM_ALPHA = 2.0 ** 0.25
LN_EPS = 1e-5
RMS_EPS = 1e-5
LAM_INIT = 0.8 - 0.6 * math.exp(0.0)

LANES = 128
HALF = D_MODEL // 2
ROW_TILE = 256
PROJ_TILE = 512
ATT_TILE = 256
HG_TILE = 256
DMA_UNROLL = 4
VMEM_LIMIT = 56 * 1024 * 1024

_NT = (((1,), (1,)), ((), ()))
_TN = (((0,), (0,)), ((), ()))

f32 = jnp.float32
bf16 = jnp.bfloat16
u32 = jnp.uint32
i32 = jnp.int32


def _layer_norm(x, g, b):
    mu = jnp.mean(x, axis=-1, keepdims=True)
    xc = x - mu
    var = jnp.mean(xc * xc, axis=-1, keepdims=True)
    return xc * lax.rsqrt(var + LN_EPS) * g + b


def _pack_rows(h):
    hb = h.astype(bf16).astype(f32)
    lo = lax.bitcast_convert_type(hb[:, :HALF], u32) >> 16
    hi = lax.bitcast_convert_type(hb[:, HALF:], u32)
    return hi | lo


def _unpack_rows(w):
    left = lax.bitcast_convert_type(w << 16, f32)
    right = lax.bitcast_convert_type(w & jnp.uint32(0xFFFF0000), f32)
    return left, right


def _rope(x, cos, sin_signed, first_half):
    fwd = pltpu.roll(x, 32, axis=1)
    bwd = pltpu.roll(x, 96, axis=1)
    return x * cos + jnp.where(first_half, bwd, fwd) * sin_signed


def _in_proj_kernel(x_ref, g_ref, b_ref, w_ref, cq_ref, sq_ref, ck_ref, sk_ref, qkv_ref, hg_ref):
    h = _layer_norm(x_ref[...], g_ref[...], b_ref[...]).astype(bf16)
    lane = lax.broadcasted_iota(i32, (h.shape[0], LANES), 1)
    first_half = (lane & 63) < 32
    for c in range(2 * A_HEADS):
        acc = jnp.dot(h, w_ref[:, c * LANES:(c + 1) * LANES], preferred_element_type=f32)
        if c < A_HEADS:
            r = _rope(acc, cq_ref[...], sq_ref[...], first_half)
        else:
            r = _rope(acc, ck_ref[...], sk_ref[...], first_half)
        qkv_ref[:, c * LANES:(c + 1) * LANES] = r.astype(bf16)
    qkv_ref[:, 2 * A_W:3 * A_W] = jnp.dot(
        h, w_ref[:, 2 * A_W:3 * A_W], preferred_element_type=f32).astype(bf16)
    for c in range(4):
        lo = 3 * A_W + c * B_W
        hg_ref[:, c * B_W:(c + 1) * B_W] = jnp.dot(h, w_ref[:, lo:lo + B_W], preferred_element_type=f32)


def _in_proj(x2, g, b, w_bf, tabs, tile, tab_blocks):
    n = x2.shape[0]
    tab_spec = pl.BlockSpec((tile, LANES), lambda i: (i % tab_blocks, 0))
    vec = pl.BlockSpec((1, D_MODEL), lambda i: (0, 0))
    return pl.pallas_call(
        _in_proj_kernel,
        grid=(n // tile,),
        in_specs=[
            pl.BlockSpec((tile, D_MODEL), lambda i: (i, 0)),
            vec, vec,
            pl.BlockSpec((D_MODEL, D_IN), lambda i: (0, 0)),
            tab_spec, tab_spec, tab_spec, tab_spec,
        ],
        out_specs=[
            pl.BlockSpec((tile, 3 * A_W), lambda i: (i, 0)),
            pl.BlockSpec((tile, 4 * B_W), lambda i: (i, 0)),
        ],
        out_shape=[
            jax.ShapeDtypeStruct((n, 3 * A_W), bf16),
            jax.ShapeDtypeStruct((n, 4 * B_W), f32),
        ],
        compiler_params=pltpu.CompilerParams(
            dimension_semantics=("arbitrary",), vmem_limit_bytes=VMEM_LIMIT),
        name="in_proj",
    )(x2, g, b, w_bf, *tabs)


def _attn_kernel(lam_ref, q_ref, k_ref, v_ref, km_ref, vm_ref, g_ref, o_ref, m_scr, l_scr, acc_scr):
    qi = pl.program_id(2)
    tq = ATT_TILE
    q = q_ref[...]
    lane = lax.broadcasted_iota(i32, (tq, LANES), 1)
    zero = jnp.zeros_like(q)
    qs = jnp.concatenate([jnp.where(lane < A_HEAD_DIM, q, zero),
                          jnp.where(lane >= A_HEAD_DIM, q, zero)], axis=0)

    def update(s, v):
        m_old = m_scr[...]
        m_new = jnp.maximum(m_old, jnp.max(s, axis=0, keepdims=True))
        corr = jnp.exp(m_old - m_new)
        p = jnp.exp(s - m_new)
        l_scr[...] = l_scr[...] * corr + jnp.sum(p, axis=0, keepdims=True)
        acc_scr[...] = acc_scr[...] * corr + lax.dot_general(v, p.astype(bf16), _TN, preferred_element_type=f32)
        m_scr[...] = m_new

    s = lax.dot_general(km_ref[...], qs, _NT, preferred_element_type=f32)
    m0 = jnp.max(s, axis=0, keepdims=True)
    p0 = jnp.exp(s - m0)
    m_scr[...] = m0
    l_scr[...] = jnp.sum(p0, axis=0, keepdims=True)
    acc_scr[...] = lax.dot_general(vm_ref[...], p0.astype(bf16), _TN, preferred_element_type=f32)

    def full_tile(j, carry):
        start = pl.multiple_of(j * tq, tq)
        kj = k_ref[pl.ds(start, tq), :]
        update(lax.dot_general(kj, qs, _NT, preferred_element_type=f32), v_ref[pl.ds(start, tq), :])
        return carry

    lax.fori_loop(0, qi, full_tile, 0)

    start = pl.multiple_of(qi * tq, tq)
    s = lax.dot_general(k_ref[pl.ds(start, tq), :], qs, _NT, preferred_element_type=f32)
    key = lax.broadcasted_iota(i32, s.shape, 0)
    qry = lax.broadcasted_iota(i32, s.shape, 1)
    visible = ((qry & (tq - 1)) >> 6) >= (key >> 6)
    update(jnp.where(visible, s, -jnp.inf), v_ref[pl.ds(start, tq), :])

    acc = acc_scr[...]
    inv = 1.0 / l_scr[...]
    o = acc[:, :tq] * inv[:, :tq] - lam_ref[0] * (acc[:, tq:] * inv[:, tq:])
    ms = jnp.mean(o * o, axis=0, keepdims=True)
    o = o * lax.rsqrt(ms + RMS_EPS) * (g_ref[...] * (1.0 - LAM_INIT))
    o_ref[...] = o.T.astype(bf16)


def _attention(lam, qkv, qkv_meta, subln_g, batch, seq):
    nq = seq // ATT_TILE
    return pl.pallas_call(
        _attn_kernel,
        grid_spec=pltpu.PrefetchScalarGridSpec(
            num_scalar_prefetch=1,
            grid=(batch, A_HEADS, nq),
            in_specs=[
                pl.BlockSpec((ATT_TILE, LANES), lambda b, h, i, lam: (b * nq + i, h)),
                pl.BlockSpec((seq, LANES), lambda b, h, i, lam: (b, A_HEADS + h)),
                pl.BlockSpec((seq, LANES), lambda b, h, i, lam: (b, 2 * A_HEADS + h)),
                pl.BlockSpec((N_META, LANES), lambda b, h, i, lam: (0, A_HEADS + h)),
                pl.BlockSpec((N_META, LANES), lambda b, h, i, lam: (0, 2 * A_HEADS + h)),
                pl.BlockSpec((LANES, 1), lambda b, h, i, lam: (0, 0)),
            ],
            out_specs=pl.BlockSpec((ATT_TILE, LANES), lambda b, h, i, lam: (b * nq + i, h)),
            scratch_shapes=[
                pltpu.VMEM((1, 2 * ATT_TILE), f32),
                pltpu.VMEM((1, 2 * ATT_TILE), f32),
                pltpu.VMEM((LANES, 2 * ATT_TILE), f32),
            ],
        ),
        out_shape=jax.ShapeDtypeStruct((batch * seq, A_W), bf16),
        compiler_params=pltpu.CompilerParams(
            dimension_semantics=("arbitrary", "arbitrary", "arbitrary"), vmem_limit_bytes=VMEM_LIMIT),
        name="attn",
    )(lam, qkv, qkv, qkv, qkv_meta, qkv_meta, subln_g)


def _split3(x):
    a = x.astype(bf16)
    r = x - a.astype(f32)
    b = r.astype(bf16)
    c = (r - b.astype(f32)).astype(bf16)
    return a, b, c


def _chunk_cumsum(tri, x):
    a, b, c = _split3(x)
    out = jnp.dot(tri, a, preferred_element_type=f32)
    out = out + jnp.dot(tri, b, preferred_element_type=f32)
    return out + jnp.dot(tri, c, preferred_element_type=f32)


def _gates(z, lb):
    log_f = jnp.log(lb + (1.0 - lb) * jax.nn.sigmoid(z))
    key = (1.0 - lb) * jax.nn.sigmoid(-z)
    return log_f, key


def _hgrn_kernel(q_ref, f_ref, i_ref, g_ref, fm_ref, im_ref, lb_ref, ng_ref, o_ref,
                 oin_scr, qhat_scr, ut_scr, dec_scr):
    lb = lb_ref[...]
    n_tiles = q_ref.shape[0] // HG_TILE
    per_tile = HG_TILE // CHUNK

    lfm, km = _gates(fm_ref[...], lb)
    r16 = lax.broadcasted_iota(i32, (N_META, N_META), 0)
    c16 = lax.broadcasted_iota(i32, (N_META, N_META), 1)
    bm = _chunk_cumsum((c16 <= r16).astype(bf16), lfm)
    kdm = km * jnp.exp(bm[N_META - 1:N_META, :] - bm)
    st = lax.dot_general(im_ref[...].astype(bf16), kdm.astype(bf16), _TN, preferred_element_type=f32)

    row = lax.broadcasted_iota(i32, (HG_TILE, HG_TILE), 0)
    col = lax.broadcasted_iota(i32, (HG_TILE, HG_TILE), 1)
    causal = ((row >> 6) == (col >> 6)) & (col <= row)
    tri = causal.astype(bf16)

    for t in range(n_tiles):
        rows = slice(t * HG_TILE, (t + 1) * HG_TILE)
        qv = q_ref[rows, :]
        qh = qv * jax.nn.sigmoid(qv) * (B_KEY_DIM ** -0.5)
        log_f, kh = _gates(f_ref[rows, :], lb)
        b = _chunk_cumsum(tri, log_f)
        b3 = b.reshape(per_tile, CHUNK, LANES)
        b_mid = jnp.broadcast_to(b3[:, CHUNK // 2:CHUNK // 2 + 1, :], b3.shape).reshape(HG_TILE, LANES)
        b_last3 = b3[:, CHUNK - 1:CHUNK, :]
        b_last = jnp.broadcast_to(b_last3, b3.shape).reshape(HG_TILE, LANES)
        qt = (qh * jnp.exp(b - b_mid)).astype(bf16)
        kt = (kh * jnp.exp(jnp.minimum(b_mid - b, 80.0))).astype(bf16)
        a = lax.dot_general(qt, kt, _NT, preferred_element_type=f32)
        a = jnp.where(causal, a, 0.0).astype(bf16)
        vv = i_ref[rows, :].astype(bf16)
        oin_scr[rows, :] = jnp.dot(a, vv, preferred_element_type=f32)
        qhat_scr[rows, :] = (qh * jnp.exp(b)).astype(bf16)
        kd = (kh * jnp.exp(b_last - b)).astype(bf16)
        for c in range(per_tile):
            cr = slice(c * CHUNK, (c + 1) * CHUNK)
            n = t * per_tile + c
            ut_scr[n] = lax.dot_general(vv[cr], kd[cr], _TN, preferred_element_type=f32)
            dec_scr[n] = jnp.exp(b_last3[c])

    ng = ng_ref[...]
    for n in range(n_tiles * per_tile):
        rows = slice(n * CHUNK, (n + 1) * CHUNK)
        o = oin_scr[rows, :] + lax.dot_general(qhat_scr[rows, :], st.astype(bf16), _NT,
                                               preferred_element_type=f32)
        ms = jnp.mean(o * o, axis=-1, keepdims=True)
        gv = g_ref[rows, :]
        o_ref[rows, :] = (o * lax.rsqrt(ms + RMS_EPS) * ng * (gv * jax.nn.sigmoid(gv))).astype(bf16)
        st = st * dec_scr[n] + ut_scr[n]


def _hgrn(hg, hg_meta, lb, norm_g, batch, seq):
    n_chunks = seq // CHUNK

    def col(c):
        return pl.BlockSpec((seq, LANES), lambda b, h: (b, c * B_HEADS + h))

    def mcol(c):
        return pl.BlockSpec((N_META, LANES), lambda b, h: (0, c * B_HEADS + h))

    return pl.pallas_call(
        _hgrn_kernel,
        grid=(batch, B_HEADS),
        in_specs=[
            col(0), col(1), col(2), col(3), mcol(1), mcol(2),
            pl.BlockSpec((None, 1, LANES), lambda b, h: (h, 0, 0)),
            pl.BlockSpec((1, LANES), lambda b, h: (0, 0)),
        ],
        out_specs=pl.BlockSpec((seq, LANES), lambda b, h: (b, h)),
        out_shape=jax.ShapeDtypeStruct((batch * seq, B_W), bf16),
        scratch_shapes=[
            pltpu.VMEM((seq, LANES), f32),
            pltpu.VMEM((seq, LANES), bf16),
            pltpu.VMEM((n_chunks, LANES, LANES), f32),
            pltpu.VMEM((n_chunks, 1, LANES), f32),
        ],
        compiler_params=pltpu.CompilerParams(
            dimension_semantics=("arbitrary", "arbitrary"), vmem_limit_bytes=VMEM_LIMIT),
        name="hgrn",
    )(hg, hg, hg, hg, hg_meta, hg_meta, lb, norm_g)


def _mix_route_kernel(x_ref, g0_ref, b0_ref, ya_ref, yb_ref, wo_ref, g1_ref, b1_ref,
                      wr_hi_ref, wr_lo_ref, br_ref,
                      h_ref, hp_ref, idx_ref, gate_ref, rank_ref, cnt_ref, carry_scr):
    step = pl.program_id(0)

    @pl.when(step == 0)
    def _():
        carry_scr[...] = jnp.zeros_like(carry_scr)

    h0 = _layer_norm(x_ref[...], g0_ref[...], b0_ref[...])
    mix = jnp.dot(ya_ref[...], wo_ref[:A_W, :], preferred_element_type=f32)
    mix = mix + jnp.dot(yb_ref[...], wo_ref[A_W:, :], preferred_element_type=f32)
    h1 = _layer_norm(DEEPNORM_ALPHA * h0 + mix, g1_ref[...], b1_ref[...])
    h_ref[...] = h1
    hp_ref[...] = _pack_rows(h1)

    h_hi = h1.astype(bf16)
    h_lo = (h1 - h_hi.astype(f32)).astype(bf16)
    logits = jnp.dot(h_hi, wr_hi_ref[...], preferred_element_type=f32)
    logits = logits + jnp.dot(h_lo, wr_hi_ref[...], preferred_element_type=f32)
    logits = logits + jnp.dot(h_hi, wr_lo_ref[...], preferred_element_type=f32)
    logits = logits + br_ref[...]

    tm = logits.shape[0]
    lane = lax.broadcasted_iota(i32, (tm, N_EXPERTS), 1)
    work = logits
    vals, hots = [], []
    sel = jnp.zeros((tm, N_EXPERTS), f32)
    for k in range(TOP_K):
        m = jnp.max(work, axis=1, keepdims=True)
        first = jnp.min(jnp.where(work == m, lane, N_EXPERTS), axis=1, keepdims=True)
        hot = lane == first
        vals.append(m)
        hots.append(hot)
        idx_ref[:, k:k + 1] = first
        sel = sel + hot.astype(f32)
        work = jnp.where(hot, -jnp.inf, work)

    es = [jnp.exp(v - vals[0]) for v in vals]
    denom = es[0] + es[1] + es[2] + es[3]
    for k in range(TOP_K):
        gate_ref[:, k:k + 1] = es[k] / denom

    r = lax.broadcasted_iota(i32, (tm, tm), 0)
    c = lax.broadcasted_iota(i32, (tm, tm), 1)
    incl = jnp.dot((c <= r).astype(bf16), sel.astype(bf16), preferred_element_type=f32)
    carry = carry_scr[...]
    excl = incl - sel + carry
    for k in range(TOP_K):
        rank_ref[:, k:k + 1] = jnp.sum(jnp.where(hots[k], excl, 0.0), axis=1, keepdims=True).astype(i32)
    carry = carry + incl[tm - 1:tm, :]
    carry_scr[...] = carry
    cnt_ref[...] = carry.astype(i32)


def _mix_route(x2, g0, b0, ya, yb, wo, g1, b1, wr_hi, wr_lo, br):
    n = x2.shape[0]
    tm = ROW_TILE

    def full(shape):
        return pl.BlockSpec(shape, lambda i: (0,) * len(shape))

    vec = full((1, D_MODEL))
    small = pl.BlockSpec((tm, TOP_K), lambda i: (i, 0))
    return pl.pallas_call(
        _mix_route_kernel,
        grid=(n // tm,),
        in_specs=[
            pl.BlockSpec((tm, D_MODEL), lambda i: (i, 0)), vec, vec,
            pl.BlockSpec((tm, A_W), lambda i: (i, 0)),
            pl.BlockSpec((tm, B_W), lambda i: (i, 0)),
            full((D_MODEL, D_MODEL)), vec, vec,
            full((D_MODEL, N_EXPERTS)), full((D_MODEL, N_EXPERTS)), full((1, N_EXPERTS)),
        ],
        out_specs=[
            pl.BlockSpec((tm, D_MODEL), lambda i: (i, 0)),
            pl.BlockSpec((tm, HALF), lambda i: (i, 0)),
            small, small, small,
            full((1, N_EXPERTS)),
        ],
        out_shape=[
            jax.ShapeDtypeStruct((n, D_MODEL), f32),
            jax.ShapeDtypeStruct((n, HALF), u32),
            jax.ShapeDtypeStruct((n, TOP_K), i32),
            jax.ShapeDtypeStruct((n, TOP_K), f32),
            jax.ShapeDtypeStruct((n, TOP_K), i32),
            jax.ShapeDtypeStruct((1, N_EXPERTS), i32),
        ],
        scratch_shapes=[pltpu.VMEM((1, N_EXPERTS), f32)],
        compiler_params=pltpu.CompilerParams(
            dimension_semantics=("arbitrary",), vmem_limit_bytes=VMEM_LIMIT),
        name="mix_route",
    )(x2, g0, b0, ya, yb, wo, g1, b1, wr_hi, wr_lo, br)


def _row_copy(src, src_row, dst, dst_row, sem):
    return pltpu.make_async_copy(src.at[pl.ds(src_row, 1), :], dst.at[pl.ds(dst_row, 1), :], sem)


def _dispatch_kernel(hp_ref, dest_hbm, zero_hbm, rows_hbm, idx_smem, idx_sem, row_sem):
    del zero_hbm
    step = pl.program_id(0)
    n_assign = ROW_TILE * TOP_K
    idx_copy = pltpu.make_async_copy(dest_hbm.at[step], idx_smem, idx_sem)
    idx_copy.start()
    idx_copy.wait()

    def issue(t, carry):
        for k in range(TOP_K):
            _row_copy(hp_ref, t, rows_hbm, idx_smem[t * TOP_K + k], row_sem).start()
        return carry

    lax.fori_loop(0, ROW_TILE, issue, 0, unroll=DMA_UNROLL)

    for k in range(TOP_K):
        pltpu.make_async_copy(hp_ref, rows_hbm.at[pl.ds(0, ROW_TILE), :], row_sem).wait()


def _dispatch(hp, dest_tiles, n_rows):
    n = hp.shape[0]
    zeros = jnp.zeros((n_rows, HALF), u32)
    return pl.pallas_call(
        _dispatch_kernel,
        grid=(n // ROW_TILE,),
        in_specs=[
            pl.BlockSpec((ROW_TILE, HALF), lambda i: (i, 0)),
            pl.BlockSpec(memory_space=pl.ANY),
            pl.BlockSpec(memory_space=pl.ANY),
        ],
        out_specs=pl.BlockSpec(memory_space=pl.ANY),
        out_shape=jax.ShapeDtypeStruct((n_rows, HALF), u32),
        scratch_shapes=[
            pltpu.SMEM((ROW_TILE * TOP_K,), i32),
            pltpu.SemaphoreType.DMA(()),
            pltpu.SemaphoreType.DMA(()),
        ],
        input_output_aliases={2: 0},
        compiler_params=pltpu.CompilerParams(
            dimension_semantics=("arbitrary",), vmem_limit_bytes=VMEM_LIMIT),
        name="dispatch",
    )(hp, dest_tiles, zeros)


def _experts_kernel(be_ref, na_ref, x_ref, wgu_ref, bg_ref, bu_ref, wdn_ref, bdn_ref, y_ref,
                    wg_scr, wu_scr, wd_scr, act_scr):
    j = pl.program_id(0)
    prev = be_ref[jnp.maximum(j - 1, 0)]
    fresh = jnp.logical_or(j == 0, be_ref[j] != prev)

    @pl.when(jnp.logical_and(fresh, j < na_ref[0]))
    def _():
        src = lax.broadcasted_iota(i32, (2 * LANES, 2 * LANES), 0)
        dst = lax.broadcasted_iota(i32, (2 * LANES, 2 * LANES), 1)
        perm = (src == jnp.where(dst < LANES, 2 * dst, 2 * (dst - LANES) + 1)).astype(bf16)
        for t in range(2 * D_EXPERT // (2 * LANES)):
            w = wgu_ref[:, t * 2 * LANES:(t + 1) * 2 * LANES].astype(bf16)
            sep = jnp.dot(w, perm, preferred_element_type=f32)
            wg_scr[:, t * LANES:(t + 1) * LANES] = sep[:, :LANES].astype(bf16)
            wu_scr[:, t * LANES:(t + 1) * LANES] = sep[:, LANES:].astype(bf16)
        wd_scr[...] = wdn_ref[...].astype(bf16)

    @pl.when(j < na_ref[0])
    def _():
        left, right = _unpack_rows(x_ref[...])
        xl = left.astype(bf16)
        xr = right.astype(bf16)
        fc = 256
        for c in range(D_EXPERT // fc):
            cols = slice(c * fc, (c + 1) * fc)
            gate = jnp.dot(xl, wg_scr[:HALF, cols], preferred_element_type=f32)
            gate = gate + jnp.dot(xr, wg_scr[HALF:, cols], preferred_element_type=f32) + bg_ref[:, cols]
            up = jnp.dot(xl, wu_scr[:HALF, cols], preferred_element_type=f32)
            up = up + jnp.dot(xr, wu_scr[HALF:, cols], preferred_element_type=f32) + bu_ref[:, cols]
            gate = jnp.minimum(gate, SWIGLU_LIMIT)
            up = jnp.clip(up, -SWIGLU_LIMIT, SWIGLU_LIMIT)
            act_scr[:, cols] = ((up + 1.0) * gate * jax.nn.sigmoid(gate * SWIGLU_ALPHA)).astype(bf16)
        y = jnp.dot(act_scr[...], wd_scr[...], preferred_element_type=f32) + bdn_ref[...]
        y_ref[...] = _pack_rows(y)

    @pl.when(j >= na_ref[0])
    def _():
        y_ref[...] = jnp.zeros_like(y_ref)


def _experts(block_expert, n_active, rows, w_gu, bg, bu, w_dn, b_dn):
    n_blocks = rows.shape[0] // EXPERT_BLOCK

    def blk(j, be, na):
        return (jnp.minimum(j, na[0] - 1), 0)

    def per_expert(shape):
        return pl.BlockSpec((None,) + shape, lambda j, be, na: (be[j], 0, 0))

    return pl.pallas_call(
        _experts_kernel,
        grid_spec=pltpu.PrefetchScalarGridSpec(
            num_scalar_prefetch=2,
            grid=(n_blocks,),
            in_specs=[
                pl.BlockSpec((EXPERT_BLOCK, HALF), blk),
                per_expert((D_MODEL, 2 * D_EXPERT)),
                per_expert((1, D_EXPERT)),
                per_expert((1, D_EXPERT)),
                per_expert((D_EXPERT, D_MODEL)),
                per_expert((1, D_MODEL)),
            ],
            out_specs=pl.BlockSpec((EXPERT_BLOCK, HALF), lambda j, be, na: (j, 0)),
            scratch_shapes=[
                pltpu.VMEM((D_MODEL, D_EXPERT), bf16),
                pltpu.VMEM((D_MODEL, D_EXPERT), bf16),
                pltpu.VMEM((D_EXPERT, D_MODEL), bf16),
                pltpu.VMEM((EXPERT_BLOCK, D_EXPERT), bf16),
            ],
        ),
        out_shape=jax.ShapeDtypeStruct(rows.shape, u32),
        compiler_params=pltpu.CompilerParams(
            dimension_semantics=("arbitrary",), vmem_limit_bytes=VMEM_LIMIT),
        name="experts",
    )(block_expert, n_active, rows, w_gu, bg, bu, w_dn, b_dn)


def _combine_kernel(h_ref, gate_ref, g2_ref, b2_ref, dest_hbm, y_hbm, o_ref,
                    idx_smem, ybuf, idx_sem, row_sem):
    step = pl.program_id(0)
    n_assign = ROW_TILE * TOP_K
    idx_copy = pltpu.make_async_copy(dest_hbm.at[step], idx_smem, idx_sem)
    idx_copy.start()
    idx_copy.wait()

    def issue(t, carry):
        for k in range(TOP_K):
            _row_copy(y_hbm, idx_smem[t * TOP_K + k], ybuf, k * ROW_TILE + t, row_sem).start()
        return carry

    lax.fori_loop(0, ROW_TILE, issue, 0, unroll=DMA_UNROLL)

    for k in range(TOP_K):
        rows = pl.ds(k * ROW_TILE, ROW_TILE)
        pltpu.make_async_copy(y_hbm.at[pl.ds(0, ROW_TILE), :], ybuf.at[rows, :], row_sem).wait()

    gates = gate_ref[...]
    left = jnp.zeros((ROW_TILE, HALF), f32)
    right = jnp.zeros((ROW_TILE, HALF), f32)
    for k in range(TOP_K):
        yl, yr = _unpack_rows(ybuf[k * ROW_TILE:(k + 1) * ROW_TILE, :])
        gk = gates[:, k:k + 1]
        left = left + yl * gk
        right = right + yr * gk
    ffn = jnp.concatenate([left, right], axis=1)
    o_ref[...] = _layer_norm(DEEPNORM_ALPHA * h_ref[...] + ffn, g2_ref[...], b2_ref[...])


def _combine(h1, gates, g2, b2, dest_tiles, y_rows):
    n = h1.shape[0]
    vec = pl.BlockSpec((1, D_MODEL), lambda i: (0, 0))
    return pl.pallas_call(
        _combine_kernel,
        grid=(n // ROW_TILE,),
        in_specs=[
            pl.BlockSpec((ROW_TILE, D_MODEL), lambda i: (i, 0)),
            pl.BlockSpec((ROW_TILE, TOP_K), lambda i: (i, 0)),
            vec, vec,
            pl.BlockSpec(memory_space=pl.ANY),
            pl.BlockSpec(memory_space=pl.ANY),
        ],
        out_specs=pl.BlockSpec((ROW_TILE, D_MODEL), lambda i: (i, 0)),
        out_shape=jax.ShapeDtypeStruct((n, D_MODEL), f32),
        scratch_shapes=[
            pltpu.SMEM((ROW_TILE * TOP_K,), i32),
            pltpu.VMEM((TOP_K * ROW_TILE, HALF), u32),
            pltpu.SemaphoreType.DMA(()),
            pltpu.SemaphoreType.DMA(()),
        ],
        compiler_params=pltpu.CompilerParams(
            dimension_semantics=("arbitrary",), vmem_limit_bytes=VMEM_LIMIT),
        name="combine",
    )(h1, gates, g2, b2, dest_tiles, y_rows)


def _rope_tables(pos, scale):
    inv = 1.0 / (ROPE_THETA ** (jnp.arange(0, A_HEAD_DIM, 2, dtype=f32) / A_HEAD_DIM))
    ang = pos.astype(f32)[:, None] * inv[None, :]
    ang = jnp.concatenate([ang, ang, ang, ang], axis=-1)
    sign = jnp.where((jnp.arange(LANES) & 63) < 32, -1.0, 1.0).astype(f32)
    return jnp.cos(ang) * scale, jnp.sin(ang) * sign * scale


def kernel(x, meta_tokens, ln_emb_g, ln_emb_b, w_in, lambda_q1, lambda_k1, lambda_q2, lambda_k2,
           subln_g, hgrn_lb_table, hgrn_norm_g, w_out, ln1_g, ln1_b, w_router, b_router,
           w_gate_up, b_gate_up, w_down, b_down, ln2_g, ln2_b):
    batch, seq, _ = x.shape
    n_tok = batch * seq
    x2 = x.reshape(n_tok, D_MODEL)
    row = lambda v: v.reshape(1, -1).astype(f32)

    w_in_bf = w_in[0].astype(bf16)
    lam = (jnp.exp(jnp.sum(lambda_q1[0].astype(f32) * lambda_k1[0].astype(f32)))
           - jnp.exp(jnp.sum(lambda_q2[0].astype(f32) * lambda_k2[0].astype(f32))) + LAM_INIT).reshape(1)
    lb = jnp.cumsum(jax.nn.softmax(hgrn_lb_table.astype(f32), axis=0), axis=0)[0].reshape(B_HEADS, 1, LANES)
    q_scale = A_HEAD_DIM ** -0.5
    pos_x = N_META + jnp.arange(seq)
    pos_m = jnp.arange(N_META)
    cq, sq = _rope_tables(pos_x, q_scale)
    ck, sk = _rope_tables(pos_x, 1.0)
    cqm, sqm = _rope_tables(pos_m, q_scale)
    ckm, skm = _rope_tables(pos_m, 1.0)
    wr = w_router[0].astype(f32)
    wr_hi = wr.astype(bf16)
    wr_lo = (wr - wr_hi.astype(f32)).astype(bf16)
    bg = b_gate_up[0][:, 0::2].reshape(N_EXPERTS, 1, D_EXPERT)
    bu = b_gate_up[0][:, 1::2].reshape(N_EXPERTS, 1, D_EXPERT)
    bdn = b_down[0].reshape(N_EXPERTS, 1, D_MODEL)

    g0, b0 = row(ln_emb_g), row(ln_emb_b)
    qkv, hg = _in_proj(x2, g0, b0, w_in_bf, (cq, sq, ck, sk), PROJ_TILE, seq // PROJ_TILE)
    qkv_m, hg_m = _in_proj(meta_tokens, g0, b0, w_in_bf, (cqm, sqm, ckm, skm), N_META, 1)

    ya = _attention(lam, qkv, qkv_m, subln_g[0].reshape(-1, 1).astype(f32), batch, seq)
    yb = _hgrn(hg, hg_m, lb, row(hgrn_norm_g[0]), batch, seq)

    h1, hp, top_idx, gates, rank, counts = _mix_route(
        x2, g0, b0, ya, yb, w_out[0].astype(bf16), row(ln1_g[0]), row(ln1_b[0]),
        wr_hi, wr_lo, row(b_router[0]))

    counts = counts.reshape(N_EXPERTS)
    blocks = (counts + EXPERT_BLOCK - 1) // EXPERT_BLOCK
    cum_blocks = jnp.cumsum(blocks)
    start_pad = (cum_blocks - blocks) * EXPERT_BLOCK
    n_blocks = n_tok * TOP_K // EXPERT_BLOCK + N_EXPERTS
    n_active = cum_blocks[-1:].astype(i32)
    jb = jnp.arange(n_blocks, dtype=i32)
    block_expert = jnp.sum((cum_blocks[None, :] <= jnp.minimum(jb, n_active - 1)[:, None]).astype(i32), axis=1)
    block_expert = jnp.minimum(block_expert, N_EXPERTS - 1).astype(i32)
    hot = top_idx[:, :, None] == jnp.arange(N_EXPERTS, dtype=i32)[None, None, :]
    dest = jnp.sum(jnp.where(hot, start_pad.astype(i32)[None, None, :], 0), axis=-1) + rank
    dest_tiles = dest.reshape(n_tok // ROW_TILE, ROW_TILE * TOP_K)

    x_rows = _dispatch(hp, dest_tiles, n_blocks * EXPERT_BLOCK)
    y_rows = _experts(block_expert, n_active, x_rows, w_gate_up[0], bg, bu, w_down[0], bdn)
    out = _combine(h1, gates, row(ln2_g[0]), row(ln2_b[0]), dest_tiles, y_rows)
    return out.reshape(batch, seq, D_MODEL)
```

```python
import functools
import math

import jax
import jax.numpy as jnp
import numpy as np
from jax import lax
from jax.experimental import pallas as pl
from jax.experimental.pallas import tpu as pltpu

D_MODEL = 1024
N_META = 16
CHUNK = 64
A_HEADS = 4
A_HEAD_DIM = 64
B_HEADS = 4
B_KEY_DIM = 128
ROPE_THETA = 10000.0
A_W = 512
B_W = 512
D_IN = 3 * A_W + 4 * B_W
N_EXPERTS = 32
TOP_K = 4
D_EXPERT = 1024
SWIGLU_ALPHA = 1.702
SWIGLU_LIMIT = 7.0
EXPERT_BLOCK = 256
DEEPNORM_ALPHA = 2.0 ** 0.25
LN_EPS = 1e-5
RMS_EPS = 1e-5
LAM_INIT = 0.8 - 0.6 * math.exp(0.0)

LANES = 128
HALF = D_MODEL // 2
ROW_TILE = 256
PROJ_TILE = 512
ATT_TILE = 256
HG_TILE = 256
DMA_UNROLL = 4
VMEM_LIMIT = 56 * 1024 * 1024

_NT = (((1,), (1,)), ((), ()))
_TN = (((0,), (0,)), ((), ()))

f32 = jnp.float32
bf16 = jnp.bfloat16
u32 = jnp.uint32
i32 = jnp.int32


def _layer_norm(x, g, b):
    mu = jnp.mean(x, axis=-1, keepdims=True)
    xc = x - mu
    var = jnp.mean(xc * xc, axis=-1, keepdims=True)
    return xc * lax.rsqrt(var + LN_EPS) * g + b


def _pack_rows(h):
    hb = h.astype(bf16).astype(f32)
    lo = lax.bitcast_convert_type(hb[:, :HALF], u32) >> 16
    hi = lax.bitcast_convert_type(hb[:, HALF:], u32)
    return hi | lo


def _unpack_rows(w):
    left = lax.bitcast_convert_type(w << 16, f32)
    right = lax.bitcast_convert_type(w & jnp.uint32(0xFFFF0000), f32)
    return left, right


def _rope(x, cos, sin_signed, first_half):
    fwd = pltpu.roll(x, 32, axis=1)
    bwd = pltpu.roll(x, 96, axis=1)
    return x * cos + jnp.where(first_half, bwd, fwd) * sin_signed


def _in_proj_kernel(x_ref, g_ref, b_ref, w_ref, cq_ref, sq_ref, ck_ref, sk_ref, qkv_ref, hg_ref):
    h = _layer_norm(x_ref[...], g_ref[...], b_ref[...]).astype(bf16)
    lane = lax.broadcasted_iota(i32, (h.shape[0], LANES), 1)
    first_half = (lane & 63) < 32
    for c in range(2 * A_HEADS):
        acc = jnp.dot(h, w_ref[:, c * LANES:(c + 1) * LANES], preferred_element_type=f32)
        if c < A_HEADS:
            r = _rope(acc, cq_ref[...], sq_ref[...], first_half)
        else:
            r = _rope(acc, ck_ref[...], sk_ref[...], first_half)
        qkv_ref[:, c * LANES:(c + 1) * LANES] = r.astype(bf16)
    qkv_ref[:, 2 * A_W:3 * A_W] = jnp.dot(
        h, w_ref[:, 2 * A_W:3 * A_W], preferred_element_type=f32).astype(bf16)
    for c in range(4):
        lo = 3 * A_W + c * B_W
        hg_ref[:, c * B_W:(c + 1) * B_W] = jnp.dot(h, w_ref[:, lo:lo + B_W], preferred_element_type=f32)


def _in_proj(x2, g, b, w_bf, tabs, tile, tab_blocks):
    n = x2.shape[0]
    tab_spec = pl.BlockSpec((tile, LANES), lambda i: (i % tab_blocks, 0))
    vec = pl.BlockSpec((1, D_MODEL), lambda i: (0, 0))
    return pl.pallas_call(
        _in_proj_kernel,
        grid=(n // tile,),
        in_specs=[
            pl.BlockSpec((tile, D_MODEL), lambda i: (i, 0)),
            vec, vec,
            pl.BlockSpec((D_MODEL, D_IN), lambda i: (0, 0)),
            tab_spec, tab_spec, tab_spec, tab_spec,
        ],
        out_specs=[
            pl.BlockSpec((tile, 3 * A_W), lambda i: (i, 0)),
            pl.BlockSpec((tile, 4 * B_W), lambda i: (i, 0)),
        ],
        out_shape=[
            jax.ShapeDtypeStruct((n, 3 * A_W), bf16),
            jax.ShapeDtypeStruct((n, 4 * B_W), f32),
        ],
        compiler_params=pltpu.CompilerParams(
            dimension_semantics=("arbitrary",), vmem_limit_bytes=VMEM_LIMIT),
        name="in_proj",
    )(x2, g, b, w_bf, *tabs)


def _attn_kernel(lam_ref, q_ref, k_ref, v_ref, km_ref, vm_ref, g_ref, o_ref):
    tq = ATT_TILE
    lane = lax.broadcasted_iota(i32, (tq, LANES), 1)
    key = lax.broadcasted_iota(i32, (tq, 2 * tq), 0)
    qry = lax.broadcasted_iota(i32, (tq, 2 * tq), 1)
    visible = ((qry & (tq - 1)) >> 6) >= (key >> 6)
    gain = g_ref[...] * (1.0 - LAM_INIT)
    km = km_ref[...]
    vm = vm_ref[...]
    for qi in range(q_ref.shape[0] // tq):
        q = q_ref[qi * tq:(qi + 1) * tq, :]
        zero = jnp.zeros_like(q)
        qs = jnp.concatenate([jnp.where(lane < A_HEAD_DIM, q, zero),
                              jnp.where(lane >= A_HEAD_DIM, q, zero)], axis=0)
        n = (qi + 1) * tq
        s = lax.dot_general(k_ref[0:n, :], qs, _NT, preferred_element_type=f32)
        diag = jnp.where(visible, s[n - tq:, :], -jnp.inf)
        s = diag if qi == 0 else jnp.concatenate([s[:n - tq, :], diag], axis=0)
        sm = lax.dot_general(km, qs, _NT, preferred_element_type=f32)
        m = jnp.maximum(jnp.max(s, axis=0, keepdims=True), jnp.max(sm, axis=0, keepdims=True))
        p = jnp.exp(s - m)
        pm = jnp.exp(sm - m)
        l = jnp.sum(p, axis=0, keepdims=True) + jnp.sum(pm, axis=0, keepdims=True)
        acc = lax.dot_general(v_ref[0:n, :], p.astype(bf16), _TN, preferred_element_type=f32)
        acc = acc + lax.dot_general(vm, pm.astype(bf16), _TN, preferred_element_type=f32)
        inv = 1.0 / l
        o = acc[:, :tq] * inv[:, :tq] - lam_ref[0] * (acc[:, tq:] * inv[:, tq:])
        ms = jnp.mean(o * o, axis=0, keepdims=True)
        o = o * lax.rsqrt(ms + RMS_EPS) * gain
        o_ref[qi * tq:(qi + 1) * tq, :] = o.T.astype(bf16)


def _attention(lam, qkv, qkv_meta, subln_g, batch, seq):
    return pl.pallas_call(
        _attn_kernel,
        grid_spec=pltpu.PrefetchScalarGridSpec(
            num_scalar_prefetch=1,
            grid=(batch, A_HEADS),
            in_specs=[
                pl.BlockSpec((seq, LANES), lambda b, h, lam: (b, h)),
                pl.BlockSpec((seq, LANES), lambda b, h, lam: (b, A_HEADS + h)),
                pl.BlockSpec((seq, LANES), lambda b, h, lam: (b, 2 * A_HEADS + h)),
                pl.BlockSpec((N_META, LANES), lambda b, h, lam: (0, A_HEADS + h)),
                pl.BlockSpec((N_META, LANES), lambda b, h, lam: (0, 2 * A_HEADS + h)),
                pl.BlockSpec((LANES, 1), lambda b, h, lam: (0, 0)),
            ],
            out_specs=pl.BlockSpec((seq, LANES), lambda b, h, lam: (b, h)),
        ),
        out_shape=jax.ShapeDtypeStruct((batch * seq, A_W), bf16),
        compiler_params=pltpu.CompilerParams(
            dimension_semantics=("arbitrary", "arbitrary"), vmem_limit_bytes=VMEM_LIMIT),
        name="attn",
    )(lam, qkv, qkv, qkv, qkv_meta, qkv_meta, subln_g)


def _split3(x):
    a = x.astype(bf16)
    r = x - a.astype(f32)
    b = r.astype(bf16)
    c = (r - b.astype(f32)).astype(bf16)
    return a, b, c


def _chunk_cumsum(tri, x):
    a, b, c = _split3(x)
    out = jnp.dot(tri, a, preferred_element_type=f32)
    out = out + jnp.dot(tri, b, preferred_element_type=f32)
    return out + jnp.dot(tri, c, preferred_element_type=f32)


def _gates(z, lb):
    log_f = jnp.log(lb + (1.0 - lb) * jax.nn.sigmoid(z))
    key = (1.0 - lb) * jax.nn.sigmoid(-z)
    return log_f, key


def _hgrn_kernel(q_ref, f_ref, i_ref, g_ref, fm_ref, im_ref, lb_ref, ng_ref, o_ref,
                 oin_scr, qhat_scr, ut_scr, dec_scr):
    lb = lb_ref[...]
    n_tiles = q_ref.shape[0] // HG_TILE
    per_tile = HG_TILE // CHUNK

    lfm, km = _gates(fm_ref[...], lb)
    r16 = lax.broadcasted_iota(i32, (N_META, N_META), 0)
    c16 = lax.broadcasted_iota(i32, (N_META, N_META), 1)
    bm = _chunk_cumsum((c16 <= r16).astype(bf16), lfm)
    kdm = km * jnp.exp(bm[N_META - 1:N_META, :] - bm)
    st = lax.dot_general(im_ref[...].astype(bf16), kdm.astype(bf16), _TN, preferred_element_type=f32)

    row = lax.broadcasted_iota(i32, (HG_TILE, HG_TILE), 0)
    col = lax.broadcasted_iota(i32, (HG_TILE, HG_TILE), 1)
    causal = ((row >> 6) == (col >> 6)) & (col <= row)
    tri = causal.astype(bf16)

    for t in range(n_tiles):
        rows = slice(t * HG_TILE, (t + 1) * HG_TILE)
        qv = q_ref[rows, :]
        qh = qv * jax.nn.sigmoid(qv) * (B_KEY_DIM ** -0.5)
        log_f, kh = _gates(f_ref[rows, :], lb)
        b = _chunk_cumsum(tri, log_f)
        b3 = b.reshape(per_tile, CHUNK, LANES)
        b_mid = jnp.broadcast_to(b3[:, CHUNK // 2:CHUNK // 2 + 1, :], b3.shape).reshape(HG_TILE, LANES)
        b_last3 = b3[:, CHUNK - 1:CHUNK, :]
        b_last = jnp.broadcast_to(b_last3, b3.shape).reshape(HG_TILE, LANES)
        qt = (qh * jnp.exp(b - b_mid)).astype(bf16)
        kt = (kh * jnp.exp(jnp.minimum(b_mid - b, 80.0))).astype(bf16)
        a = lax.dot_general(qt, kt, _NT, preferred_element_type=f32)
        a = jnp.where(causal, a, 0.0).astype(bf16)
        vv = i_ref[rows, :].astype(bf16)
        oin_scr[rows, :] = jnp.dot(a, vv, preferred_element_type=f32)
        qhat_scr[rows, :] = (qh * jnp.exp(b)).astype(bf16)
        kd = (kh * jnp.exp(b_last - b)).astype(bf16)
        for c in range(per_tile):
            cr = slice(c * CHUNK, (c + 1) * CHUNK)
            n = t * per_tile + c
            ut_scr[n] = lax.dot_general(vv[cr], kd[cr], _TN, preferred_element_type=f32)
            dec_scr[n] = jnp.exp(b_last3[c])

    ng = ng_ref[...]
    for n in range(n_tiles * per_tile):
        rows = slice(n * CHUNK, (n + 1) * CHUNK)
        o = oin_scr[rows, :] + lax.dot_general(qhat_scr[rows, :], st.astype(bf16), _NT,
                                               preferred_element_type=f32)
        ms = jnp.mean(o * o, axis=-1, keepdims=True)
        gv = g_ref[rows, :]
        o_ref[rows, :] = (o * lax.rsqrt(ms + RMS_EPS) * ng * (gv * jax.nn.sigmoid(gv))).astype(bf16)
        st = st * dec_scr[n] + ut_scr[n]


def _hgrn(hg, hg_meta, lb, norm_g, batch, seq):
    n_chunks = seq // CHUNK

    def col(c):
        return pl.BlockSpec((seq, LANES), lambda b, h: (b, c * B_HEADS + h))

    def mcol(c):
        return pl.BlockSpec((N_META, LANES), lambda b, h: (0, c * B_HEADS + h))

    return pl.pallas_call(
        _hgrn_kernel,
        grid=(batch, B_HEADS),
        in_specs=[
            col(0), col(1), col(2), col(3), mcol(1), mcol(2),
            pl.BlockSpec((None, 1, LANES), lambda b, h: (h, 0, 0)),
            pl.BlockSpec((1, LANES), lambda b, h: (0, 0)),
        ],
        out_specs=pl.BlockSpec((seq, LANES), lambda b, h: (b, h)),
        out_shape=jax.ShapeDtypeStruct((batch * seq, B_W), bf16),
        scratch_shapes=[
            pltpu.VMEM((seq, LANES), f32),
            pltpu.VMEM((seq, LANES), bf16),
            pltpu.VMEM((n_chunks, LANES, LANES), f32),
            pltpu.VMEM((n_chunks, 1, LANES), f32),
        ],
        compiler_params=pltpu.CompilerParams(
            dimension_semantics=("arbitrary", "arbitrary"), vmem_limit_bytes=VMEM_LIMIT),
        name="hgrn",
    )(hg, hg, hg, hg, hg_meta, hg_meta, lb, norm_g)


def _mix_route_kernel(x_ref, g0_ref, b0_ref, ya_ref, yb_ref, wo_ref, g1_ref, b1_ref,
                      wr_hi_ref, wr_lo_ref, br_ref,
                      h_ref, hp_ref, idx_ref, gate_ref, rank_ref, cnt_ref, carry_scr):
    step = pl.program_id(0)

    @pl.when(step == 0)
    def _():
        carry_scr[...] = jnp.zeros_like(carry_scr)

    h0 = _layer_norm(x_ref[...], g0_ref[...], b0_ref[...])
    mix = jnp.dot(ya_ref[...], wo_ref[:A_W, :], preferred_element_type=f32)
    mix = mix + jnp.dot(yb_ref[...], wo_ref[A_W:, :], preferred_element_type=f32)
    h1 = _layer_norm(DEEPNORM_ALPHA * h0 + mix, g1_ref[...], b1_ref[...])
    h_ref[...] = h1
    hp_ref[...] = _pack_rows(h1)

    h_hi = h1.astype(bf16)
    h_lo = (h1 - h_hi.astype(f32)).astype(bf16)
    logits = jnp.dot(h_hi, wr_hi_ref[...], preferred_element_type=f32)
    logits = logits + jnp.dot(h_lo, wr_hi_ref[...], preferred_element_type=f32)
    logits = logits + jnp.dot(h_hi, wr_lo_ref[...], preferred_element_type=f32)
    logits = logits + br_ref[...]

    tm = logits.shape[0]
    lane = lax.broadcasted_iota(i32, (tm, N_EXPERTS), 1)
    work = logits
    vals, hots = [], []
    sel = jnp.zeros((tm, N_EXPERTS), f32)
    for k in range(TOP_K):
        m = jnp.max(work, axis=1, keepdims=True)
        first = jnp.min(jnp.where(work == m, lane, N_EXPERTS), axis=1, keepdims=True)
        hot = lane == first
        vals.append(m)
        hots.append(hot)
        idx_ref[:, k:k + 1] = first
        sel = sel + hot.astype(f32)
        work = jnp.where(hot, -jnp.inf, work)

    es = [jnp.exp(v - vals[0]) for v in vals]
    denom = es[0] + es[1] + es[2] + es[3]
    for k in range(TOP_K):
        gate_ref[:, k:k + 1] = es[k] / denom

    r = lax.broadcasted_iota(i32, (tm, tm), 0)
    c = lax.broadcasted_iota(i32, (tm, tm), 1)
    incl = jnp.dot((c <= r).astype(bf16), sel.astype(bf16), preferred_element_type=f32)
    carry = carry_scr[...]
    excl = incl - sel + carry
    for k in range(TOP_K):
        rank_ref[:, k:k + 1] = jnp.sum(jnp.where(hots[k], excl, 0.0), axis=1, keepdims=True).astype(i32)
    carry = carry + incl[tm - 1:tm, :]
    carry_scr[...] = carry
    cnt_ref[...] = carry.astype(i32)


def _mix_route(x2, g0, b0, ya, yb, wo, g1, b1, wr_hi, wr_lo, br):
    n = x2.shape[0]
    tm = ROW_TILE

    def full(shape):
        return pl.BlockSpec(shape, lambda i: (0,) * len(shape))

    vec = full((1, D_MODEL))
    small = pl.BlockSpec((tm, TOP_K), lambda i: (i, 0))
    return pl.pallas_call(
        _mix_route_kernel,
        grid=(n // tm,),
        in_specs=[
            pl.BlockSpec((tm, D_MODEL), lambda i: (i, 0)), vec, vec,
            pl.BlockSpec((tm, A_W), lambda i: (i, 0)),
            pl.BlockSpec((tm, B_W), lambda i: (i, 0)),
            full((D_MODEL, D_MODEL)), vec, vec,
            full((D_MODEL, N_EXPERTS)), full((D_MODEL, N_EXPERTS)), full((1, N_EXPERTS)),
        ],
        out_specs=[
            pl.BlockSpec((tm, D_MODEL), lambda i: (i, 0)),
            pl.BlockSpec((tm, HALF), lambda i: (i, 0)),
            small, small, small,
            full((1, N_EXPERTS)),
        ],
        out_shape=[
            jax.ShapeDtypeStruct((n, D_MODEL), f32),
            jax.ShapeDtypeStruct((n, HALF), u32),
            jax.ShapeDtypeStruct((n, TOP_K), i32),
            jax.ShapeDtypeStruct((n, TOP_K), f32),
            jax.ShapeDtypeStruct((n, TOP_K), i32),
            jax.ShapeDtypeStruct((1, N_EXPERTS), i32),
        ],
        scratch_shapes=[pltpu.VMEM((1, N_EXPERTS), f32)],
        compiler_params=pltpu.CompilerParams(
            dimension_semantics=("arbitrary",), vmem_limit_bytes=VMEM_LIMIT),
        name="mix_route",
    )(x2, g0, b0, ya, yb, wo, g1, b1, wr_hi, wr_lo, br)


def _row_copy(src, src_row, dst, dst_row, sem):
    return pltpu.make_async_copy(src.at[pl.ds(src_row, 1), :], dst.at[pl.ds(dst_row, 1), :], sem)


def _dispatch_kernel(hp_ref, dest_hbm, zero_hbm, rows_hbm, idx_smem, idx_sem, row_sem):
    del zero_hbm
    step = pl.program_id(0)
    n_assign = ROW_TILE * TOP_K
    idx_copy = pltpu.make_async_copy(dest_hbm.at[step], idx_smem, idx_sem)
    idx_copy.start()
    idx_copy.wait()

    def issue(t, carry):
        for k in range(TOP_K):
            _row_copy(hp_ref, t, rows_hbm, idx_smem[t * TOP_K + k], row_sem).start()
        return carry

    lax.fori_loop(0, ROW_TILE, issue, 0, unroll=DMA_UNROLL)

    for k in range(TOP_K):
        pltpu.make_async_copy(hp_ref, rows_hbm.at[pl.ds(0, ROW_TILE), :], row_sem).wait()


def _dispatch(hp, dest_tiles, n_rows):
    n = hp.shape[0]
    zeros = jnp.zeros((n_rows, HALF), u32)
    return pl.pallas_call(
        _dispatch_kernel,
        grid=(n // ROW_TILE,),
        in_specs=[
            pl.BlockSpec((ROW_TILE, HALF), lambda i: (i, 0)),
            pl.BlockSpec(memory_space=pl.ANY),
            pl.BlockSpec(memory_space=pl.ANY),
        ],
        out_specs=pl.BlockSpec(memory_space=pl.ANY),
        out_shape=jax.ShapeDtypeStruct((n_rows, HALF), u32),
        scratch_shapes=[
            pltpu.SMEM((ROW_TILE * TOP_K,), i32),
            pltpu.SemaphoreType.DMA(()),
            pltpu.SemaphoreType.DMA(()),
        ],
        input_output_aliases={2: 0},
        compiler_params=pltpu.CompilerParams(
            dimension_semantics=("arbitrary",), vmem_limit_bytes=VMEM_LIMIT),
        name="dispatch",
    )(hp, dest_tiles, zeros)


def _experts_kernel(be_ref, na_ref, nxt_ref, x_ref, wgu_hbm, bg_ref, bu_ref, wdn_hbm, bdn_ref, y_ref,
                    gu_stage, dn_stage, wg_scr, wu_scr, wd_scr, act_scr, gu_sem, dn_sem):
    j = pl.program_id(0)
    expert = be_ref[j]
    prev = be_ref[jnp.maximum(j - 1, 0)]
    fresh = jnp.logical_or(j == 0, expert != prev)

    def weight_copies(e):
        return (pltpu.make_async_copy(wgu_hbm.at[e], gu_stage, gu_sem),
                pltpu.make_async_copy(wdn_hbm.at[e], dn_stage, dn_sem))

    @pl.when(j == 0)
    def _():
        for c in weight_copies(expert):
            c.start()

    @pl.when(jnp.logical_and(fresh, j < na_ref[0]))
    def _():
        for c in weight_copies(expert):
            c.wait()
        src = lax.broadcasted_iota(i32, (2 * LANES, 2 * LANES), 0)
        dst = lax.broadcasted_iota(i32, (2 * LANES, 2 * LANES), 1)
        perm = (src == jnp.where(dst < LANES, 2 * dst, 2 * (dst - LANES) + 1)).astype(bf16)
        for t in range(2 * D_EXPERT // (2 * LANES)):
            w = gu_stage[:, t * 2 * LANES:(t + 1) * 2 * LANES].astype(bf16)
            sep = jnp.dot(w, perm, preferred_element_type=f32)
            wg_scr[:, t * LANES:(t + 1) * LANES] = sep[:, :LANES].astype(bf16)
            wu_scr[:, t * LANES:(t + 1) * LANES] = sep[:, LANES:].astype(bf16)
        wd_scr[...] = dn_stage[...].astype(bf16)

        @pl.when(nxt_ref[j] >= 0)
        def _():
            for c in weight_copies(nxt_ref[j]):
                c.start()

    @pl.when(j < na_ref[0])
    def _():
        left, right = _unpack_rows(x_ref[...])
        xl = left.astype(bf16)
        xr = right.astype(bf16)
        fc = 256
        for c in range(D_EXPERT // fc):
            cols = slice(c * fc, (c + 1) * fc)
            gate = jnp.dot(xl, wg_scr[:HALF, cols], preferred_element_type=f32)
            gate = gate + jnp.dot(xr, wg_scr[HALF:, cols], preferred_element_type=f32) + bg_ref[:, cols]
            up = jnp.dot(xl, wu_scr[:HALF, cols], preferred_element_type=f32)
            up = up + jnp.dot(xr, wu_scr[HALF:, cols], preferred_element_type=f32) + bu_ref[:, cols]
            gate = jnp.minimum(gate, SWIGLU_LIMIT)
            up = jnp.clip(up, -SWIGLU_LIMIT, SWIGLU_LIMIT)
            act_scr[:, cols] = ((up + 1.0) * gate * jax.nn.sigmoid(gate * SWIGLU_ALPHA)).astype(bf16)
        y = jnp.dot(act_scr[...], wd_scr[...], preferred_element_type=f32) + bdn_ref[...]
        y_ref[...] = _pack_rows(y)

    @pl.when(j >= na_ref[0])
    def _():
        y_ref[...] = jnp.zeros_like(y_ref)


def _experts(block_expert, n_active, next_expert, rows, w_gu, bg, bu, w_dn, b_dn):
    n_blocks = rows.shape[0] // EXPERT_BLOCK

    def blk(j, be, na, nxt):
        return (jnp.minimum(j, na[0] - 1), 0)

    def per_expert(shape):
        return pl.BlockSpec((None,) + shape, lambda j, be, na, nxt: (be[j], 0, 0))

    return pl.pallas_call(
        _experts_kernel,
        grid_spec=pltpu.PrefetchScalarGridSpec(
            num_scalar_prefetch=3,
            grid=(n_blocks,),
            in_specs=[
                pl.BlockSpec((EXPERT_BLOCK, HALF), blk),
                pl.BlockSpec(memory_space=pl.ANY),
                per_expert((1, D_EXPERT)),
                per_expert((1, D_EXPERT)),
                pl.BlockSpec(memory_space=pl.ANY),
                per_expert((1, D_MODEL)),
            ],
            out_specs=pl.BlockSpec((EXPERT_BLOCK, HALF), lambda j, be, na, nxt: (j, 0)),
            scratch_shapes=[
                pltpu.VMEM((D_MODEL, 2 * D_EXPERT), f32),
                pltpu.VMEM((D_EXPERT, D_MODEL), f32),
                pltpu.VMEM((D_MODEL, D_EXPERT), bf16),
                pltpu.VMEM((D_MODEL, D_EXPERT), bf16),
                pltpu.VMEM((D_EXPERT, D_MODEL), bf16),
                pltpu.VMEM((EXPERT_BLOCK, D_EXPERT), bf16),
                pltpu.SemaphoreType.DMA(()),
                pltpu.SemaphoreType.DMA(()),
            ],
        ),
        out_shape=jax.ShapeDtypeStruct(rows.shape, u32),
        compiler_params=pltpu.CompilerParams(
            dimension_semantics=("arbitrary",), vmem_limit_bytes=VMEM_LIMIT),
        name="experts",
    )(block_expert, n_active, next_expert, rows, w_gu, bg, bu, w_dn, b_dn)


def _combine_kernel(h_ref, gate_ref, g2_ref, b2_ref, dest_hbm, y_hbm, o_ref,
                    idx_smem, ybuf, idx_sem, row_sem):
    step = pl.program_id(0)
    n_assign = ROW_TILE * TOP_K
    idx_copy = pltpu.make_async_copy(dest_hbm.at[step], idx_smem, idx_sem)
    idx_copy.start()
    idx_copy.wait()

    def issue(t, carry):
        for k in range(TOP_K):
            _row_copy(y_hbm, idx_smem[t * TOP_K + k], ybuf, k * ROW_TILE + t, row_sem).start()
        return carry

    lax.fori_loop(0, ROW_TILE, issue, 0, unroll=DMA_UNROLL)

    for k in range(TOP_K):
        rows = pl.ds(k * ROW_TILE, ROW_TILE)
        pltpu.make_async_copy(y_hbm.at[pl.ds(0, ROW_TILE), :], ybuf.at[rows, :], row_sem).wait()

    gates = gate_ref[...]
    left = jnp.zeros((ROW_TILE, HALF), f32)
    right = jnp.zeros((ROW_TILE, HALF), f32)
    for k in range(TOP_K):
        yl, yr = _unpack_rows(ybuf[k * ROW_TILE:(k + 1) * ROW_TILE, :])
        gk = gates[:, k:k + 1]
        left = left + yl * gk
        right = right + yr * gk
    ffn = jnp.concatenate([left, right], axis=1)
    o_ref[...] = _layer_norm(DEEPNORM_ALPHA * h_ref[...] + ffn, g2_ref[...], b2_ref[...])


def _combine(h1, gates, g2, b2, dest_tiles, y_rows):
    n = h1.shape[0]
    vec = pl.BlockSpec((1, D_MODEL), lambda i: (0, 0))
    return pl.pallas_call(
        _combine_kernel,
        grid=(n // ROW_TILE,),
        in_specs=[
            pl.BlockSpec((ROW_TILE, D_MODEL), lambda i: (i, 0)),
            pl.BlockSpec((ROW_TILE, TOP_K), lambda i: (i, 0)),
            vec, vec,
            pl.BlockSpec(memory_space=pl.ANY),
            pl.BlockSpec(memory_space=pl.ANY),
        ],
        out_specs=pl.BlockSpec((ROW_TILE, D_MODEL), lambda i: (i, 0)),
        out_shape=jax.ShapeDtypeStruct((n, D_MODEL), f32),
        scratch_shapes=[
            pltpu.SMEM((ROW_TILE * TOP_K,), i32),
            pltpu.VMEM((TOP_K * ROW_TILE, HALF), u32),
            pltpu.SemaphoreType.DMA(()),
            pltpu.SemaphoreType.DMA(()),
        ],
        compiler_params=pltpu.CompilerParams(
            dimension_semantics=("arbitrary",), vmem_limit_bytes=VMEM_LIMIT),
        name="combine",
    )(h1, gates, g2, b2, dest_tiles, y_rows)


def _rope_tables(pos, scale):
    inv = 1.0 / (ROPE_THETA ** (jnp.arange(0, A_HEAD_DIM, 2, dtype=f32) / A_HEAD_DIM))
    ang = pos.astype(f32)[:, None] * inv[None, :]
    ang = jnp.concatenate([ang, ang, ang, ang], axis=-1)
    sign = jnp.where((jnp.arange(LANES) & 63) < 32, -1.0, 1.0).astype(f32)
    return jnp.cos(ang) * scale, jnp.sin(ang) * sign * scale


def kernel(x, meta_tokens, ln_emb_g, ln_emb_b, w_in, lambda_q1, lambda_k1, lambda_q2, lambda_k2,
           subln_g, hgrn_lb_table, hgrn_norm_g, w_out, ln1_g, ln1_b, w_router, b_router,
           w_gate_up, b_gate_up, w_down, b_down, ln2_g, ln2_b):
    batch, seq, _ = x.shape
    n_tok = batch * seq
    x2 = x.reshape(n_tok, D_MODEL)
    row = lambda v: v.reshape(1, -1).astype(f32)

    w_in_bf = w_in[0].astype(bf16)
    lam = (jnp.exp(jnp.sum(lambda_q1[0].astype(f32) * lambda_k1[0].astype(f32)))
           - jnp.exp(jnp.sum(lambda_q2[0].astype(f32) * lambda_k2[0].astype(f32))) + LAM_INIT).reshape(1)
    lb = jnp.cumsum(jax.nn.softmax(hgrn_lb_table.astype(f32), axis=0), axis=0)[0].reshape(B_HEADS, 1, LANES)
    q_scale = A_HEAD_DIM ** -0.5
    pos_x = N_META + jnp.arange(seq)
    pos_m = jnp.arange(N_META)
    cq, sq = _rope_tables(pos_x, q_scale)
    ck, sk = _rope_tables(pos_x, 1.0)
    cqm, sqm = _rope_tables(pos_m, q_scale)
    ckm, skm = _rope_tables(pos_m, 1.0)
    wr = w_router[0].astype(f32)
    wr_hi = wr.astype(bf16)
    wr_lo = (wr - wr_hi.astype(f32)).astype(bf16)
    bg = b_gate_up[0][:, 0::2].reshape(N_EXPERTS, 1, D_EXPERT)
    bu = b_gate_up[0][:, 1::2].reshape(N_EXPERTS, 1, D_EXPERT)
    bdn = b_down[0].reshape(N_EXPERTS, 1, D_MODEL)

    g0, b0 = row(ln_emb_g), row(ln_emb_b)
    qkv, hg = _in_proj(x2, g0, b0, w_in_bf, (cq, sq, ck, sk), PROJ_TILE, seq // PROJ_TILE)
    qkv_m, hg_m = _in_proj(meta_tokens, g0, b0, w_in_bf, (cqm, sqm, ckm, skm), N_META, 1)

    ya = _attention(lam, qkv, qkv_m, subln_g[0].reshape(-1, 1).astype(f32), batch, seq)
    yb = _hgrn(hg, hg_m, lb, row(hgrn_norm_g[0]), batch, seq)

    h1, hp, top_idx, gates, rank, counts = _mix_route(
        x2, g0, b0, ya, yb, w_out[0].astype(bf16), row(ln1_g[0]), row(ln1_b[0]),
        wr_hi, wr_lo, row(b_router[0]))

    counts = counts.reshape(N_EXPERTS)
    blocks = (counts + EXPERT_BLOCK - 1) // EXPERT_BLOCK
    cum_blocks = jnp.cumsum(blocks)
    start_pad = (cum_blocks - blocks) * EXPERT_BLOCK
    n_blocks = n_tok * TOP_K // EXPERT_BLOCK + N_EXPERTS
    n_active = cum_blocks[-1:].astype(i32)
    jb = jnp.arange(n_blocks, dtype=i32)
    block_expert = jnp.sum((cum_blocks[None, :] <= jnp.minimum(jb, n_active - 1)[:, None]).astype(i32), axis=1)
    block_expert = jnp.minimum(block_expert, N_EXPERTS - 1).astype(i32)
    eid = jnp.arange(N_EXPERTS, dtype=i32)
    later_active = jnp.logical_and(blocks[None, :] > 0, eid[None, :] > eid[:, None])
    next_active = jnp.min(jnp.where(later_active, eid[None, :], N_EXPERTS), axis=1)
    next_active = jnp.where(next_active == N_EXPERTS, -1, next_active).astype(i32)
    next_expert = next_active[block_expert]
    hot = top_idx[:, :, None] == jnp.arange(N_EXPERTS, dtype=i32)[None, None, :]
    dest = jnp.sum(jnp.where(hot, start_pad.astype(i32)[None, None, :], 0), axis=-1) + rank
    dest_tiles = dest.reshape(n_tok // ROW_TILE, ROW_TILE * TOP_K)

    x_rows = _dispatch(hp, dest_tiles, n_blocks * EXPERT_BLOCK)
    y_rows = _experts(block_expert, n_active, next_expert, x_rows, w_gate_up[0], bg, bu, w_down[0], bdn)
    out = _combine(h1, gates, row(ln2_g[0]), row(ln2_b[0]), dest_tiles, y_rows)
    return out.reshape(batch, seq, D_MODEL)
```

```python
import functools
import math

import jax
import jax.numpy as jnp
import numpy as np
from jax import lax
from jax.experimental import pallas as pl
from jax.experimental.pallas import tpu as pltpu

D_MODEL = 1024
N_META = 16
CHUNK = 64
A_HEADS = 4
A_HEAD_DIM = 64
B_HEADS = 4
B_KEY_DIM = 128
ROPE_THETA = 10000.0
A_W = 512
B_W = 512
D_IN = 3 * A_W + 4 * B_W
N_EXPERTS = 32
TOP_K = 4
D_EXPERT = 1024
SWIGLU_ALPHA = 1.702
SWIGLU_LIMIT = 7.0
EXPERT_BLOCK = 256
DEEPNORM_ALPHA = 2.0 ** 0.25
LN_EPS = 1e-5
RMS_EPS = 1e-5
LAM_INIT = 0.8 - 0.6 * math.exp(0.0)

LANES = 128
HALF = D_MODEL // 2
ROW_TILE = 256
PROJ_TILE = 512
ATT_TILE = 256
HG_TILE = 256
DMA_UNROLL = 4
VMEM_LIMIT = 56 * 1024 * 1024

_NT = (((1,), (1,)), ((), ()))
_TN = (((0,), (0,)), ((), ()))

f32 = jnp.float32
bf16 = jnp.bfloat16
u32 = jnp.uint32
i32 = jnp.int32


def _layer_norm(x, g, b):
    mu = jnp.mean(x, axis=-1, keepdims=True)
    xc = x - mu
    var = jnp.mean(xc * xc, axis=-1, keepdims=True)
    return xc * lax.rsqrt(var + LN_EPS) * g + b


def _pack_rows(h):
    hb = h.astype(bf16).astype(f32)
    lo = lax.bitcast_convert_type(hb[:, :HALF], u32) >> 16
    hi = lax.bitcast_convert_type(hb[:, HALF:], u32)
    return hi | lo


def _unpack_rows(w):
    left = lax.bitcast_convert_type(w << 16, f32)
    right = lax.bitcast_convert_type(w & jnp.uint32(0xFFFF0000), f32)
    return left, right


def _rope(x, cos, sin_signed, first_half):
    fwd = pltpu.roll(x, 32, axis=1)
    bwd = pltpu.roll(x, x.shape[1] - 32, axis=1)
    return x * cos + jnp.where(first_half, bwd, fwd) * sin_signed


def _in_proj_kernel(x_ref, g_ref, b_ref, w_ref, cq_ref, sq_ref, ck_ref, sk_ref, qkv_ref, hg_ref):
    h = _layer_norm(x_ref[...], g_ref[...], b_ref[...]).astype(bf16)
    lane = lax.broadcasted_iota(i32, (h.shape[0], A_W), 1)
    first_half = (lane & 63) < 32
    heads = lambda t: jnp.concatenate([t[...]] * A_HEADS, axis=1)
    for c, (cos_ref, sin_ref) in enumerate(((cq_ref, sq_ref), (ck_ref, sk_ref))):
        acc = jnp.dot(h, w_ref[:, c * A_W:(c + 1) * A_W], preferred_element_type=f32)
        r = _rope(acc, heads(cos_ref), heads(sin_ref), first_half)
        qkv_ref[:, c * A_W:(c + 1) * A_W] = r.astype(bf16)
    qkv_ref[:, 2 * A_W:3 * A_W] = jnp.dot(
        h, w_ref[:, 2 * A_W:3 * A_W], preferred_element_type=f32).astype(bf16)
    for c in range(4):
        lo = 3 * A_W + c * B_W
        hg_ref[:, c * B_W:(c + 1) * B_W] = jnp.dot(h, w_ref[:, lo:lo + B_W], preferred_element_type=f32)


def _in_proj(x2, g, b, w_bf, tabs, tile, tab_blocks):
    n = x2.shape[0]
    tab_spec = pl.BlockSpec((tile, LANES), lambda i: (i % tab_blocks, 0))
    vec = pl.BlockSpec((1, D_MODEL), lambda i: (0, 0))
    return pl.pallas_call(
        _in_proj_kernel,
        grid=(n // tile,),
        in_specs=[
            pl.BlockSpec((tile, D_MODEL), lambda i: (i, 0)),
            vec, vec,
            pl.BlockSpec((D_MODEL, D_IN), lambda i: (0, 0)),
            tab_spec, tab_spec, tab_spec, tab_spec,
        ],
        out_specs=[
            pl.BlockSpec((tile, 3 * A_W), lambda i: (i, 0)),
            pl.BlockSpec((tile, 4 * B_W), lambda i: (i, 0)),
        ],
        out_shape=[
            jax.ShapeDtypeStruct((n, 3 * A_W), bf16),
            jax.ShapeDtypeStruct((n, 4 * B_W), f32),
        ],
        compiler_params=pltpu.CompilerParams(
            dimension_semantics=("arbitrary",), vmem_limit_bytes=VMEM_LIMIT),
        name="in_proj",
    )(x2, g, b, w_bf, *tabs)


def _attn_kernel(lam_ref, q_ref, k_ref, v_ref, km_ref, vm_ref, g_ref, o_ref):
    tq = ATT_TILE
    lane = lax.broadcasted_iota(i32, (tq, LANES), 1)
    key = lax.broadcasted_iota(i32, (tq, 2 * tq), 0)
    qry = lax.broadcasted_iota(i32, (tq, 2 * tq), 1)
    visible = ((qry & (tq - 1)) >> 6) >= (key >> 6)
    gain = g_ref[...] * (1.0 - LAM_INIT)
    km = km_ref[...]
    vm = vm_ref[...]
    for qi in range(q_ref.shape[0] // tq):
        q = q_ref[qi * tq:(qi + 1) * tq, :]
        zero = jnp.zeros_like(q)
        qs = jnp.concatenate([jnp.where(lane < A_HEAD_DIM, q, zero),
                              jnp.where(lane >= A_HEAD_DIM, q, zero)], axis=0)
        n = (qi + 1) * tq
        s = lax.dot_general(k_ref[0:n, :], qs, _NT, preferred_element_type=f32)
        diag = jnp.where(visible, s[n - tq:, :], -jnp.inf)
        s = diag if qi == 0 else jnp.concatenate([s[:n - tq, :], diag], axis=0)
        sm = lax.dot_general(km, qs, _NT, preferred_element_type=f32)
        m = jnp.maximum(jnp.max(s, axis=0, keepdims=True), jnp.max(sm, axis=0, keepdims=True))
        p = jnp.exp(s - m)
        pm = jnp.exp(sm - m)
        l = jnp.sum(p, axis=0, keepdims=True) + jnp.sum(pm, axis=0, keepdims=True)
        acc = lax.dot_general(v_ref[0:n, :], p.astype(bf16), _TN, preferred_element_type=f32)
        acc = acc + lax.dot_general(vm, pm.astype(bf16), _TN, preferred_element_type=f32)
        inv = 1.0 / l
        o = acc[:, :tq] * inv[:, :tq] - lam_ref[0] * (acc[:, tq:] * inv[:, tq:])
        ms = jnp.mean(o * o, axis=0, keepdims=True)
        o = o * lax.rsqrt(ms + RMS_EPS) * gain
        o_ref[qi * tq:(qi + 1) * tq, :] = o.T.astype(bf16)


def _attention(lam, qkv, qkv_meta, subln_g, batch, seq):
    return pl.pallas_call(
        _attn_kernel,
        grid_spec=pltpu.PrefetchScalarGridSpec(
            num_scalar_prefetch=1,
            grid=(batch, A_HEADS),
            in_specs=[
                pl.BlockSpec((seq, LANES), lambda b, h, lam: (b, h)),
                pl.BlockSpec((seq, LANES), lambda b, h, lam: (b, A_HEADS + h)),
                pl.BlockSpec((seq, LANES), lambda b, h, lam: (b, 2 * A_HEADS + h)),
                pl.BlockSpec((N_META, LANES), lambda b, h, lam: (0, A_HEADS + h)),
                pl.BlockSpec((N_META, LANES), lambda b, h, lam: (0, 2 * A_HEADS + h)),
                pl.BlockSpec((LANES, 1), lambda b, h, lam: (0, 0)),
            ],
            out_specs=pl.BlockSpec((seq, LANES), lambda b, h, lam: (b, h)),
        ),
        out_shape=jax.ShapeDtypeStruct((batch * seq, A_W), bf16),
        compiler_params=pltpu.CompilerParams(
            dimension_semantics=("arbitrary", "arbitrary"), vmem_limit_bytes=VMEM_LIMIT),
        name="attn",
    )(lam, qkv, qkv, qkv, qkv_meta, qkv_meta, subln_g)


def _split3(x):
    a = x.astype(bf16)
    r = x - a.astype(f32)
    b = r.astype(bf16)
    c = (r - b.astype(f32)).astype(bf16)
    return a, b, c


def _chunk_cumsum(tri, x):
    a, b, c = _split3(x)
    out = jnp.dot(tri, a, preferred_element_type=f32)
    out = out + jnp.dot(tri, b, preferred_element_type=f32)
    return out + jnp.dot(tri, c, preferred_element_type=f32)


def _gates(z, lb):
    log_f = jnp.log(lb + (1.0 - lb) * jax.nn.sigmoid(z))
    key = (1.0 - lb) * jax.nn.sigmoid(-z)
    return log_f, key


def _hgrn_kernel(q_ref, f_ref, i_ref, g_ref, fm_ref, im_ref, lb_ref, ng_ref, o_ref,
                 oin_scr, qhat_scr, ut_scr, dec_scr):
    lb = lb_ref[...]
    n_tiles = q_ref.shape[0] // HG_TILE
    per_tile = HG_TILE // CHUNK

    lfm, km = _gates(fm_ref[...], lb)
    r16 = lax.broadcasted_iota(i32, (N_META, N_META), 0)
    c16 = lax.broadcasted_iota(i32, (N_META, N_META), 1)
    bm = _chunk_cumsum((c16 <= r16).astype(bf16), lfm)
    kdm = km * jnp.exp(bm[N_META - 1:N_META, :] - bm)
    st = lax.dot_general(im_ref[...].astype(bf16), kdm.astype(bf16), _TN, preferred_element_type=f32)

    row = lax.broadcasted_iota(i32, (HG_TILE, HG_TILE), 0)
    col = lax.broadcasted_iota(i32, (HG_TILE, HG_TILE), 1)
    causal = ((row >> 6) == (col >> 6)) & (col <= row)
    tri = causal.astype(bf16)

    for t in range(n_tiles):
        rows = slice(t * HG_TILE, (t + 1) * HG_TILE)
        qv = q_ref[rows, :]
        qh = qv * jax.nn.sigmoid(qv) * (B_KEY_DIM ** -0.5)
        log_f, kh = _gates(f_ref[rows, :], lb)
        b = _chunk_cumsum(tri, log_f)
        b3 = b.reshape(per_tile, CHUNK, LANES)
        b_mid = jnp.broadcast_to(b3[:, CHUNK // 2:CHUNK // 2 + 1, :], b3.shape).reshape(HG_TILE, LANES)
        b_last3 = b3[:, CHUNK - 1:CHUNK, :]
        b_last = jnp.broadcast_to(b_last3, b3.shape).reshape(HG_TILE, LANES)
        qt = (qh * jnp.exp(b - b_mid)).astype(bf16)
        kt = (kh * jnp.exp(jnp.minimum(b_mid - b, 80.0))).astype(bf16)
        a = lax.dot_general(qt, kt, _NT, preferred_element_type=f32)
        a = jnp.where(causal, a, 0.0).astype(bf16)
        vv = i_ref[rows, :].astype(bf16)
        oin_scr[rows, :] = jnp.dot(a, vv, preferred_element_type=f32)
        qhat_scr[rows, :] = (qh * jnp.exp(b)).astype(bf16)
        kd = (kh * jnp.exp(b_last - b)).astype(bf16)
        for c in range(per_tile):
            cr = slice(c * CHUNK, (c + 1) * CHUNK)
            n = t * per_tile + c
            ut_scr[n] = lax.dot_general(vv[cr], kd[cr], _TN, preferred_element_type=f32)
            dec_scr[n] = jnp.exp(b_last3[c])

    ng = ng_ref[...]
    for n in range(n_tiles * per_tile):
        rows = slice(n * CHUNK, (n + 1) * CHUNK)
        o = oin_scr[rows, :] + lax.dot_general(qhat_scr[rows, :], st.astype(bf16), _NT,
                                               preferred_element_type=f32)
        ms = jnp.mean(o * o, axis=-1, keepdims=True)
        gv = g_ref[rows, :]
        o_ref[rows, :] = (o * lax.rsqrt(ms + RMS_EPS) * ng * (gv * jax.nn.sigmoid(gv))).astype(bf16)
        st = st * dec_scr[n] + ut_scr[n]


def _hgrn(hg, hg_meta, lb, norm_g, batch, seq):
    n_chunks = seq // CHUNK

    def col(c):
        return pl.BlockSpec((seq, LANES), lambda b, h: (b, c * B_HEADS + h))

    def mcol(c):
        return pl.BlockSpec((N_META, LANES), lambda b, h: (0, c * B_HEADS + h))

    return pl.pallas_call(
        _hgrn_kernel,
        grid=(batch, B_HEADS),
        in_specs=[
            col(0), col(1), col(2), col(3), mcol(1), mcol(2),
            pl.BlockSpec((None, 1, LANES), lambda b, h: (h, 0, 0)),
            pl.BlockSpec((1, LANES), lambda b, h: (0, 0)),
        ],
        out_specs=pl.BlockSpec((seq, LANES), lambda b, h: (b, h)),
        out_shape=jax.ShapeDtypeStruct((batch * seq, B_W), bf16),
        scratch_shapes=[
            pltpu.VMEM((seq, LANES), f32),
            pltpu.VMEM((seq, LANES), bf16),
            pltpu.VMEM((n_chunks, LANES, LANES), f32),
            pltpu.VMEM((n_chunks, 1, LANES), f32),
        ],
        compiler_params=pltpu.CompilerParams(
            dimension_semantics=("arbitrary", "arbitrary"), vmem_limit_bytes=VMEM_LIMIT),
        name="hgrn",
    )(hg, hg, hg, hg, hg_meta, hg_meta, lb, norm_g)


def _mix_route_kernel(x_ref, g0_ref, b0_ref, ya_ref, yb_ref, wo_ref, g1_ref, b1_ref,
                      wr_hi_ref, wr_lo_ref, br_ref,
                      h_ref, hp_ref, idx_ref, gate_ref, rank_ref, cnt_ref, carry_scr):
    step = pl.program_id(0)

    @pl.when(step == 0)
    def _():
        carry_scr[...] = jnp.zeros_like(carry_scr)

    h0 = _layer_norm(x_ref[...], g0_ref[...], b0_ref[...])
    mix = jnp.dot(ya_ref[...], wo_ref[:A_W, :], preferred_element_type=f32)
    mix = mix + jnp.dot(yb_ref[...], wo_ref[A_W:, :], preferred_element_type=f32)
    h1 = _layer_norm(DEEPNORM_ALPHA * h0 + mix, g1_ref[...], b1_ref[...])
    h_ref[...] = h1
    hp_ref[...] = _pack_rows(h1)

    h_hi = h1.astype(bf16)
    h_lo = (h1 - h_hi.astype(f32)).astype(bf16)
    logits = jnp.dot(h_hi, wr_hi_ref[...], preferred_element_type=f32)
    logits = logits + jnp.dot(h_lo, wr_hi_ref[...], preferred_element_type=f32)
    logits = logits + jnp.dot(h_hi, wr_lo_ref[...], preferred_element_type=f32)

    lt = logits.T[:N_EXPERTS, :] + br_ref[...]
    tm = lt.shape[1]
    eid = lax.broadcasted_iota(i32, (N_EXPERTS, tm), 0)
    work = lt
    vals, hots = [], []
    sel = jnp.zeros((N_EXPERTS, tm), f32)
    for k in range(TOP_K):
        m = jnp.max(work, axis=0, keepdims=True)
        first = jnp.min(jnp.where(work == m, eid, N_EXPERTS), axis=0, keepdims=True)
        hot = eid == first
        vals.append(m)
        hots.append(hot)
        idx_ref[k:k + 1, :] = first
        sel = sel + hot.astype(f32)
        work = jnp.where(hot, -jnp.inf, work)

    es = [jnp.exp(v - vals[0]) for v in vals]
    denom = es[0] + es[1] + es[2] + es[3]
    for k in range(TOP_K):
        gate_ref[k:k + 1, :] = es[k] / denom

    r = lax.broadcasted_iota(i32, (tm, tm), 0)
    c = lax.broadcasted_iota(i32, (tm, tm), 1)
    incl = jnp.dot(sel.astype(bf16), (r <= c).astype(bf16), preferred_element_type=f32)
    carry = carry_scr[...]
    excl = incl - sel + carry
    for k in range(TOP_K):
        rank_ref[k:k + 1, :] = jnp.sum(jnp.where(hots[k], excl, 0.0), axis=0, keepdims=True).astype(i32)
    carry = carry + incl[:, tm - 1:tm]
    carry_scr[...] = carry
    cnt_ref[...] = carry.astype(i32)


def _mix_route(x2, g0, b0, ya, yb, wo, g1, b1, wr_hi, wr_lo, br):
    n = x2.shape[0]
    tm = ROW_TILE

    def full(shape):
        return pl.BlockSpec(shape, lambda i: (0,) * len(shape))

    vec = full((1, D_MODEL))
    small = pl.BlockSpec((TOP_K, tm), lambda i: (0, i))
    return pl.pallas_call(
        _mix_route_kernel,
        grid=(n // tm,),
        in_specs=[
            pl.BlockSpec((tm, D_MODEL), lambda i: (i, 0)), vec, vec,
            pl.BlockSpec((tm, A_W), lambda i: (i, 0)),
            pl.BlockSpec((tm, B_W), lambda i: (i, 0)),
            full((D_MODEL, D_MODEL)), vec, vec,
            full((D_MODEL, LANES)), full((D_MODEL, LANES)), full((N_EXPERTS, 1)),
        ],
        out_specs=[
            pl.BlockSpec((tm, D_MODEL), lambda i: (i, 0)),
            pl.BlockSpec((tm, HALF), lambda i: (i, 0)),
            small, small, small,
            full((N_EXPERTS, 1)),
        ],
        out_shape=[
            jax.ShapeDtypeStruct((n, D_MODEL), f32),
            jax.ShapeDtypeStruct((n, HALF), u32),
            jax.ShapeDtypeStruct((TOP_K, n), i32),
            jax.ShapeDtypeStruct((TOP_K, n), f32),
            jax.ShapeDtypeStruct((TOP_K, n), i32),
            jax.ShapeDtypeStruct((N_EXPERTS, 1), i32),
        ],
        scratch_shapes=[pltpu.VMEM((N_EXPERTS, 1), f32)],
        compiler_params=pltpu.CompilerParams(
            dimension_semantics=("arbitrary",), vmem_limit_bytes=VMEM_LIMIT),
        name="mix_route",
    )(x2, g0, b0, ya, yb, wo, g1, b1, wr_hi, wr_lo, br)


def _row_copy(src, src_row, dst, dst_row, sem):
    return pltpu.make_async_copy(src.at[pl.ds(src_row, 1), :], dst.at[pl.ds(dst_row, 1), :], sem)


def _dispatch_kernel(hp_ref, dest_hbm, zero_hbm, rows_hbm, idx_smem, idx_sem, row_sem):
    del zero_hbm
    step = pl.program_id(0)
    n_assign = ROW_TILE * TOP_K
    idx_copy = pltpu.make_async_copy(dest_hbm.at[step], idx_smem, idx_sem)
    idx_copy.start()
    idx_copy.wait()

    def issue(t, carry):
        for k in range(TOP_K):
            _row_copy(hp_ref, t, rows_hbm, idx_smem[t * TOP_K + k], row_sem).start(priority=k % 2)
        return carry

    lax.fori_loop(0, ROW_TILE, issue, 0, unroll=DMA_UNROLL)

    for k in range(TOP_K):
        pltpu.make_async_copy(hp_ref, rows_hbm.at[pl.ds(0, ROW_TILE), :], row_sem).wait()


def _dispatch(hp, dest_tiles, n_rows):
    n = hp.shape[0]
    zeros = jnp.zeros((n_rows, HALF), u32)
    return pl.pallas_call(
        _dispatch_kernel,
        grid=(n // ROW_TILE,),
        in_specs=[
            pl.BlockSpec((ROW_TILE, HALF), lambda i: (i, 0)),
            pl.BlockSpec(memory_space=pl.ANY),
            pl.BlockSpec(memory_space=pl.ANY),
        ],
        out_specs=pl.BlockSpec(memory_space=pl.ANY),
        out_shape=jax.ShapeDtypeStruct((n_rows, HALF), u32),
        scratch_shapes=[
            pltpu.SMEM((ROW_TILE * TOP_K,), i32),
            pltpu.SemaphoreType.DMA(()),
            pltpu.SemaphoreType.DMA(()),
        ],
        input_output_aliases={2: 0},
        compiler_params=pltpu.CompilerParams(
            dimension_semantics=("arbitrary",), vmem_limit_bytes=VMEM_LIMIT),
        name="dispatch",
    )(hp, dest_tiles, zeros)


def _experts_kernel(be_ref, na_ref, nxt_ref, x_ref, wgu_hbm, bg_ref, bu_ref, wdn_hbm, bdn_ref, y_ref,
                    gu_stage, dn_stage, wg_scr, wu_scr, wd_scr, act_scr, gu_sem, dn_sem):
    j = pl.program_id(0)
    expert = be_ref[j]
    prev = be_ref[jnp.maximum(j - 1, 0)]
    fresh = jnp.logical_or(j == 0, expert != prev)

    def weight_copies(e):
        return (pltpu.make_async_copy(wgu_hbm.at[e], gu_stage, gu_sem),
                pltpu.make_async_copy(wdn_hbm.at[e], dn_stage, dn_sem))

    @pl.when(j == 0)
    def _():
        for c in weight_copies(expert):
            c.start()

    @pl.when(jnp.logical_and(fresh, j < na_ref[0]))
    def _():
        for c in weight_copies(expert):
            c.wait()
        src = lax.broadcasted_iota(i32, (2 * LANES, 2 * LANES), 0)
        dst = lax.broadcasted_iota(i32, (2 * LANES, 2 * LANES), 1)
        perm = (src == jnp.where(dst < LANES, 2 * dst, 2 * (dst - LANES) + 1)).astype(bf16)
        for t in range(2 * D_EXPERT // (2 * LANES)):
            w = gu_stage[:, t * 2 * LANES:(t + 1) * 2 * LANES].astype(bf16)
            sep = jnp.dot(w, perm, preferred_element_type=f32)
            wg_scr[:, t * LANES:(t + 1) * LANES] = sep[:, :LANES].astype(bf16)
            wu_scr[:, t * LANES:(t + 1) * LANES] = sep[:, LANES:].astype(bf16)
        wd_scr[...] = dn_stage[...].astype(bf16)

        @pl.when(nxt_ref[j] >= 0)
        def _():
            for c in weight_copies(nxt_ref[j]):
                c.start()

    @pl.when(j < na_ref[0])
    def _():
        left, right = _unpack_rows(x_ref[...])
        xl = left.astype(bf16)
        xr = right.astype(bf16)
        fc = 256
        for c in range(D_EXPERT // fc):
            cols = slice(c * fc, (c + 1) * fc)
            gate = jnp.dot(xl, wg_scr[:HALF, cols], preferred_element_type=f32)
            gate = gate + jnp.dot(xr, wg_scr[HALF:, cols], preferred_element_type=f32) + bg_ref[:, cols]
            up = jnp.dot(xl, wu_scr[:HALF, cols], preferred_element_type=f32)
            up = up + jnp.dot(xr, wu_scr[HALF:, cols], preferred_element_type=f32) + bu_ref[:, cols]
            gate = jnp.minimum(gate, SWIGLU_LIMIT)
            up = jnp.clip(up, -SWIGLU_LIMIT, SWIGLU_LIMIT)
            act_scr[:, cols] = ((up + 1.0) * gate * jax.nn.sigmoid(gate * SWIGLU_ALPHA)).astype(bf16)
        y = jnp.dot(act_scr[...], wd_scr[...], preferred_element_type=f32) + bdn_ref[...]
        y_ref[...] = _pack_rows(y)

    @pl.when(j >= na_ref[0])
    def _():
        y_ref[...] = jnp.zeros_like(y_ref)


def _experts(block_expert, n_active, next_expert, rows, w_gu, bg, bu, w_dn, b_dn):
    n_blocks = rows.shape[0] // EXPERT_BLOCK

    def blk(j, be, na, nxt):
        return (jnp.minimum(j, na[0] - 1), 0)

    def per_expert(shape):
        return pl.BlockSpec((None,) + shape, lambda j, be, na, nxt: (be[j], 0, 0))

    return pl.pallas_call(
        _experts_kernel,
        grid_spec=pltpu.PrefetchScalarGridSpec(
            num_scalar_prefetch=3,
            grid=(n_blocks,),
            in_specs=[
                pl.BlockSpec((EXPERT_BLOCK, HALF), blk),
                pl.BlockSpec(memory_space=pl.ANY),
                per_expert((1, D_EXPERT)),
                per_expert((1, D_EXPERT)),
                pl.BlockSpec(memory_space=pl.ANY),
                per_expert((1, D_MODEL)),
            ],
            out_specs=pl.BlockSpec((EXPERT_BLOCK, HALF), lambda j, be, na, nxt: (j, 0)),
            scratch_shapes=[
                pltpu.VMEM((D_MODEL, 2 * D_EXPERT), f32),
                pltpu.VMEM((D_EXPERT, D_MODEL), f32),
                pltpu.VMEM((D_MODEL, D_EXPERT), bf16),
                pltpu.VMEM((D_MODEL, D_EXPERT), bf16),
                pltpu.VMEM((D_EXPERT, D_MODEL), bf16),
                pltpu.VMEM((EXPERT_BLOCK, D_EXPERT), bf16),
                pltpu.SemaphoreType.DMA(()),
                pltpu.SemaphoreType.DMA(()),
            ],
        ),
        out_shape=jax.ShapeDtypeStruct(rows.shape, u32),
        compiler_params=pltpu.CompilerParams(
            dimension_semantics=("arbitrary",), vmem_limit_bytes=VMEM_LIMIT),
        name="experts",
    )(block_expert, n_active, next_expert, rows, w_gu, bg, bu, w_dn, b_dn)


def _combine_kernel(h_ref, gate_ref, g2_ref, b2_ref, dest_hbm, y_hbm, o_ref,
                    idx_smem, ybuf, idx_sem, row_sem):
    step = pl.program_id(0)
    n_assign = ROW_TILE * TOP_K
    idx_copy = pltpu.make_async_copy(dest_hbm.at[step], idx_smem, idx_sem)
    idx_copy.start()
    idx_copy.wait()

    def issue(t, carry):
        for k in range(TOP_K):
            _row_copy(y_hbm, idx_smem[t * TOP_K + k], ybuf, k * ROW_TILE + t, row_sem).start(priority=k % 2)
        return carry

    lax.fori_loop(0, ROW_TILE, issue, 0, unroll=DMA_UNROLL)

    for k in range(TOP_K):
        rows = pl.ds(k * ROW_TILE, ROW_TILE)
        pltpu.make_async_copy(y_hbm.at[pl.ds(0, ROW_TILE), :], ybuf.at[rows, :], row_sem).wait()

    gates = gate_ref[...]
    left = jnp.zeros((ROW_TILE, HALF), f32)
    right = jnp.zeros((ROW_TILE, HALF), f32)
    for k in range(TOP_K):
        yl, yr = _unpack_rows(ybuf[k * ROW_TILE:(k + 1) * ROW_TILE, :])
        gk = gates[:, k:k + 1]
        left = left + yl * gk
        right = right + yr * gk
    ffn = jnp.concatenate([left, right], axis=1)
    o_ref[...] = _layer_norm(DEEPNORM_ALPHA * h_ref[...] + ffn, g2_ref[...], b2_ref[...])


def _combine(h1, gates, g2, b2, dest_tiles, y_rows):
    n = h1.shape[0]
    vec = pl.BlockSpec((1, D_MODEL), lambda i: (0, 0))
    return pl.pallas_call(
        _combine_kernel,
        grid=(n // ROW_TILE,),
        in_specs=[
            pl.BlockSpec((ROW_TILE, D_MODEL), lambda i: (i, 0)),
            pl.BlockSpec((ROW_TILE, TOP_K), lambda i: (i, 0)),
            vec, vec,
            pl.BlockSpec(memory_space=pl.ANY),
            pl.BlockSpec(memory_space=pl.ANY),
        ],
        out_specs=pl.BlockSpec((ROW_TILE, D_MODEL), lambda i: (i, 0)),
        out_shape=jax.ShapeDtypeStruct((n, D_MODEL), f32),
        scratch_shapes=[
            pltpu.SMEM((ROW_TILE * TOP_K,), i32),
            pltpu.VMEM((TOP_K * ROW_TILE, HALF), u32),
            pltpu.SemaphoreType.DMA(()),
            pltpu.SemaphoreType.DMA(()),
        ],
        compiler_params=pltpu.CompilerParams(
            dimension_semantics=("arbitrary",), vmem_limit_bytes=VMEM_LIMIT),
        name="combine",
    )(h1, gates, g2, b2, dest_tiles, y_rows)


def _rope_tables(pos, scale):
    inv = 1.0 / (ROPE_THETA ** (jnp.arange(0, A_HEAD_DIM, 2, dtype=f32) / A_HEAD_DIM))
    ang = pos.astype(f32)[:, None] * inv[None, :]
    ang = jnp.concatenate([ang, ang, ang, ang], axis=-1)
    sign = jnp.where((jnp.arange(LANES) & 63) < 32, -1.0, 1.0).astype(f32)
    return jnp.cos(ang) * scale, jnp.sin(ang) * sign * scale


def kernel(x, meta_tokens, ln_emb_g, ln_emb_b, w_in, lambda_q1, lambda_k1, lambda_q2, lambda_k2,
           subln_g, hgrn_lb_table, hgrn_norm_g, w_out, ln1_g, ln1_b, w_router, b_router,
           w_gate_up, b_gate_up, w_down, b_down, ln2_g, ln2_b):
    batch, seq, _ = x.shape
    n_tok = batch * seq
    x2 = x.reshape(n_tok, D_MODEL)
    row = lambda v: v.reshape(1, -1).astype(f32)

    w_in_bf = w_in[0].astype(bf16)
    lam = (jnp.exp(jnp.sum(lambda_q1[0].astype(f32) * lambda_k1[0].astype(f32)))
           - jnp.exp(jnp.sum(lambda_q2[0].astype(f32) * lambda_k2[0].astype(f32))) + LAM_INIT).reshape(1)
    lb = jnp.cumsum(jax.nn.softmax(hgrn_lb_table.astype(f32), axis=0), axis=0)[0].reshape(B_HEADS, 1, LANES)
    q_scale = A_HEAD_DIM ** -0.5
    pos_x = N_META + jnp.arange(seq)
    pos_m = jnp.arange(N_META)
    cq, sq = _rope_tables(pos_x, q_scale)
    ck, sk = _rope_tables(pos_x, 1.0)
    cqm, sqm = _rope_tables(pos_m, q_scale)
    ckm, skm = _rope_tables(pos_m, 1.0)
    wr = jnp.pad(w_router[0].astype(f32), ((0, 0), (0, LANES - N_EXPERTS)))
    wr_hi = wr.astype(bf16)
    wr_lo = (wr - wr_hi.astype(f32)).astype(bf16)
    bg = b_gate_up[0][:, 0::2].reshape(N_EXPERTS, 1, D_EXPERT)
    bu = b_gate_up[0][:, 1::2].reshape(N_EXPERTS, 1, D_EXPERT)
    bdn = b_down[0].reshape(N_EXPERTS, 1, D_MODEL)

    g0, b0 = row(ln_emb_g), row(ln_emb_b)
    qkv, hg = _in_proj(x2, g0, b0, w_in_bf, (cq, sq, ck, sk), PROJ_TILE, seq // PROJ_TILE)
    qkv_m, hg_m = _in_proj(meta_tokens, g0, b0, w_in_bf, (cqm, sqm, ckm, skm), N_META, 1)

    ya = _attention(lam, qkv, qkv_m, subln_g[0].reshape(-1, 1).astype(f32), batch, seq)
    yb = _hgrn(hg, hg_m, lb, row(hgrn_norm_g[0]), batch, seq)

    h1, hp, top_idx, gates, rank, counts = _mix_route(
        x2, g0, b0, ya, yb, w_out[0].astype(bf16), row(ln1_g[0]), row(ln1_b[0]),
        wr_hi, wr_lo, b_router[0].reshape(-1, 1).astype(f32))

    counts = counts.reshape(N_EXPERTS)
    blocks = (counts + EXPERT_BLOCK - 1) // EXPERT_BLOCK
    cum_blocks = jnp.cumsum(blocks)
    start_pad = (cum_blocks - blocks) * EXPERT_BLOCK
    n_blocks = n_tok * TOP_K // EXPERT_BLOCK + N_EXPERTS
    n_active = cum_blocks[-1:].astype(i32)
    jb = jnp.arange(n_blocks, dtype=i32)
    block_expert = jnp.sum((cum_blocks[None, :] <= jnp.minimum(jb, n_active - 1)[:, None]).astype(i32), axis=1)
    block_expert = jnp.minimum(block_expert, N_EXPERTS - 1).astype(i32)
    eid = jnp.arange(N_EXPERTS, dtype=i32)
    later_active = jnp.logical_and(blocks[None, :] > 0, eid[None, :] > eid[:, None])
    next_active = jnp.min(jnp.where(later_active, eid[None, :], N_EXPERTS), axis=1)
    next_active = jnp.where(next_active == N_EXPERTS, -1, next_active).astype(i32)
    next_expert = next_active[block_expert]
    hot = top_idx[None, :, :] == eid[:, None, None]
    dest = jnp.sum(jnp.where(hot, start_pad.astype(i32)[:, None, None], 0), axis=0) + rank
    dest_tiles = dest.T.reshape(n_tok // ROW_TILE, ROW_TILE * TOP_K)

    x_rows = _dispatch(hp, dest_tiles, n_blocks * EXPERT_BLOCK)
    y_rows = _experts(block_expert, n_active, next_expert, x_rows, w_gate_up[0], bg, bu, w_down[0], bdn)
    out = _combine(h1, gates.T, row(ln2_g[0]), row(ln2_b[0]), dest_tiles, y_rows)
    return out.reshape(batch, seq, D_MODEL)
```

```python
import functools
import math

import jax
import jax.numpy as jnp
import numpy as np
from jax import lax
from jax.experimental import pallas as pl
from jax.experimental.pallas import tpu as pltpu
from jax.experimental.pallas import tpu_sc as plsc

D_MODEL = 1024
N_META = 16
CHUNK = 64
A_HEADS = 4
A_HEAD_DIM = 64
B_HEADS = 4
B_KEY_DIM = 128
ROPE_THETA = 10000.0
A_W = 512
B_W = 512
D_IN = 3 * A_W + 4 * B_W
N_EXPERTS = 32
TOP_K = 4
D_EXPERT = 1024
SWIGLU_ALPHA = 1.702
SWIGLU_LIMIT = 7.0
EXPERT_BLOCK = 256
DEEPNORM_ALPHA = 2.0 ** 0.25
LN_EPS = 1e-5
RMS_EPS = 1e-5
LAM_INIT = 0.8 - 0.6 * math.exp(0.0)

LANES = 128
HALF = D_MODEL // 2
ROW_TILE = 256
PROJ_TILE = 512
ATT_TILE = 256
HG_TILE = 256
SC_CORES = 2
SC_SUBCORES = 16
SC_CHUNK = 128
VMEM_LIMIT = 56 * 1024 * 1024

_NT = (((1,), (1,)), ((), ()))
_TN = (((0,), (0,)), ((), ()))

f32 = jnp.float32
bf16 = jnp.bfloat16
u32 = jnp.uint32
i32 = jnp.int32


def _layer_norm(x, g, b):
    mu = jnp.mean(x, axis=-1, keepdims=True)
    xc = x - mu
    var = jnp.mean(xc * xc, axis=-1, keepdims=True)
    return xc * lax.rsqrt(var + LN_EPS) * g + b


def _pack_rows(h):
    hb = h.astype(bf16).astype(f32)
    lo = lax.bitcast_convert_type(hb[:, :HALF], u32) >> 16
    hi = lax.bitcast_convert_type(hb[:, HALF:], u32)
    return hi | lo


def _unpack_rows(w):
    left = lax.bitcast_convert_type(w << 16, f32)
    right = lax.bitcast_convert_type(w & jnp.uint32(0xFFFF0000), f32)
    return left, right


def _rope(x, cos, sin_signed, first_half):
    fwd = pltpu.roll(x, 32, axis=1)
    bwd = pltpu.roll(x, x.shape[1] - 32, axis=1)
    return x * cos + jnp.where(first_half, bwd, fwd) * sin_signed


def _in_proj_kernel(x_ref, g_ref, b_ref, w_ref, cq_ref, sq_ref, ck_ref, sk_ref, qkv_ref, hg_ref):
    h = _layer_norm(x_ref[...], g_ref[...], b_ref[...]).astype(bf16)
    lane = lax.broadcasted_iota(i32, (h.shape[0], A_W), 1)
    first_half = (lane & 63) < 32
    heads = lambda t: jnp.concatenate([t[...]] * A_HEADS, axis=1)
    for c, (cos_ref, sin_ref) in enumerate(((cq_ref, sq_ref), (ck_ref, sk_ref))):
        acc = jnp.dot(h, w_ref[:, c * A_W:(c + 1) * A_W], preferred_element_type=f32)
        r = _rope(acc, heads(cos_ref), heads(sin_ref), first_half)
        qkv_ref[:, c * A_W:(c + 1) * A_W] = r.astype(bf16)
    qkv_ref[:, 2 * A_W:3 * A_W] = jnp.dot(
        h, w_ref[:, 2 * A_W:3 * A_W], preferred_element_type=f32).astype(bf16)
    for c in range(4):
        lo = 3 * A_W + c * B_W
        hg_ref[:, c * B_W:(c + 1) * B_W] = jnp.dot(h, w_ref[:, lo:lo + B_W], preferred_element_type=f32)


def _in_proj(x2, g, b, w_bf, tabs, tile, tab_blocks):
    n = x2.shape[0]
    tab_spec = pl.BlockSpec((tile, LANES), lambda i: (i % tab_blocks, 0))
    vec = pl.BlockSpec((1, D_MODEL), lambda i: (0, 0))
    return pl.pallas_call(
        _in_proj_kernel,
        grid=(n // tile,),
        in_specs=[
            pl.BlockSpec((tile, D_MODEL), lambda i: (i, 0)),
            vec, vec,
            pl.BlockSpec((D_MODEL, D_IN), lambda i: (0, 0)),
            tab_spec, tab_spec, tab_spec, tab_spec,
        ],
        out_specs=[
            pl.BlockSpec((tile, 3 * A_W), lambda i: (i, 0)),
            pl.BlockSpec((tile, 4 * B_W), lambda i: (i, 0)),
        ],
        out_shape=[
            jax.ShapeDtypeStruct((n, 3 * A_W), bf16),
            jax.ShapeDtypeStruct((n, 4 * B_W), f32),
        ],
        compiler_params=pltpu.CompilerParams(
            dimension_semantics=("arbitrary",), vmem_limit_bytes=VMEM_LIMIT),
        name="in_proj",
    )(x2, g, b, w_bf, *tabs)


def _attn_kernel(lam_ref, q_ref, k_ref, v_ref, km_ref, vm_ref, g_ref, o_ref):
    tq = ATT_TILE
    lane = lax.broadcasted_iota(i32, (tq, LANES), 1)
    key = lax.broadcasted_iota(i32, (tq, 2 * tq), 0)
    qry = lax.broadcasted_iota(i32, (tq, 2 * tq), 1)
    visible = ((qry & (tq - 1)) >> 6) >= (key >> 6)
    gain = g_ref[...] * (1.0 - LAM_INIT)
    km = km_ref[...]
    vm = vm_ref[...]
    for qi in range(q_ref.shape[0] // tq):
        q = q_ref[qi * tq:(qi + 1) * tq, :]
        zero = jnp.zeros_like(q)
        qs = jnp.concatenate([jnp.where(lane < A_HEAD_DIM, q, zero),
                              jnp.where(lane >= A_HEAD_DIM, q, zero)], axis=0)
        n = (qi + 1) * tq
        s = lax.dot_general(k_ref[0:n, :], qs, _NT, preferred_element_type=f32)
        diag = jnp.where(visible, s[n - tq:, :], -jnp.inf)
        s = diag if qi == 0 else jnp.concatenate([s[:n - tq, :], diag], axis=0)
        sm = lax.dot_general(km, qs, _NT, preferred_element_type=f32)
        m = jnp.maximum(jnp.max(s, axis=0, keepdims=True), jnp.max(sm, axis=0, keepdims=True))
        p = jnp.exp(s - m)
        pm = jnp.exp(sm - m)
        l = jnp.sum(p, axis=0, keepdims=True) + jnp.sum(pm, axis=0, keepdims=True)
        acc = lax.dot_general(v_ref[0:n, :], p.astype(bf16), _TN, preferred_element_type=f32)
        acc = acc + lax.dot_general(vm, pm.astype(bf16), _TN, preferred_element_type=f32)
        inv = 1.0 / l
        o = acc[:, :tq] * inv[:, :tq] - lam_ref[0] * (acc[:, tq:] * inv[:, tq:])
        ms = jnp.mean(o * o, axis=0, keepdims=True)
        o = o * lax.rsqrt(ms + RMS_EPS) * gain
        o_ref[qi * tq:(qi + 1) * tq, :] = o.T.astype(bf16)


def _attention(lam, qkv, qkv_meta, subln_g, batch, seq):
    return pl.pallas_call(
        _attn_kernel,
        grid_spec=pltpu.PrefetchScalarGridSpec(
            num_scalar_prefetch=1,
            grid=(batch, A_HEADS),
            in_specs=[
                pl.BlockSpec((seq, LANES), lambda b, h, lam: (b, h)),
                pl.BlockSpec((seq, LANES), lambda b, h, lam: (b, A_HEADS + h)),
                pl.BlockSpec((seq, LANES), lambda b, h, lam: (b, 2 * A_HEADS + h)),
                pl.BlockSpec((N_META, LANES), lambda b, h, lam: (0, A_HEADS + h)),
                pl.BlockSpec((N_META, LANES), lambda b, h, lam: (0, 2 * A_HEADS + h)),
                pl.BlockSpec((LANES, 1), lambda b, h, lam: (0, 0)),
            ],
            out_specs=pl.BlockSpec((seq, LANES), lambda b, h, lam: (b, h)),
        ),
        out_shape=jax.ShapeDtypeStruct((batch * seq, A_W), bf16),
        compiler_params=pltpu.CompilerParams(
            dimension_semantics=("arbitrary", "arbitrary"), vmem_limit_bytes=VMEM_LIMIT),
        name="attn",
    )(lam, qkv, qkv, qkv, qkv_meta, qkv_meta, subln_g)


def _split3(x):
    a = x.astype(bf16)
    r = x - a.astype(f32)
    b = r.astype(bf16)
    c = (r - b.astype(f32)).astype(bf16)
    return a, b, c


def _chunk_cumsum(tri, x):
    a, b, c = _split3(x)
    out = jnp.dot(tri, a, preferred_element_type=f32)
    out = out + jnp.dot(tri, b, preferred_element_type=f32)
    return out + jnp.dot(tri, c, preferred_element_type=f32)


def _gates(z, lb):
    log_f = jnp.log(lb + (1.0 - lb) * jax.nn.sigmoid(z))
    key = (1.0 - lb) * jax.nn.sigmoid(-z)
    return log_f, key


def _hgrn_kernel(q_ref, f_ref, i_ref, g_ref, fm_ref, im_ref, lb_ref, ng_ref, o_ref,
                 oin_scr, qhat_scr, ut_scr, dec_scr):
    lb = lb_ref[...]
    n_tiles = q_ref.shape[0] // HG_TILE
    per_tile = HG_TILE // CHUNK

    lfm, km = _gates(fm_ref[...], lb)
    r16 = lax.broadcasted_iota(i32, (N_META, N_META), 0)
    c16 = lax.broadcasted_iota(i32, (N_META, N_META), 1)
    bm = _chunk_cumsum((c16 <= r16).astype(bf16), lfm)
    kdm = km * jnp.exp(bm[N_META - 1:N_META, :] - bm)
    st = lax.dot_general(im_ref[...].astype(bf16), kdm.astype(bf16), _TN, preferred_element_type=f32)

    row = lax.broadcasted_iota(i32, (HG_TILE, HG_TILE), 0)
    col = lax.broadcasted_iota(i32, (HG_TILE, HG_TILE), 1)
    causal = ((row >> 6) == (col >> 6)) & (col <= row)
    tri = causal.astype(bf16)

    for t in range(n_tiles):
        rows = slice(t * HG_TILE, (t + 1) * HG_TILE)
        qv = q_ref[rows, :]
        qh = qv * jax.nn.sigmoid(qv) * (B_KEY_DIM ** -0.5)
        log_f, kh = _gates(f_ref[rows, :], lb)
        b = _chunk_cumsum(tri, log_f)
        b3 = b.reshape(per_tile, CHUNK, LANES)
        b_mid = jnp.broadcast_to(b3[:, CHUNK // 2:CHUNK // 2 + 1, :], b3.shape).reshape(HG_TILE, LANES)
        b_last3 = b3[:, CHUNK - 1:CHUNK, :]
        b_last = jnp.broadcast_to(b_last3, b3.shape).reshape(HG_TILE, LANES)
        qt = (qh * jnp.exp(b - b_mid)).astype(bf16)
        kt = (kh * jnp.exp(jnp.minimum(b_mid - b, 80.0))).astype(bf16)
        a = lax.dot_general(qt, kt, _NT, preferred_element_type=f32)
        a = jnp.where(causal, a, 0.0).astype(bf16)
        vv = i_ref[rows, :].astype(bf16)
        oin_scr[rows, :] = jnp.dot(a, vv, preferred_element_type=f32)
        qhat_scr[rows, :] = (qh * jnp.exp(b)).astype(bf16)
        kd = (kh * jnp.exp(b_last - b)).astype(bf16)
        for c in range(per_tile):
            cr = slice(c * CHUNK, (c + 1) * CHUNK)
            n = t * per_tile + c
            ut_scr[n] = lax.dot_general(vv[cr], kd[cr], _TN, preferred_element_type=f32)
            dec_scr[n] = jnp.exp(b_last3[c])

    ng = ng_ref[...]
    for n in range(n_tiles * per_tile):
        rows = slice(n * CHUNK, (n + 1) * CHUNK)
        o = oin_scr[rows, :] + lax.dot_general(qhat_scr[rows, :], st.astype(bf16), _NT,
                                               preferred_element_type=f32)
        ms = jnp.mean(o * o, axis=-1, keepdims=True)
        gv = g_ref[rows, :]
        o_ref[rows, :] = (o * lax.rsqrt(ms + RMS_EPS) * ng * (gv * jax.nn.sigmoid(gv))).astype(bf16)
        st = st * dec_scr[n] + ut_scr[n]


def _hgrn(hg, hg_meta, lb, norm_g, batch, seq):
    n_chunks = seq // CHUNK

    def col(c):
        return pl.BlockSpec((seq, LANES), lambda b, h: (b, c * B_HEADS + h))

    def mcol(c):
        return pl.BlockSpec((N_META, LANES), lambda b, h: (0, c * B_HEADS + h))

    return pl.pallas_call(
        _hgrn_kernel,
        grid=(batch, B_HEADS),
        in_specs=[
            col(0), col(1), col(2), col(3), mcol(1), mcol(2),
            pl.BlockSpec((None, 1, LANES), lambda b, h: (h, 0, 0)),
            pl.BlockSpec((1, LANES), lambda b, h: (0, 0)),
        ],
        out_specs=pl.BlockSpec((seq, LANES), lambda b, h: (b, h)),
        out_shape=jax.ShapeDtypeStruct((batch * seq, B_W), bf16),
        scratch_shapes=[
            pltpu.VMEM((seq, LANES), f32),
            pltpu.VMEM((seq, LANES), bf16),
            pltpu.VMEM((n_chunks, LANES, LANES), f32),
            pltpu.VMEM((n_chunks, 1, LANES), f32),
        ],
        compiler_params=pltpu.CompilerParams(
            dimension_semantics=("arbitrary", "arbitrary"), vmem_limit_bytes=VMEM_LIMIT),
        name="hgrn",
    )(hg, hg, hg, hg, hg_meta, hg_meta, lb, norm_g)


def _mix_route_kernel(x_ref, g0_ref, b0_ref, ya_ref, yb_ref, wo_ref, g1_ref, b1_ref,
                      wr_hi_ref, wr_lo_ref, br_ref,
                      h_ref, hp_ref, idx_ref, gate_ref, rank_ref, cnt_ref, carry_scr):
    step = pl.program_id(0)

    @pl.when(step == 0)
    def _():
        carry_scr[...] = jnp.zeros_like(carry_scr)

    h0 = _layer_norm(x_ref[...], g0_ref[...], b0_ref[...])
    mix = jnp.dot(ya_ref[...], wo_ref[:A_W, :], preferred_element_type=f32)
    mix = mix + jnp.dot(yb_ref[...], wo_ref[A_W:, :], preferred_element_type=f32)
    h1 = _layer_norm(DEEPNORM_ALPHA * h0 + mix, g1_ref[...], b1_ref[...])
    h_ref[...] = h1
    hp_ref[...] = _pack_rows(h1)

    h_hi = h1.astype(bf16)
    h_lo = (h1 - h_hi.astype(f32)).astype(bf16)
    logits = jnp.dot(h_hi, wr_hi_ref[...], preferred_element_type=f32)
    logits = logits + jnp.dot(h_lo, wr_hi_ref[...], preferred_element_type=f32)
    logits = logits + jnp.dot(h_hi, wr_lo_ref[...], preferred_element_type=f32)

    lt = logits.T[:N_EXPERTS, :] + br_ref[...]
    tm = lt.shape[1]
    eid = lax.broadcasted_iota(i32, (N_EXPERTS, tm), 0)
    work = lt
    vals, hots = [], []
    sel = jnp.zeros((N_EXPERTS, tm), f32)
    for k in range(TOP_K):
        m = jnp.max(work, axis=0, keepdims=True)
        first = jnp.min(jnp.where(work == m, eid, N_EXPERTS), axis=0, keepdims=True)
        hot = eid == first
        vals.append(m)
        hots.append(hot)
        idx_ref[k:k + 1, :] = first
        sel = sel + hot.astype(f32)
        work = jnp.where(hot, -jnp.inf, work)

    es = [jnp.exp(v - vals[0]) for v in vals]
    denom = es[0] + es[1] + es[2] + es[3]
    for k in range(TOP_K):
        gate_ref[k:k + 1, :] = es[k] / denom

    r = lax.broadcasted_iota(i32, (tm, tm), 0)
    c = lax.broadcasted_iota(i32, (tm, tm), 1)
    incl = jnp.dot(sel.astype(bf16), (r <= c).astype(bf16), preferred_element_type=f32)
    carry = carry_scr[...]
    excl = incl - sel + carry
    for k in range(TOP_K):
        rank_ref[k:k + 1, :] = jnp.sum(jnp.where(hots[k], excl, 0.0), axis=0, keepdims=True).astype(i32)
    carry = carry + incl[:, tm - 1:tm]
    carry_scr[...] = carry
    cnt_ref[...] = carry.astype(i32)


def _mix_route(x2, g0, b0, ya, yb, wo, g1, b1, wr_hi, wr_lo, br):
    n = x2.shape[0]
    tm = ROW_TILE

    def full(shape):
        return pl.BlockSpec(shape, lambda i: (0,) * len(shape))

    vec = full((1, D_MODEL))
    small = pl.BlockSpec((TOP_K, tm), lambda i: (0, i))
    return pl.pallas_call(
        _mix_route_kernel,
        grid=(n // tm,),
        in_specs=[
            pl.BlockSpec((tm, D_MODEL), lambda i: (i, 0)), vec, vec,
            pl.BlockSpec((tm, A_W), lambda i: (i, 0)),
            pl.BlockSpec((tm, B_W), lambda i: (i, 0)),
            full((D_MODEL, D_MODEL)), vec, vec,
            full((D_MODEL, LANES)), full((D_MODEL, LANES)), full((N_EXPERTS, 1)),
        ],
        out_specs=[
            pl.BlockSpec((tm, D_MODEL), lambda i: (i, 0)),
            pl.BlockSpec((tm, HALF), lambda i: (i, 0)),
            small, small, small,
            full((N_EXPERTS, 1)),
        ],
        out_shape=[
            jax.ShapeDtypeStruct((n, D_MODEL), f32),
            jax.ShapeDtypeStruct((n, HALF), u32),
            jax.ShapeDtypeStruct((TOP_K, n), i32),
            jax.ShapeDtypeStruct((TOP_K, n), f32),
            jax.ShapeDtypeStruct((TOP_K, n), i32),
            jax.ShapeDtypeStruct((N_EXPERTS, 1), i32),
        ],
        scratch_shapes=[pltpu.VMEM((N_EXPERTS, 1), f32)],
        compiler_params=pltpu.CompilerParams(
            dimension_semantics=("arbitrary",), vmem_limit_bytes=VMEM_LIMIT),
        name="mix_route",
    )(x2, g0, b0, ya, yb, wo, g1, b1, wr_hi, wr_lo, br)


def _sc_mesh():
    return plsc.VectorSubcoreMesh(core_axis_name="c", subcore_axis_name="s",
                                  num_cores=SC_CORES, num_subcores=SC_SUBCORES)


def _sc_worker():
    return lax.axis_index("s") * SC_CORES + lax.axis_index("c")


def _dispatch(hp, idx_chunks, n_rows):
    per_worker = hp.shape[0] // SC_CHUNK // (SC_CORES * SC_SUBCORES)

    @functools.partial(
        pl.kernel, mesh=_sc_mesh(),
        out_type=jax.ShapeDtypeStruct((n_rows, HALF), u32),
        scratch_types=[pltpu.VMEM((TOP_K, SC_CHUNK), i32), pltpu.VMEM((SC_CHUNK, HALF), u32),
                       pltpu.SemaphoreType.DMA],
        name="dispatch_sc",
    )
    def scatter_rows(hp_hbm, idx_hbm, rows_hbm, idx_v, rows_v, sem):
        first = _sc_worker() * per_worker

        @pl.loop(0, per_worker)
        def _(i):
            c = first + i
            pltpu.sync_copy(idx_hbm.at[c], idx_v)
            pltpu.sync_copy(hp_hbm.at[pl.ds(c * SC_CHUNK, SC_CHUNK)], rows_v)
            copies = [pltpu.async_copy(rows_v, rows_hbm.at[idx_v.at[k]], sem) for k in range(TOP_K)]
            for copy in copies:
                copy.wait()

    return scatter_rows(hp, idx_chunks)


def _gather(y_rows, idx_chunks, n_tok):
    per_worker = n_tok // SC_CHUNK // (SC_CORES * SC_SUBCORES)

    @functools.partial(
        pl.kernel, mesh=_sc_mesh(),
        out_type=jax.ShapeDtypeStruct((TOP_K, n_tok, HALF), u32),
        scratch_types=[pltpu.VMEM((TOP_K, SC_CHUNK), i32), pltpu.VMEM((SC_CHUNK, HALF), u32),
                       pltpu.SemaphoreType.DMA],
        name="gather_sc",
    )
    def gather_rows(y_hbm, idx_hbm, out_hbm, idx_v, rows_v, sem):
        first = _sc_worker() * per_worker

        @pl.loop(0, per_worker)
        def _(i):
            c = first + i
            pltpu.sync_copy(idx_hbm.at[c], idx_v)
            for k in range(TOP_K):
                pltpu.async_copy(y_hbm.at[idx_v.at[k]], rows_v, sem).wait()
                pltpu.sync_copy(rows_v, out_hbm.at[k, pl.ds(c * SC_CHUNK, SC_CHUNK)])

    return gather_rows(y_rows, idx_chunks)


def _experts_kernel(be_ref, na_ref, nxt_ref, x_ref, wgu_hbm, bg_ref, bu_ref, wdn_hbm, bdn_ref, y_ref,
                    gu_stage, dn_stage, wg_scr, wu_scr, wd_scr, act_scr, gu_sem, dn_sem):
    j = pl.program_id(0)
    expert = be_ref[j]
    prev = be_ref[jnp.maximum(j - 1, 0)]
    fresh = jnp.logical_or(j == 0, expert != prev)

    def weight_copies(e):
        return (pltpu.make_async_copy(wgu_hbm.at[e], gu_stage, gu_sem),
                pltpu.make_async_copy(wdn_hbm.at[e], dn_stage, dn_sem))

    @pl.when(j == 0)
    def _():
        for c in weight_copies(expert):
            c.start()

    @pl.when(jnp.logical_and(fresh, j < na_ref[0]))
    def _():
        for c in weight_copies(expert):
            c.wait()
        src = lax.broadcasted_iota(i32, (2 * LANES, 2 * LANES), 0)
        dst = lax.broadcasted_iota(i32, (2 * LANES, 2 * LANES), 1)
        perm = (src == jnp.where(dst < LANES, 2 * dst, 2 * (dst - LANES) + 1)).astype(bf16)
        for t in range(2 * D_EXPERT // (2 * LANES)):
            w = gu_stage[:, t * 2 * LANES:(t + 1) * 2 * LANES].astype(bf16)
            sep = jnp.dot(w, perm, preferred_element_type=f32)
            wg_scr[:, t * LANES:(t + 1) * LANES] = sep[:, :LANES].astype(bf16)
            wu_scr[:, t * LANES:(t + 1) * LANES] = sep[:, LANES:].astype(bf16)
        wd_scr[...] = dn_stage[...].astype(bf16)

        @pl.when(nxt_ref[j] >= 0)
        def _():
            for c in weight_copies(nxt_ref[j]):
                c.start()

    @pl.when(j < na_ref[0])
    def _():
        left, right = _unpack_rows(x_ref[...])
        xl = left.astype(bf16)
        xr = right.astype(bf16)
        fc = 256
        for c in range(D_EXPERT // fc):
            cols = slice(c * fc, (c + 1) * fc)
            gate = jnp.dot(xl, wg_scr[:HALF, cols], preferred_element_type=f32)
            gate = gate + jnp.dot(xr, wg_scr[HALF:, cols], preferred_element_type=f32) + bg_ref[:, cols]
            up = jnp.dot(xl, wu_scr[:HALF, cols], preferred_element_type=f32)
            up = up + jnp.dot(xr, wu_scr[HALF:, cols], preferred_element_type=f32) + bu_ref[:, cols]
            gate = jnp.minimum(gate, SWIGLU_LIMIT)
            up = jnp.clip(up, -SWIGLU_LIMIT, SWIGLU_LIMIT)
            act_scr[:, cols] = ((up + 1.0) * gate * jax.nn.sigmoid(gate * SWIGLU_ALPHA)).astype(bf16)
        y = jnp.dot(act_scr[...], wd_scr[...], preferred_element_type=f32) + bdn_ref[...]
        y_ref[...] = _pack_rows(y)

    @pl.when(j >= na_ref[0])
    def _():
        y_ref[...] = jnp.zeros_like(y_ref)


def _experts(block_expert, n_active, next_expert, rows, w_gu, bg, bu, w_dn, b_dn):
    n_blocks = rows.shape[0] // EXPERT_BLOCK

    def blk(j, be, na, nxt):
        return (jnp.minimum(j, na[0] - 1), 0)

    def per_expert(shape):
        return pl.BlockSpec((None,) + shape, lambda j, be, na, nxt: (be[j], 0, 0))

    return pl.pallas_call(
        _experts_kernel,
        grid_spec=pltpu.PrefetchScalarGridSpec(
            num_scalar_prefetch=3,
            grid=(n_blocks,),
            in_specs=[
                pl.BlockSpec((EXPERT_BLOCK, HALF), blk),
                pl.BlockSpec(memory_space=pl.ANY),
                per_expert((1, D_EXPERT)),
                per_expert((1, D_EXPERT)),
                pl.BlockSpec(memory_space=pl.ANY),
                per_expert((1, D_MODEL)),
            ],
            out_specs=pl.BlockSpec((EXPERT_BLOCK, HALF), lambda j, be, na, nxt: (j, 0)),
            scratch_shapes=[
                pltpu.VMEM((D_MODEL, 2 * D_EXPERT), f32),
                pltpu.VMEM((D_EXPERT, D_MODEL), f32),
                pltpu.VMEM((D_MODEL, D_EXPERT), bf16),
                pltpu.VMEM((D_MODEL, D_EXPERT), bf16),
                pltpu.VMEM((D_EXPERT, D_MODEL), bf16),
                pltpu.VMEM((EXPERT_BLOCK, D_EXPERT), bf16),
                pltpu.SemaphoreType.DMA(()),
                pltpu.SemaphoreType.DMA(()),
            ],
        ),
        out_shape=jax.ShapeDtypeStruct(rows.shape, u32),
        compiler_params=pltpu.CompilerParams(
            dimension_semantics=("arbitrary",), vmem_limit_bytes=VMEM_LIMIT),
        name="experts",
    )(block_expert, n_active, next_expert, rows, w_gu, bg, bu, w_dn, b_dn)


def _combine_kernel(h_ref, gate_ref, y_ref, g2_ref, b2_ref, o_ref):
    gates = gate_ref[...]
    left = jnp.zeros((ROW_TILE, HALF), f32)
    right = jnp.zeros((ROW_TILE, HALF), f32)
    for k in range(TOP_K):
        yl, yr = _unpack_rows(y_ref[k])
        gk = gates[:, k:k + 1]
        left = left + yl * gk
        right = right + yr * gk
    ffn = jnp.concatenate([left, right], axis=1)
    o_ref[...] = _layer_norm(DEEPNORM_ALPHA * h_ref[...] + ffn, g2_ref[...], b2_ref[...])


def _combine(h1, gates, y_tok, g2, b2):
    n = h1.shape[0]
    vec = pl.BlockSpec((1, D_MODEL), lambda i: (0, 0))
    return pl.pallas_call(
        _combine_kernel,
        grid=(n // ROW_TILE,),
        in_specs=[
            pl.BlockSpec((ROW_TILE, D_MODEL), lambda i: (i, 0)),
            pl.BlockSpec((ROW_TILE, TOP_K), lambda i: (i, 0)),
            pl.BlockSpec((TOP_K, ROW_TILE, HALF), lambda i: (0, i, 0)),
            vec, vec,
        ],
        out_specs=pl.BlockSpec((ROW_TILE, D_MODEL), lambda i: (i, 0)),
        out_shape=jax.ShapeDtypeStruct((n, D_MODEL), f32),
        compiler_params=pltpu.CompilerParams(
            dimension_semantics=("arbitrary",), vmem_limit_bytes=VMEM_LIMIT),
        name="combine",
    )(h1, gates, y_tok, g2, b2)


def _rope_tables(pos, scale):
    inv = 1.0 / (ROPE_THETA ** (jnp.arange(0, A_HEAD_DIM, 2, dtype=f32) / A_HEAD_DIM))
    ang = pos.astype(f32)[:, None] * inv[None, :]
    ang = jnp.concatenate([ang, ang, ang, ang], axis=-1)
    sign = jnp.where((jnp.arange(LANES) & 63) < 32, -1.0, 1.0).astype(f32)
    return jnp.cos(ang) * scale, jnp.sin(ang) * sign * scale


def kernel(x, meta_tokens, ln_emb_g, ln_emb_b, w_in, lambda_q1, lambda_k1, lambda_q2, lambda_k2,
           subln_g, hgrn_lb_table, hgrn_norm_g, w_out, ln1_g, ln1_b, w_router, b_router,
           w_gate_up, b_gate_up, w_down, b_down, ln2_g, ln2_b):
    batch, seq, _ = x.shape
    n_tok = batch * seq
    x2 = x.reshape(n_tok, D_MODEL)
    row = lambda v: v.reshape(1, -1).astype(f32)

    w_in_bf = w_in[0].astype(bf16)
    lam = (jnp.exp(jnp.sum(lambda_q1[0].astype(f32) * lambda_k1[0].astype(f32)))
           - jnp.exp(jnp.sum(lambda_q2[0].astype(f32) * lambda_k2[0].astype(f32))) + LAM_INIT).reshape(1)
    lb = jnp.cumsum(jax.nn.softmax(hgrn_lb_table.astype(f32), axis=0), axis=0)[0].reshape(B_HEADS, 1, LANES)
    q_scale = A_HEAD_DIM ** -0.5
    pos_x = N_META + jnp.arange(seq)
    pos_m = jnp.arange(N_META)
    cq, sq = _rope_tables(pos_x, q_scale)
    ck, sk = _rope_tables(pos_x, 1.0)
    cqm, sqm = _rope_tables(pos_m, q_scale)
    ckm, skm = _rope_tables(pos_m, 1.0)
    wr = jnp.pad(w_router[0].astype(f32), ((0, 0), (0, LANES - N_EXPERTS)))
    wr_hi = wr.astype(bf16)
    wr_lo = (wr - wr_hi.astype(f32)).astype(bf16)
    bg = b_gate_up[0][:, 0::2].reshape(N_EXPERTS, 1, D_EXPERT)
    bu = b_gate_up[0][:, 1::2].reshape(N_EXPERTS, 1, D_EXPERT)
    bdn = b_down[0].reshape(N_EXPERTS, 1, D_MODEL)

    g0, b0 = row(ln_emb_g), row(ln_emb_b)
    qkv, hg = _in_proj(x2, g0, b0, w_in_bf, (cq, sq, ck, sk), PROJ_TILE, seq // PROJ_TILE)
    qkv_m, hg_m = _in_proj(meta_tokens, g0, b0, w_in_bf, (cqm, sqm, ckm, skm), N_META, 1)

    ya = _attention(lam, qkv, qkv_m, subln_g[0].reshape(-1, 1).astype(f32), batch, seq)
    yb = _hgrn(hg, hg_m, lb, row(hgrn_norm_g[0]), batch, seq)

    h1, hp, top_idx, gates, rank, counts = _mix_route(
        x2, g0, b0, ya, yb, w_out[0].astype(bf16), row(ln1_g[0]), row(ln1_b[0]),
        wr_hi, wr_lo, b_router[0].reshape(-1, 1).astype(f32))

    counts = counts.reshape(N_EXPERTS)
    blocks = (counts + EXPERT_BLOCK - 1) // EXPERT_BLOCK
    cum_blocks = jnp.cumsum(blocks)
    start_pad = (cum_blocks - blocks) * EXPERT_BLOCK
    n_blocks = n_tok * TOP_K // EXPERT_BLOCK + N_EXPERTS
    n_active = cum_blocks[-1:].astype(i32)
    jb = jnp.arange(n_blocks, dtype=i32)
    block_expert = jnp.sum((cum_blocks[None, :] <= jnp.minimum(jb, n_active - 1)[:, None]).astype(i32), axis=1)
    block_expert = jnp.minimum(block_expert, N_EXPERTS - 1).astype(i32)
    eid = jnp.arange(N_EXPERTS, dtype=i32)
    later_active = jnp.logical_and(blocks[None, :] > 0, eid[None, :] > eid[:, None])
    next_active = jnp.min(jnp.where(later_active, eid[None, :], N_EXPERTS), axis=1)
    next_active = jnp.where(next_active == N_EXPERTS, -1, next_active).astype(i32)
    next_expert = next_active[block_expert]
    hot = top_idx[None, :, :] == eid[:, None, None]
    dest = jnp.sum(jnp.where(hot, start_pad.astype(i32)[:, None, None], 0), axis=0) + rank
    idx_chunks = dest.reshape(TOP_K, n_tok // SC_CHUNK, SC_CHUNK).transpose(1, 0, 2)

    x_rows = _dispatch(hp, idx_chunks, n_blocks * EXPERT_BLOCK)
    y_rows = _experts(block_expert, n_active, next_expert, x_rows, w_gate_up[0], bg, bu, w_down[0], bdn)
    y_tok = _gather(y_rows, idx_chunks, n_tok)
    out = _combine(h1, gates.T, y_tok, row(ln2_g[0]), row(ln2_b[0]))
    return out.reshape(batch, seq, D_MODEL)
```

```python
import functools
import math

import jax
import jax.numpy as jnp
import numpy as np
from jax import lax
from jax.experimental import pallas as pl
from jax.experimental.pallas import tpu as pltpu
from jax.experimental.pallas import tpu_sc as plsc

D_MODEL = 1024
N_META = 16
CHUNK = 64
A_HEADS = 4
A_HEAD_DIM = 64
B_HEADS = 4
B_KEY_DIM = 128
ROPE_THETA = 10000.0
A_W = 512
B_W = 512
D_IN = 3 * A_W + 4 * B_W
N_EXPERTS = 32
TOP_K = 4
D_EXPERT = 1024
SWIGLU_ALPHA = 1.702
SWIGLU_LIMIT = 7.0
EXPERT_BLOCK = 256
DEEPNORM_ALPHA = 2.0 ** 0.25
LN_EPS = 1e-5
RMS_EPS = 1e-5
LAM_INIT = 0.8 - 0.6 * math.exp(0.0)

LANES = 128
HALF = D_MODEL // 2
ROW_TILE = 256
PROJ_TILE = 512
ATT_TILE = 256
HG_TILE = 256
SC_CORES = 2
SC_SUBCORES = 16
SC_CHUNK = 128
SC_GATHER_CHUNK = 64
COMBINE_PARTS = 4
VMEM_LIMIT = 56 * 1024 * 1024

_NT = (((1,), (1,)), ((), ()))
_TN = (((0,), (0,)), ((), ()))

f32 = jnp.float32
bf16 = jnp.bfloat16
u32 = jnp.uint32
i32 = jnp.int32


def _layer_norm(x, g, b):
    mu = jnp.mean(x, axis=-1, keepdims=True)
    xc = x - mu
    var = jnp.mean(xc * xc, axis=-1, keepdims=True)
    return xc * lax.rsqrt(var + LN_EPS) * g + b


def _pack_rows(h):
    hb = h.astype(bf16).astype(f32)
    lo = lax.bitcast_convert_type(hb[:, :HALF], u32) >> 16
    hi = lax.bitcast_convert_type(hb[:, HALF:], u32)
    return hi | lo


def _unpack_rows(w):
    left = lax.bitcast_convert_type(w << 16, f32)
    right = lax.bitcast_convert_type(w & jnp.uint32(0xFFFF0000), f32)
    return left, right


def _rope(x, cos, sin_signed, first_half):
    fwd = pltpu.roll(x, 32, axis=1)
    bwd = pltpu.roll(x, x.shape[1] - 32, axis=1)
    return x * cos + jnp.where(first_half, bwd, fwd) * sin_signed


def _in_proj_kernel(x_ref, g_ref, b_ref, w_ref, cq_ref, sq_ref, ck_ref, sk_ref, qkv_ref, hg_ref):
    h = _layer_norm(x_ref[...], g_ref[...], b_ref[...]).astype(bf16)
    lane = lax.broadcasted_iota(i32, (h.shape[0], A_W), 1)
    first_half = (lane & 63) < 32
    heads = lambda t: jnp.concatenate([t[...]] * A_HEADS, axis=1)
    for c, (cos_ref, sin_ref) in enumerate(((cq_ref, sq_ref), (ck_ref, sk_ref))):
        acc = jnp.dot(h, w_ref[:, c * A_W:(c + 1) * A_W], preferred_element_type=f32)
        r = _rope(acc, heads(cos_ref), heads(sin_ref), first_half)
        qkv_ref[:, c * A_W:(c + 1) * A_W] = r.astype(bf16)
    qkv_ref[:, 2 * A_W:3 * A_W] = jnp.dot(
        h, w_ref[:, 2 * A_W:3 * A_W], preferred_element_type=f32).astype(bf16)
    for c in range(4):
        lo = 3 * A_W + c * B_W
        hg_ref[:, c * B_W:(c + 1) * B_W] = jnp.dot(h, w_ref[:, lo:lo + B_W], preferred_element_type=f32)


def _in_proj(x2, g, b, w_bf, tabs, tile, tab_blocks):
    n = x2.shape[0]
    tab_spec = pl.BlockSpec((tile, LANES), lambda i: (i % tab_blocks, 0))
    vec = pl.BlockSpec((1, D_MODEL), lambda i: (0, 0))
    return pl.pallas_call(
        _in_proj_kernel,
        grid=(n // tile,),
        in_specs=[
            pl.BlockSpec((tile, D_MODEL), lambda i: (i, 0)),
            vec, vec,
            pl.BlockSpec((D_MODEL, D_IN), lambda i: (0, 0)),
            tab_spec, tab_spec, tab_spec, tab_spec,
        ],
        out_specs=[
            pl.BlockSpec((tile, 3 * A_W), lambda i: (i, 0)),
            pl.BlockSpec((tile, 4 * B_W), lambda i: (i, 0)),
        ],
        out_shape=[
            jax.ShapeDtypeStruct((n, 3 * A_W), bf16),
            jax.ShapeDtypeStruct((n, 4 * B_W), f32),
        ],
        compiler_params=pltpu.CompilerParams(
            dimension_semantics=("arbitrary",), vmem_limit_bytes=VMEM_LIMIT),
        name="in_proj",
    )(x2, g, b, w_bf, *tabs)


def _attn_kernel(lam_ref, q_ref, k_ref, v_ref, km_ref, vm_ref, g_ref, o_ref):
    tq = ATT_TILE
    lane = lax.broadcasted_iota(i32, (tq, LANES), 1)
    key = lax.broadcasted_iota(i32, (tq, 2 * tq), 0)
    qry = lax.broadcasted_iota(i32, (tq, 2 * tq), 1)
    visible = ((qry & (tq - 1)) >> 6) >= (key >> 6)
    gain = g_ref[...] * (1.0 - LAM_INIT)
    km = km_ref[...]
    vm = vm_ref[...]
    for qi in range(q_ref.shape[0] // tq):
        q = q_ref[qi * tq:(qi + 1) * tq, :]
        zero = jnp.zeros_like(q)
        qs = jnp.concatenate([jnp.where(lane < A_HEAD_DIM, q, zero),
                              jnp.where(lane >= A_HEAD_DIM, q, zero)], axis=0)
        n = (qi + 1) * tq
        s = lax.dot_general(k_ref[0:n, :], qs, _NT, preferred_element_type=f32)
        diag = jnp.where(visible, s[n - tq:, :], -jnp.inf)
        s = diag if qi == 0 else jnp.concatenate([s[:n - tq, :], diag], axis=0)
        sm = lax.dot_general(km, qs, _NT, preferred_element_type=f32)
        m = jnp.maximum(jnp.max(s, axis=0, keepdims=True), jnp.max(sm, axis=0, keepdims=True))
        p = jnp.exp2(s - m)
        pm = jnp.exp2(sm - m)
        l = jnp.sum(p, axis=0, keepdims=True) + jnp.sum(pm, axis=0, keepdims=True)
        acc = lax.dot_general(v_ref[0:n, :], p.astype(bf16), _TN, preferred_element_type=f32)
        acc = acc + lax.dot_general(vm, pm.astype(bf16), _TN, preferred_element_type=f32)
        inv = 1.0 / l
        o = acc[:, :tq] * inv[:, :tq] - lam_ref[0] * (acc[:, tq:] * inv[:, tq:])
        ms = jnp.mean(o * o, axis=0, keepdims=True)
        o = o * lax.rsqrt(ms + RMS_EPS) * gain
        o_ref[qi * tq:(qi + 1) * tq, :] = o.T.astype(bf16)


def _attention(lam, qkv, qkv_meta, subln_g, batch, seq):
    return pl.pallas_call(
        _attn_kernel,
        grid_spec=pltpu.PrefetchScalarGridSpec(
            num_scalar_prefetch=1,
            grid=(batch, A_HEADS),
            in_specs=[
                pl.BlockSpec((seq, LANES), lambda b, h, lam: (b, h)),
                pl.BlockSpec((seq, LANES), lambda b, h, lam: (b, A_HEADS + h)),
                pl.BlockSpec((seq, LANES), lambda b, h, lam: (b, 2 * A_HEADS + h)),
                pl.BlockSpec((N_META, LANES), lambda b, h, lam: (0, A_HEADS + h)),
                pl.BlockSpec((N_META, LANES), lambda b, h, lam: (0, 2 * A_HEADS + h)),
                pl.BlockSpec((LANES, 1), lambda b, h, lam: (0, 0)),
            ],
            out_specs=pl.BlockSpec((seq, LANES), lambda b, h, lam: (b, h)),
        ),
        out_shape=jax.ShapeDtypeStruct((batch * seq, A_W), bf16),
        compiler_params=pltpu.CompilerParams(
            dimension_semantics=("arbitrary", "arbitrary"), vmem_limit_bytes=VMEM_LIMIT),
        name="attn",
    )(lam, qkv, qkv, qkv, qkv_meta, qkv_meta, subln_g)


def _split3(x):
    a = x.astype(bf16)
    r = x - a.astype(f32)
    b = r.astype(bf16)
    c = (r - b.astype(f32)).astype(bf16)
    return a, b, c


def _chunk_cumsum(tri, x):
    a, b, c = _split3(x)
    out = jnp.dot(tri, a, preferred_element_type=f32)
    out = out + jnp.dot(tri, b, preferred_element_type=f32)
    return out + jnp.dot(tri, c, preferred_element_type=f32)


def _gates(z, lb):
    sig = jax.nn.sigmoid(z)
    log_f = jnp.log(lb + (1.0 - lb) * sig)
    key = (1.0 - lb) * (1.0 - sig)
    return log_f, key


def _hgrn_kernel(q_ref, f_ref, i_ref, g_ref, fm_ref, im_ref, lb_ref, ng_ref, o_ref,
                 oin_scr, qhat_scr, ut_scr, dec_scr):
    lb = lb_ref[...]
    n_tiles = q_ref.shape[0] // HG_TILE
    per_tile = HG_TILE // CHUNK

    lfm, km = _gates(fm_ref[...], lb)
    r16 = lax.broadcasted_iota(i32, (N_META, N_META), 0)
    c16 = lax.broadcasted_iota(i32, (N_META, N_META), 1)
    bm = _chunk_cumsum((c16 <= r16).astype(bf16), lfm)
    kdm = km * jnp.exp(bm[N_META - 1:N_META, :] - bm)
    st = lax.dot_general(im_ref[...].astype(bf16), kdm.astype(bf16), _TN, preferred_element_type=f32)

    row = lax.broadcasted_iota(i32, (HG_TILE, HG_TILE), 0)
    col = lax.broadcasted_iota(i32, (HG_TILE, HG_TILE), 1)
    causal = ((row >> 6) == (col >> 6)) & (col <= row)
    tri = causal.astype(bf16)

    for t in range(n_tiles):
        rows = slice(t * HG_TILE, (t + 1) * HG_TILE)
        qv = q_ref[rows, :]
        qh = qv * jax.nn.sigmoid(qv) * (B_KEY_DIM ** -0.5)
        log_f, kh = _gates(f_ref[rows, :], lb)
        b = _chunk_cumsum(tri, log_f)
        b3 = b.reshape(per_tile, CHUNK, LANES)
        b_mid = jnp.broadcast_to(b3[:, CHUNK // 2:CHUNK // 2 + 1, :], b3.shape).reshape(HG_TILE, LANES)
        b_last3 = b3[:, CHUNK - 1:CHUNK, :]
        b_last = jnp.broadcast_to(b_last3, b3.shape).reshape(HG_TILE, LANES)
        qt = (qh * jnp.exp(b - b_mid)).astype(bf16)
        kt = (kh * jnp.exp(jnp.minimum(b_mid - b, 80.0))).astype(bf16)
        a = lax.dot_general(qt, kt, _NT, preferred_element_type=f32)
        a = jnp.where(causal, a, 0.0).astype(bf16)
        vv = i_ref[rows, :].astype(bf16)
        oin_scr[rows, :] = jnp.dot(a, vv, preferred_element_type=f32)
        qhat_scr[rows, :] = (qh * jnp.exp(b)).astype(bf16)
        kd = (kh * jnp.exp(b_last - b)).astype(bf16)
        for c in range(per_tile):
            cr = slice(c * CHUNK, (c + 1) * CHUNK)
            n = t * per_tile + c
            ut_scr[n] = lax.dot_general(vv[cr], kd[cr], _TN, preferred_element_type=f32)
            dec_scr[n] = jnp.exp(b_last3[c])

    ng = ng_ref[...]
    for n in range(n_tiles * per_tile):
        rows = slice(n * CHUNK, (n + 1) * CHUNK)
        o = oin_scr[rows, :] + lax.dot_general(qhat_scr[rows, :], st.astype(bf16), _NT,
                                               preferred_element_type=f32)
        ms = jnp.mean(o * o, axis=-1, keepdims=True)
        gv = g_ref[rows, :]
        o_ref[rows, :] = (o * lax.rsqrt(ms + RMS_EPS) * ng * (gv * jax.nn.sigmoid(gv))).astype(bf16)
        st = st * dec_scr[n] + ut_scr[n]


def _hgrn(hg, hg_meta, lb, norm_g, batch, seq):
    n_chunks = seq // CHUNK

    def col(c):
        return pl.BlockSpec((seq, LANES), lambda b, h: (b, c * B_HEADS + h))

    def mcol(c):
        return pl.BlockSpec((N_META, LANES), lambda b, h: (0, c * B_HEADS + h))

    return pl.pallas_call(
        _hgrn_kernel,
        grid=(batch, B_HEADS),
        in_specs=[
            col(0), col(1), col(2), col(3), mcol(1), mcol(2),
            pl.BlockSpec((None, 1, LANES), lambda b, h: (h, 0, 0)),
            pl.BlockSpec((1, LANES), lambda b, h: (0, 0)),
        ],
        out_specs=pl.BlockSpec((seq, LANES), lambda b, h: (b, h)),
        out_shape=jax.ShapeDtypeStruct((batch * seq, B_W), bf16),
        scratch_shapes=[
            pltpu.VMEM((seq, LANES), f32),
            pltpu.VMEM((seq, LANES), bf16),
            pltpu.VMEM((n_chunks, LANES, LANES), f32),
            pltpu.VMEM((n_chunks, 1, LANES), f32),
        ],
        compiler_params=pltpu.CompilerParams(
            dimension_semantics=("arbitrary", "arbitrary"), vmem_limit_bytes=VMEM_LIMIT),
        name="hgrn",
    )(hg, hg, hg, hg, hg_meta, hg_meta, lb, norm_g)


def _mix_route_kernel(x_ref, g0_ref, b0_ref, ya_ref, yb_ref, wo_ref, g1_ref, b1_ref,
                      wr_hi_ref, wr_lo_ref, br_ref,
                      h_ref, hp_ref, idx_ref, gate_ref, rank_ref, cnt_ref, carry_scr):
    step = pl.program_id(0)

    @pl.when(step == 0)
    def _():
        carry_scr[...] = jnp.zeros_like(carry_scr)

    h0 = _layer_norm(x_ref[...], g0_ref[...], b0_ref[...])
    mix = jnp.dot(ya_ref[...], wo_ref[:A_W, :], preferred_element_type=f32)
    mix = mix + jnp.dot(yb_ref[...], wo_ref[A_W:, :], preferred_element_type=f32)
    h1 = _layer_norm(DEEPNORM_ALPHA * h0 + mix, g1_ref[...], b1_ref[...])
    h_ref[...] = h1
    hp_ref[...] = _pack_rows(h1)

    h_hi = h1.astype(bf16)
    h_lo = (h1 - h_hi.astype(f32)).astype(bf16)
    logits = jnp.dot(h_hi, wr_hi_ref[...], preferred_element_type=f32)
    logits = logits + jnp.dot(h_lo, wr_hi_ref[...], preferred_element_type=f32)
    logits = logits + jnp.dot(h_hi, wr_lo_ref[...], preferred_element_type=f32)

    lt = logits.T[:N_EXPERTS, :] + br_ref[...]
    tm = lt.shape[1]
    eid = lax.broadcasted_iota(i32, (N_EXPERTS, tm), 0)
    work = lt
    vals, hots = [], []
    sel = jnp.zeros((N_EXPERTS, tm), f32)
    for k in range(TOP_K):
        m = jnp.max(work, axis=0, keepdims=True)
        first = jnp.min(jnp.where(work == m, eid, N_EXPERTS), axis=0, keepdims=True)
        hot = eid == first
        vals.append(m)
        hots.append(hot)
        idx_ref[k:k + 1, :] = first
        sel = sel + hot.astype(f32)
        work = jnp.where(hot, -jnp.inf, work)

    es = [jnp.exp(v - vals[0]) for v in vals]
    denom = es[0] + es[1] + es[2] + es[3]
    for k in range(TOP_K):
        gate_ref[k:k + 1, :] = es[k] / denom

    r = lax.broadcasted_iota(i32, (tm, tm), 0)
    c = lax.broadcasted_iota(i32, (tm, tm), 1)
    incl = jnp.dot(sel.astype(bf16), (r <= c).astype(bf16), preferred_element_type=f32)
    carry = carry_scr[...]
    excl = incl - sel + carry
    for k in range(TOP_K):
        rank_ref[k:k + 1, :] = jnp.sum(jnp.where(hots[k], excl, 0.0), axis=0, keepdims=True).astype(i32)
    carry = carry + incl[:, tm - 1:tm]
    carry_scr[...] = carry
    cnt_ref[...] = carry.astype(i32)


def _mix_route(x2, g0, b0, ya, yb, wo, g1, b1, wr_hi, wr_lo, br):
    n = x2.shape[0]
    tm = ROW_TILE

    def full(shape):
        return pl.BlockSpec(shape, lambda i: (0,) * len(shape))

    vec = full((1, D_MODEL))
    small = pl.BlockSpec((TOP_K, tm), lambda i: (0, i))
    return pl.pallas_call(
        _mix_route_kernel,
        grid=(n // tm,),
        in_specs=[
            pl.BlockSpec((tm, D_MODEL), lambda i: (i, 0)), vec, vec,
            pl.BlockSpec((tm, A_W), lambda i: (i, 0)),
            pl.BlockSpec((tm, B_W), lambda i: (i, 0)),
            full((D_MODEL, D_MODEL)), vec, vec,
            full((D_MODEL, LANES)), full((D_MODEL, LANES)), full((N_EXPERTS, 1)),
        ],
        out_specs=[
            pl.BlockSpec((tm, D_MODEL), lambda i: (i, 0)),
            pl.BlockSpec((tm, HALF), lambda i: (i, 0)),
            small, small, small,
            full((N_EXPERTS, 1)),
        ],
        out_shape=[
            jax.ShapeDtypeStruct((n, D_MODEL), f32),
            jax.ShapeDtypeStruct((n, HALF), u32),
            jax.ShapeDtypeStruct((TOP_K, n), i32),
            jax.ShapeDtypeStruct((TOP_K, n), f32),
            jax.ShapeDtypeStruct((TOP_K, n), i32),
            jax.ShapeDtypeStruct((N_EXPERTS, 1), i32),
        ],
        scratch_shapes=[pltpu.VMEM((N_EXPERTS, 1), f32)],
        compiler_params=pltpu.CompilerParams(
            dimension_semantics=("arbitrary",), vmem_limit_bytes=VMEM_LIMIT),
        name="mix_route",
    )(x2, g0, b0, ya, yb, wo, g1, b1, wr_hi, wr_lo, br)


def _sc_mesh():
    return plsc.VectorSubcoreMesh(core_axis_name="c", subcore_axis_name="s",
                                  num_cores=SC_CORES, num_subcores=SC_SUBCORES)


def _sc_worker():
    return lax.axis_index("s") * SC_CORES + lax.axis_index("c")


def _dispatch(hp, idx_chunks, n_rows):
    per_worker = hp.shape[0] // SC_CHUNK // (SC_CORES * SC_SUBCORES)

    @functools.partial(
        pl.kernel, mesh=_sc_mesh(),
        out_type=jax.ShapeDtypeStruct((n_rows, HALF), u32),
        scratch_types=[pltpu.VMEM((TOP_K, SC_CHUNK), i32), pltpu.VMEM((SC_CHUNK, HALF), u32),
                       pltpu.SemaphoreType.DMA],
        name="dispatch_sc",
    )
    def scatter_rows(hp_hbm, idx_hbm, rows_hbm, idx_v, rows_v, sem):
        first = _sc_worker() * per_worker

        @pl.loop(0, per_worker)
        def _(i):
            c = first + i
            pltpu.sync_copy(idx_hbm.at[c], idx_v)
            pltpu.sync_copy(hp_hbm.at[pl.ds(c * SC_CHUNK, SC_CHUNK)], rows_v)
            copies = [pltpu.async_copy(rows_v, rows_hbm.at[idx_v.at[k]], sem) for k in range(TOP_K)]
            for copy in copies:
                copy.wait()

    return scatter_rows(hp, idx_chunks)


def _gather(y_rows, idx_chunks, n_tok):
    chunk = idx_chunks.shape[2]
    per_worker = n_tok // chunk // (SC_CORES * SC_SUBCORES)
    row_buf = pltpu.VMEM((chunk, HALF), u32)

    @functools.partial(
        pl.kernel, mesh=_sc_mesh(),
        out_type=jax.ShapeDtypeStruct((TOP_K, n_tok, HALF), u32),
        scratch_types=[pltpu.VMEM((TOP_K, chunk), i32), row_buf, row_buf,
                       pltpu.SemaphoreType.DMA, pltpu.SemaphoreType.DMA,
                       pltpu.SemaphoreType.DMA, pltpu.SemaphoreType.DMA],
        name="gather_sc",
    )
    def gather_rows(y_hbm, idx_hbm, out_hbm, idx_v, buf_a, buf_b, ga, gb, wa, wb):
        first = _sc_worker() * per_worker
        bufs, g_sems, w_sems = (buf_a, buf_b), (ga, gb), (wa, wb)

        @pl.loop(0, per_worker)
        def _(i):
            c = first + i
            pltpu.sync_copy(idx_hbm.at[c], idx_v)
            gathers = [None] * TOP_K
            writes = [None] * TOP_K
            gathers[0] = pltpu.async_copy(y_hbm.at[idx_v.at[0]], bufs[0], g_sems[0])
            for k in range(TOP_K):
                b = k % 2
                gathers[k].wait()
                writes[k] = pltpu.async_copy(bufs[b], out_hbm.at[k, pl.ds(c * chunk, chunk)], w_sems[b])
                if k >= 1:
                    writes[k - 1].wait()
                if k + 1 < TOP_K:
                    gathers[k + 1] = pltpu.async_copy(y_hbm.at[idx_v.at[k + 1]], bufs[1 - b], g_sems[1 - b])
            writes[TOP_K - 1].wait()

    return gather_rows(y_rows, idx_chunks)


def _experts_kernel(be_ref, na_ref, nxt_ref, x_ref, wgu_hbm, bg_ref, bu_ref, wdn_hbm, bdn_ref, y_ref,
                    gu_stage, dn_stage, wg_scr, wu_scr, wd_scr, act_scr, gu_sem, dn_sem):
    j = pl.program_id(0)
    expert = be_ref[j]
    prev = be_ref[jnp.maximum(j - 1, 0)]
    fresh = jnp.logical_or(j == 0, expert != prev)

    def weight_copies(e):
        return (pltpu.make_async_copy(wgu_hbm.at[e], gu_stage, gu_sem),
                pltpu.make_async_copy(wdn_hbm.at[e], dn_stage, dn_sem))

    @pl.when(j == 0)
    def _():
        for c in weight_copies(expert):
            c.start()

    @pl.when(jnp.logical_and(fresh, j < na_ref[0]))
    def _():
        for c in weight_copies(expert):
            c.wait()
        src = lax.broadcasted_iota(i32, (2 * LANES, 2 * LANES), 0)
        dst = lax.broadcasted_iota(i32, (2 * LANES, 2 * LANES), 1)
        perm = (src == jnp.where(dst < LANES, 2 * dst, 2 * (dst - LANES) + 1)).astype(bf16)
        for t in range(2 * D_EXPERT // (2 * LANES)):
            w = gu_stage[:, t * 2 * LANES:(t + 1) * 2 * LANES].astype(bf16)
            sep = jnp.dot(w, perm, preferred_element_type=f32)
            wg_scr[:, t * LANES:(t + 1) * LANES] = sep[:, :LANES].astype(bf16)
            wu_scr[:, t * LANES:(t + 1) * LANES] = sep[:, LANES:].astype(bf16)
        wd_scr[...] = dn_stage[...].astype(bf16)

        @pl.when(nxt_ref[j] >= 0)
        def _():
            for c in weight_copies(nxt_ref[j]):
                c.start(priority=1)

    @pl.when(j < na_ref[0])
    def _():
        left, right = _unpack_rows(x_ref[...])
        xl = left.astype(bf16)
        xr = right.astype(bf16)
        fc = 256
        for c in range(D_EXPERT // fc):
            cols = slice(c * fc, (c + 1) * fc)
            gate = jnp.dot(xl, wg_scr[:HALF, cols], preferred_element_type=f32)
            gate = gate + jnp.dot(xr, wg_scr[HALF:, cols], preferred_element_type=f32) + bg_ref[:, cols]
            up = jnp.dot(xl, wu_scr[:HALF, cols], preferred_element_type=f32)
            up = up + jnp.dot(xr, wu_scr[HALF:, cols], preferred_element_type=f32) + bu_ref[:, cols]
            gate = jnp.minimum(gate, SWIGLU_LIMIT)
            up = jnp.clip(up, -SWIGLU_LIMIT, SWIGLU_LIMIT)
            act_scr[:, cols] = ((up + 1.0) * gate * jax.nn.sigmoid(gate * SWIGLU_ALPHA)).astype(bf16)
        y = jnp.dot(act_scr[...], wd_scr[...], preferred_element_type=f32) + bdn_ref[...]
        y_ref[...] = _pack_rows(y)

    @pl.when(j >= na_ref[0])
    def _():
        y_ref[...] = jnp.zeros_like(y_ref)


def _experts(block_expert, n_active, next_expert, rows, w_gu, bg, bu, w_dn, b_dn):
    n_blocks = rows.shape[0] // EXPERT_BLOCK

    def blk(j, be, na, nxt):
        return (jnp.minimum(j, na[0] - 1), 0)

    def per_expert(shape):
        return pl.BlockSpec((None,) + shape, lambda j, be, na, nxt: (be[j], 0, 0))

    return pl.pallas_call(
        _experts_kernel,
        grid_spec=pltpu.PrefetchScalarGridSpec(
            num_scalar_prefetch=3,
            grid=(n_blocks,),
            in_specs=[
                pl.BlockSpec((EXPERT_BLOCK, HALF), blk),
                pl.BlockSpec(memory_space=pl.ANY),
                per_expert((1, D_EXPERT)),
                per_expert((1, D_EXPERT)),
                pl.BlockSpec(memory_space=pl.ANY),
                per_expert((1, D_MODEL)),
            ],
            out_specs=pl.BlockSpec((EXPERT_BLOCK, HALF), lambda j, be, na, nxt: (j, 0)),
            scratch_shapes=[
                pltpu.VMEM((D_MODEL, 2 * D_EXPERT), f32),
                pltpu.VMEM((D_EXPERT, D_MODEL), f32),
                pltpu.VMEM((D_MODEL, D_EXPERT), bf16),
                pltpu.VMEM((D_MODEL, D_EXPERT), bf16),
                pltpu.VMEM((D_EXPERT, D_MODEL), bf16),
                pltpu.VMEM((EXPERT_BLOCK, D_EXPERT), bf16),
                pltpu.SemaphoreType.DMA(()),
                pltpu.SemaphoreType.DMA(()),
            ],
        ),
        out_shape=jax.ShapeDtypeStruct(rows.shape, u32),
        compiler_params=pltpu.CompilerParams(
            dimension_semantics=("arbitrary",), vmem_limit_bytes=VMEM_LIMIT),
        name="experts",
    )(block_expert, n_active, next_expert, rows, w_gu, bg, bu, w_dn, b_dn)


def _combine_kernel(h_ref, gate_ref, y_ref, g2_ref, b2_ref, *rest):
    o_ref = rest[-1]
    gates = gate_ref[...]
    left = jnp.zeros((ROW_TILE, HALF), f32)
    right = jnp.zeros((ROW_TILE, HALF), f32)
    for k in range(TOP_K):
        yl, yr = _unpack_rows(y_ref[k])
        gk = gates[:, k:k + 1]
        left = left + yl * gk
        right = right + yr * gk
    ffn = jnp.concatenate([left, right], axis=1)
    o_ref[...] = _layer_norm(DEEPNORM_ALPHA * h_ref[...] + ffn, g2_ref[...], b2_ref[...])


def _combine(h1, gates, y_part, g2, b2, part, prev_out):
    n = h1.shape[0]
    tiles = y_part.shape[1] // ROW_TILE
    first = part * tiles
    vec = pl.BlockSpec((1, D_MODEL), lambda i: (0, 0))
    chained = [] if prev_out is None else [prev_out]
    return pl.pallas_call(
        _combine_kernel,
        grid=(tiles,),
        in_specs=[
            pl.BlockSpec((ROW_TILE, D_MODEL), lambda i: (first + i, 0)),
            pl.BlockSpec((ROW_TILE, TOP_K), lambda i: (first + i, 0)),
            pl.BlockSpec((TOP_K, ROW_TILE, HALF), lambda i: (0, i, 0)),
            vec, vec,
        ] + [pl.BlockSpec(memory_space=pl.ANY)] * len(chained),
        out_specs=pl.BlockSpec((ROW_TILE, D_MODEL), lambda i: (first + i, 0)),
        out_shape=jax.ShapeDtypeStruct((n, D_MODEL), f32),
        input_output_aliases={5: 0} if chained else {},
        compiler_params=pltpu.CompilerParams(
            dimension_semantics=("arbitrary",), vmem_limit_bytes=VMEM_LIMIT),
        name="combine",
    )(h1, gates, y_part, g2, b2, *chained)


def _rope_tables(pos, scale):
    inv = 1.0 / (ROPE_THETA ** (jnp.arange(0, A_HEAD_DIM, 2, dtype=f32) / A_HEAD_DIM))
    ang = pos.astype(f32)[:, None] * inv[None, :]
    ang = jnp.concatenate([ang, ang, ang, ang], axis=-1)
    sign = jnp.where((jnp.arange(LANES) & 63) < 32, -1.0, 1.0).astype(f32)
    return jnp.cos(ang) * scale, jnp.sin(ang) * sign * scale


def kernel(x, meta_tokens, ln_emb_g, ln_emb_b, w_in, lambda_q1, lambda_k1, lambda_q2, lambda_k2,
           subln_g, hgrn_lb_table, hgrn_norm_g, w_out, ln1_g, ln1_b, w_router, b_router,
           w_gate_up, b_gate_up, w_down, b_down, ln2_g, ln2_b):
    batch, seq, _ = x.shape
    n_tok = batch * seq
    x2 = x.reshape(n_tok, D_MODEL)
    row = lambda v: v.reshape(1, -1).astype(f32)

    w_in_bf = w_in[0].astype(bf16)
    lam = (jnp.exp(jnp.sum(lambda_q1[0].astype(f32) * lambda_k1[0].astype(f32)))
           - jnp.exp(jnp.sum(lambda_q2[0].astype(f32) * lambda_k2[0].astype(f32))) + LAM_INIT).reshape(1)
    lb = jnp.cumsum(jax.nn.softmax(hgrn_lb_table.astype(f32), axis=0), axis=0)[0].reshape(B_HEADS, 1, LANES)
    q_scale = A_HEAD_DIM ** -0.5 * math.log2(math.e)
    pos_x = N_META + jnp.arange(seq)
    pos_m = jnp.arange(N_META)
    cq, sq = _rope_tables(pos_x, q_scale)
    ck, sk = _rope_tables(pos_x, 1.0)
    cqm, sqm = _rope_tables(pos_m, q_scale)
    ckm, skm = _rope_tables(pos_m, 1.0)
    wr = jnp.pad(w_router[0].astype(f32), ((0, 0), (0, LANES - N_EXPERTS)))
    wr_hi = wr.astype(bf16)
    wr_lo = (wr - wr_hi.astype(f32)).astype(bf16)
    bg = b_gate_up[0][:, 0::2].reshape(N_EXPERTS, 1, D_EXPERT)
    bu = b_gate_up[0][:, 1::2].reshape(N_EXPERTS, 1, D_EXPERT)
    bdn = b_down[0].reshape(N_EXPERTS, 1, D_MODEL)

    g0, b0 = row(ln_emb_g), row(ln_emb_b)
    qkv, hg = _in_proj(x2, g0, b0, w_in_bf, (cq, sq, ck, sk), PROJ_TILE, seq // PROJ_TILE)
    qkv_m, hg_m = _in_proj(meta_tokens, g0, b0, w_in_bf, (cqm, sqm, ckm, skm), N_META, 1)

    ya = _attention(lam, qkv, qkv_m, subln_g[0].reshape(-1, 1).astype(f32), batch, seq)
    yb = _hgrn(hg, hg_m, lb, row(hgrn_norm_g[0]), batch, seq)

    h1, hp, top_idx, gates, rank, counts = _mix_route(
        x2, g0, b0, ya, yb, w_out[0].astype(bf16), row(ln1_g[0]), row(ln1_b[0]),
        wr_hi, wr_lo, b_router[0].reshape(-1, 1).astype(f32))

    counts = counts.reshape(N_EXPERTS)
    blocks = (counts + EXPERT_BLOCK - 1) // EXPERT_BLOCK
    cum_blocks = jnp.cumsum(blocks)
    start_pad = (cum_blocks - blocks) * EXPERT_BLOCK
    n_blocks = n_tok * TOP_K // EXPERT_BLOCK + N_EXPERTS
    n_active = cum_blocks[-1:].astype(i32)
    jb = jnp.arange(n_blocks, dtype=i32)
    block_expert = jnp.sum((cum_blocks[None, :] <= jnp.minimum(jb, n_active - 1)[:, None]).astype(i32), axis=1)
    block_expert = jnp.minimum(block_expert, N_EXPERTS - 1).astype(i32)
    eid = jnp.arange(N_EXPERTS, dtype=i32)
    later_active = jnp.logical_and(blocks[None, :] > 0, eid[None, :] > eid[:, None])
    next_active = jnp.min(jnp.where(later_active, eid[None, :], N_EXPERTS), axis=1)
    next_active = jnp.where(next_active == N_EXPERTS, -1, next_active).astype(i32)
    next_expert = next_active[block_expert]
    hot = top_idx[None, :, :] == eid[:, None, None]
    dest = jnp.sum(jnp.where(hot, start_pad.astype(i32)[:, None, None], 0), axis=0) + rank
    chunked = lambda c: dest.reshape(TOP_K, n_tok // c, c).transpose(1, 0, 2)
    idx_chunks = chunked(SC_CHUNK)

    x_rows = _dispatch(hp, idx_chunks, n_blocks * EXPERT_BLOCK)
    y_rows = _experts(block_expert, n_active, next_expert, x_rows, w_gate_up[0], bg, bu, w_down[0], bdn)
    idx_gather = chunked(SC_GATHER_CHUNK)
    part_tok = n_tok // COMBINE_PARTS
    part_chunks = part_tok // SC_GATHER_CHUNK
    gates_t = gates.T
    out = None
    for part in range(COMBINE_PARTS):
        y_part = _gather(y_rows, idx_gather[part * part_chunks:(part + 1) * part_chunks], part_tok)
        out = _combine(h1, gates_t, y_part, row(ln2_g[0]), row(ln2_b[0]), part, out)
    return out.reshape(batch, seq, D_MODEL)
```

```python
import functools
import math

import jax
import jax.numpy as jnp
import numpy as np
from jax import lax
from jax.experimental import pallas as pl
from jax.experimental.pallas import tpu as pltpu
from jax.experimental.pallas import tpu_sc as plsc

D_MODEL = 1024
N_META = 16
CHUNK = 64
A_HEADS = 4
A_HEAD_DIM = 64
B_HEADS = 4
B_KEY_DIM = 128
ROPE_THETA = 10000.0
A_W = 512
B_W = 512
D_IN = 3 * A_W + 4 * B_W
N_EXPERTS = 32
TOP_K = 4
D_EXPERT = 1024
SWIGLU_ALPHA = 1.702
SWIGLU_LIMIT = 7.0
EXPERT_BLOCK = 256
DEEPNORM_ALPHA = 2.0 ** 0.25
LN_EPS = 1e-5
RMS_EPS = 1e-5
LAM_INIT = 0.8 - 0.6 * math.exp(0.0)

LANES = 128
HALF = D_MODEL // 2
ROW_TILE = 256
PROJ_TILE = 512
ATT_TILE = 256
HG_TILE = 256
GU_CHUNK = 512
SC_CORES = 2
SC_SUBCORES = 16
SC_CHUNK = 128
SC_GATHER_CHUNK = 64
COMBINE_PARTS = 4
VMEM_LIMIT = 56 * 1024 * 1024

_NT = (((1,), (1,)), ((), ()))
_TN = (((0,), (0,)), ((), ()))

f32 = jnp.float32
bf16 = jnp.bfloat16
u32 = jnp.uint32
i32 = jnp.int32


def _layer_norm(x, g, b):
    mu = jnp.mean(x, axis=-1, keepdims=True)
    xc = x - mu
    var = jnp.mean(xc * xc, axis=-1, keepdims=True)
    return xc * lax.rsqrt(var + LN_EPS) * g + b


def _pack_rows(h):
    hb = h.astype(bf16).astype(f32)
    lo = lax.bitcast_convert_type(hb[:, :HALF], u32) >> 16
    hi = lax.bitcast_convert_type(hb[:, HALF:], u32)
    return hi | lo


def _unpack_rows(w):
    left = lax.bitcast_convert_type(w << 16, f32)
    right = lax.bitcast_convert_type(w & jnp.uint32(0xFFFF0000), f32)
    return left, right


def _rope(x, cos, sin_signed, first_half):
    fwd = pltpu.roll(x, 32, axis=1)
    bwd = pltpu.roll(x, x.shape[1] - 32, axis=1)
    return x * cos + jnp.where(first_half, bwd, fwd) * sin_signed


def _in_proj_kernel(x_ref, g_ref, b_ref, w_ref, cq_ref, sq_ref, ck_ref, sk_ref, qkv_ref, hg_ref):
    h = _layer_norm(x_ref[...], g_ref[...], b_ref[...]).astype(bf16)
    lane = lax.broadcasted_iota(i32, (h.shape[0], A_W), 1)
    first_half = (lane & 63) < 32
    heads = lambda t: jnp.concatenate([t[...]] * A_HEADS, axis=1)
    for c, (cos_ref, sin_ref) in enumerate(((cq_ref, sq_ref), (ck_ref, sk_ref))):
        acc = jnp.dot(h, w_ref[:, c * A_W:(c + 1) * A_W], preferred_element_type=f32)
        r = _rope(acc, heads(cos_ref), heads(sin_ref), first_half)
        qkv_ref[:, c * A_W:(c + 1) * A_W] = r.astype(bf16)
    qkv_ref[:, 2 * A_W:3 * A_W] = jnp.dot(
        h, w_ref[:, 2 * A_W:3 * A_W], preferred_element_type=f32).astype(bf16)
    for c in range(4):
        lo = 3 * A_W + c * B_W
        hg_ref[:, c * B_W:(c + 1) * B_W] = jnp.dot(h, w_ref[:, lo:lo + B_W], preferred_element_type=f32)


def _in_proj(x2, g, b, w_bf, tabs, tile, tab_blocks):
    n = x2.shape[0]
    tab_spec = pl.BlockSpec((tile, LANES), lambda i: (i % tab_blocks, 0))
    vec = pl.BlockSpec((1, D_MODEL), lambda i: (0, 0))
    return pl.pallas_call(
        _in_proj_kernel,
        grid=(n // tile,),
        in_specs=[
            pl.BlockSpec((tile, D_MODEL), lambda i: (i, 0)),
            vec, vec,
            pl.BlockSpec((D_MODEL, D_IN), lambda i: (0, 0)),
            tab_spec, tab_spec, tab_spec, tab_spec,
        ],
        out_specs=[
            pl.BlockSpec((tile, 3 * A_W), lambda i: (i, 0)),
            pl.BlockSpec((tile, 4 * B_W), lambda i: (i, 0)),
        ],
        out_shape=[
            jax.ShapeDtypeStruct((n, 3 * A_W), bf16),
            jax.ShapeDtypeStruct((n, 4 * B_W), f32),
        ],
        compiler_params=pltpu.CompilerParams(
            dimension_semantics=("arbitrary",), vmem_limit_bytes=VMEM_LIMIT),
        name="in_proj",
    )(x2, g, b, w_bf, *tabs)


def _attn_kernel(lam_ref, q_ref, k_ref, v_ref, km_ref, vm_ref, g_ref, o_ref):
    tq = ATT_TILE
    lane = lax.broadcasted_iota(i32, (tq, LANES), 1)
    key = lax.broadcasted_iota(i32, (tq, 2 * tq), 0)
    qry = lax.broadcasted_iota(i32, (tq, 2 * tq), 1)
    visible = ((qry & (tq - 1)) >> 6) >= (key >> 6)
    gain = g_ref[...] * (1.0 - LAM_INIT)
    km = km_ref[...]
    vm = vm_ref[...]
    for qi in range(q_ref.shape[0] // tq):
        q = q_ref[qi * tq:(qi + 1) * tq, :]
        zero = jnp.zeros_like(q)
        qs = jnp.concatenate([jnp.where(lane < A_HEAD_DIM, q, zero),
                              jnp.where(lane >= A_HEAD_DIM, q, zero)], axis=0)
        n = (qi + 1) * tq
        s = lax.dot_general(k_ref[0:n, :], qs, _NT, preferred_element_type=f32)
        diag = jnp.where(visible, s[n - tq:, :], -jnp.inf)
        s = diag if qi == 0 else jnp.concatenate([s[:n - tq, :], diag], axis=0)
        sm = lax.dot_general(km, qs, _NT, preferred_element_type=f32)
        m = jnp.maximum(jnp.max(s, axis=0, keepdims=True), jnp.max(sm, axis=0, keepdims=True))
        p = jnp.exp2(s - m)
        pm = jnp.exp2(sm - m)
        l = jnp.sum(p, axis=0, keepdims=True) + jnp.sum(pm, axis=0, keepdims=True)
        acc = lax.dot_general(v_ref[0:n, :], p.astype(bf16), _TN, preferred_element_type=f32)
        acc = acc + lax.dot_general(vm, pm.astype(bf16), _TN, preferred_element_type=f32)
        inv = 1.0 / l
        o = acc[:, :tq] * inv[:, :tq] - lam_ref[0] * (acc[:, tq:] * inv[:, tq:])
        ms = jnp.mean(o * o, axis=0, keepdims=True)
        o = o * lax.rsqrt(ms + RMS_EPS) * gain
        o_ref[qi * tq:(qi + 1) * tq, :] = o.T.astype(bf16)


def _attention(lam, qkv, qkv_meta, subln_g, batch, seq):
    return pl.pallas_call(
        _attn_kernel,
        grid_spec=pltpu.PrefetchScalarGridSpec(
            num_scalar_prefetch=1,
            grid=(batch, A_HEADS),
            in_specs=[
                pl.BlockSpec((seq, LANES), lambda b, h, lam: (b, h)),
                pl.BlockSpec((seq, LANES), lambda b, h, lam: (b, A_HEADS + h)),
                pl.BlockSpec((seq, LANES), lambda b, h, lam: (b, 2 * A_HEADS + h)),
                pl.BlockSpec((N_META, LANES), lambda b, h, lam: (0, A_HEADS + h)),
                pl.BlockSpec((N_META, LANES), lambda b, h, lam: (0, 2 * A_HEADS + h)),
                pl.BlockSpec((LANES, 1), lambda b, h, lam: (0, 0)),
            ],
            out_specs=pl.BlockSpec((seq, LANES), lambda b, h, lam: (b, h)),
        ),
        out_shape=jax.ShapeDtypeStruct((batch * seq, A_W), bf16),
        compiler_params=pltpu.CompilerParams(
            dimension_semantics=("arbitrary", "arbitrary"), vmem_limit_bytes=VMEM_LIMIT),
        name="attn",
    )(lam, qkv, qkv, qkv, qkv_meta, qkv_meta, subln_g)


def _split3(x):
    a = x.astype(bf16)
    r = x - a.astype(f32)
    b = r.astype(bf16)
    c = (r - b.astype(f32)).astype(bf16)
    return a, b, c


def _chunk_cumsum(tri, x):
    a, b, c = _split3(x)
    out = jnp.dot(tri, a, preferred_element_type=f32)
    out = out + jnp.dot(tri, b, preferred_element_type=f32)
    return out + jnp.dot(tri, c, preferred_element_type=f32)


def _gates(z, lb):
    sig = jax.nn.sigmoid(z)
    log_f = jnp.log(lb + (1.0 - lb) * sig)
    key = (1.0 - lb) * (1.0 - sig)
    return log_f, key


def _hgrn_kernel(q_ref, f_ref, i_ref, g_ref, fm_ref, im_ref, lb_ref, ng_ref, o_ref,
                 oin_scr, qhat_scr, ut_scr, dec_scr):
    lb = lb_ref[...]
    n_tiles = q_ref.shape[0] // HG_TILE
    per_tile = HG_TILE // CHUNK

    lfm, km = _gates(fm_ref[...], lb)
    r16 = lax.broadcasted_iota(i32, (N_META, N_META), 0)
    c16 = lax.broadcasted_iota(i32, (N_META, N_META), 1)
    bm = _chunk_cumsum((c16 <= r16).astype(bf16), lfm)
    kdm = km * jnp.exp(bm[N_META - 1:N_META, :] - bm)
    st = lax.dot_general(im_ref[...].astype(bf16), kdm.astype(bf16), _TN, preferred_element_type=f32)

    row = lax.broadcasted_iota(i32, (HG_TILE, HG_TILE), 0)
    col = lax.broadcasted_iota(i32, (HG_TILE, HG_TILE), 1)
    causal = ((row >> 6) == (col >> 6)) & (col <= row)
    tri = causal.astype(bf16)

    for t in range(n_tiles):
        rows = slice(t * HG_TILE, (t + 1) * HG_TILE)
        qv = q_ref[rows, :]
        qh = qv * jax.nn.sigmoid(qv) * (B_KEY_DIM ** -0.5)
        log_f, kh = _gates(f_ref[rows, :], lb)
        b = _chunk_cumsum(tri, log_f)
        b3 = b.reshape(per_tile, CHUNK, LANES)
        b_mid = jnp.broadcast_to(b3[:, CHUNK // 2:CHUNK // 2 + 1, :], b3.shape).reshape(HG_TILE, LANES)
        b_last3 = b3[:, CHUNK - 1:CHUNK, :]
        b_last = jnp.broadcast_to(b_last3, b3.shape).reshape(HG_TILE, LANES)
        qt = (qh * jnp.exp(b - b_mid)).astype(bf16)
        kt = (kh * jnp.exp(jnp.minimum(b_mid - b, 80.0))).astype(bf16)
        a = lax.dot_general(qt, kt, _NT, preferred_element_type=f32)
        a = jnp.where(causal, a, 0.0).astype(bf16)
        vv = i_ref[rows, :].astype(bf16)
        oin_scr[rows, :] = jnp.dot(a, vv, preferred_element_type=f32)
        qhat_scr[rows, :] = (qh * jnp.exp(b)).astype(bf16)
        kd = (kh * jnp.exp(b_last - b)).astype(bf16)
        for c in range(per_tile):
            cr = slice(c * CHUNK, (c + 1) * CHUNK)
            n = t * per_tile + c
            ut_scr[n] = lax.dot_general(vv[cr], kd[cr], _TN, preferred_element_type=f32)
            dec_scr[n] = jnp.exp(b_last3[c])

    ng = ng_ref[...]
    for n in range(n_tiles * per_tile):
        rows = slice(n * CHUNK, (n + 1) * CHUNK)
        o = oin_scr[rows, :] + lax.dot_general(qhat_scr[rows, :], st.astype(bf16), _NT,
                                               preferred_element_type=f32)
        ms = jnp.mean(o * o, axis=-1, keepdims=True)
        gv = g_ref[rows, :]
        o_ref[rows, :] = (o * lax.rsqrt(ms + RMS_EPS) * ng * (gv * jax.nn.sigmoid(gv))).astype(bf16)
        st = st * dec_scr[n] + ut_scr[n]


def _hgrn(hg, hg_meta, lb, norm_g, batch, seq):
    n_chunks = seq // CHUNK

    def col(c):
        return pl.BlockSpec((seq, LANES), lambda b, h: (b, c * B_HEADS + h))

    def mcol(c):
        return pl.BlockSpec((N_META, LANES), lambda b, h: (0, c * B_HEADS + h))

    return pl.pallas_call(
        _hgrn_kernel,
        grid=(batch, B_HEADS),
        in_specs=[
            col(0), col(1), col(2), col(3), mcol(1), mcol(2),
            pl.BlockSpec((None, 1, LANES), lambda b, h: (h, 0, 0)),
            pl.BlockSpec((1, LANES), lambda b, h: (0, 0)),
        ],
        out_specs=pl.BlockSpec((seq, LANES), lambda b, h: (b, h)),
        out_shape=jax.ShapeDtypeStruct((batch * seq, B_W), bf16),
        scratch_shapes=[
            pltpu.VMEM((seq, LANES), f32),
            pltpu.VMEM((seq, LANES), bf16),
            pltpu.VMEM((n_chunks, LANES, LANES), f32),
            pltpu.VMEM((n_chunks, 1, LANES), f32),
        ],
        compiler_params=pltpu.CompilerParams(
            dimension_semantics=("arbitrary", "arbitrary"), vmem_limit_bytes=VMEM_LIMIT),
        name="hgrn",
    )(hg, hg, hg, hg, hg_meta, hg_meta, lb, norm_g)


def _mix_route_kernel(x_ref, g0_ref, b0_ref, ya_ref, yb_ref, wo_ref, g1_ref, b1_ref,
                      wr_ref, br_ref,
                      h_ref, hp_ref, idx_ref, gate_ref, rank_ref, cnt_ref, carry_scr):
    step = pl.program_id(0)

    @pl.when(step == 0)
    def _():
        carry_scr[...] = jnp.zeros_like(carry_scr)

    h0 = _layer_norm(x_ref[...], g0_ref[...], b0_ref[...])
    mix = jnp.dot(ya_ref[...], wo_ref[:A_W, :], preferred_element_type=f32)
    mix = mix + jnp.dot(yb_ref[...], wo_ref[A_W:, :], preferred_element_type=f32)
    h1 = _layer_norm(DEEPNORM_ALPHA * h0 + mix, g1_ref[...], b1_ref[...])
    h_ref[...] = h1
    hp_ref[...] = _pack_rows(h1)

    h_hi = h1.astype(bf16)
    h_lo = (h1 - h_hi.astype(f32)).astype(bf16)
    parts = jnp.dot(h_hi, wr_ref[...], preferred_element_type=f32)
    parts = parts + jnp.dot(h_lo, wr_ref[...], preferred_element_type=f32)

    parts = parts.T
    lt = parts[:N_EXPERTS, :] + parts[N_EXPERTS:2 * N_EXPERTS, :] + br_ref[...]
    tm = lt.shape[1]
    eid = lax.broadcasted_iota(i32, (N_EXPERTS, tm), 0)
    work = lt
    vals, hots = [], []
    sel = jnp.zeros((N_EXPERTS, tm), f32)
    for k in range(TOP_K):
        m = jnp.max(work, axis=0, keepdims=True)
        first = jnp.min(jnp.where(work == m, eid, N_EXPERTS), axis=0, keepdims=True)
        hot = eid == first
        vals.append(m)
        hots.append(hot)
        idx_ref[k:k + 1, :] = first
        sel = sel + hot.astype(f32)
        work = jnp.where(hot, -jnp.inf, work)

    es = [jnp.exp(v - vals[0]) for v in vals]
    denom = es[0] + es[1] + es[2] + es[3]
    for k in range(TOP_K):
        gate_ref[k:k + 1, :] = es[k] / denom

    r = lax.broadcasted_iota(i32, (tm, tm), 0)
    c = lax.broadcasted_iota(i32, (tm, tm), 1)
    incl = jnp.dot(sel.astype(bf16), (r <= c).astype(bf16), preferred_element_type=f32)
    carry = carry_scr[...]
    excl = incl - sel + carry
    for k in range(TOP_K):
        rank_ref[k:k + 1, :] = jnp.sum(jnp.where(hots[k], excl, 0.0), axis=0, keepdims=True).astype(i32)
    carry = carry + incl[:, tm - 1:tm]
    carry_scr[...] = carry
    cnt_ref[...] = carry.astype(i32)


def _mix_route(x2, g0, b0, ya, yb, wo, g1, b1, wr, br):
    n = x2.shape[0]
    tm = ROW_TILE

    def full(shape):
        return pl.BlockSpec(shape, lambda i: (0,) * len(shape))

    vec = full((1, D_MODEL))
    small = pl.BlockSpec((TOP_K, tm), lambda i: (0, i))
    return pl.pallas_call(
        _mix_route_kernel,
        grid=(n // tm,),
        in_specs=[
            pl.BlockSpec((tm, D_MODEL), lambda i: (i, 0)), vec, vec,
            pl.BlockSpec((tm, A_W), lambda i: (i, 0)),
            pl.BlockSpec((tm, B_W), lambda i: (i, 0)),
            full((D_MODEL, D_MODEL)), vec, vec,
            full((D_MODEL, LANES)), full((N_EXPERTS, 1)),
        ],
        out_specs=[
            pl.BlockSpec((tm, D_MODEL), lambda i: (i, 0)),
            pl.BlockSpec((tm, HALF), lambda i: (i, 0)),
            small, small, small,
            full((N_EXPERTS, 1)),
        ],
        out_shape=[
            jax.ShapeDtypeStruct((n, D_MODEL), f32),
            jax.ShapeDtypeStruct((n, HALF), u32),
            jax.ShapeDtypeStruct((TOP_K, n), i32),
            jax.ShapeDtypeStruct((TOP_K, n), f32),
            jax.ShapeDtypeStruct((TOP_K, n), i32),
            jax.ShapeDtypeStruct((N_EXPERTS, 1), i32),
        ],
        scratch_shapes=[pltpu.VMEM((N_EXPERTS, 1), f32)],
        compiler_params=pltpu.CompilerParams(
            dimension_semantics=("arbitrary",), vmem_limit_bytes=VMEM_LIMIT),
        name="mix_route",
    )(x2, g0, b0, ya, yb, wo, g1, b1, wr, br)


def _dest_kernel(idx_ref, rank_ref, start_ref, dest_ref):
    eid = lax.broadcasted_iota(i32, (N_EXPERTS, idx_ref.shape[1]), 0)
    start = start_ref[...]
    for k in range(TOP_K):
        base = jnp.sum(jnp.where(eid == idx_ref[k:k + 1, :], start, 0), axis=0, keepdims=True)
        dest_ref[k:k + 1, :] = base + rank_ref[k:k + 1, :]


def _dest_rows(top_idx, rank, start_pad):
    n = top_idx.shape[1]
    tile = 2048
    spec = pl.BlockSpec((TOP_K, tile), lambda i: (0, i))
    return pl.pallas_call(
        _dest_kernel,
        grid=(n // tile,),
        in_specs=[spec, spec, pl.BlockSpec((N_EXPERTS, 1), lambda i: (0, 0))],
        out_specs=spec,
        out_shape=jax.ShapeDtypeStruct((TOP_K, n), i32),
        compiler_params=pltpu.CompilerParams(dimension_semantics=("arbitrary",)),
        name="dest_rows",
    )(top_idx, rank, start_pad.reshape(N_EXPERTS, 1).astype(i32))


def _sc_mesh():
    return plsc.VectorSubcoreMesh(core_axis_name="c", subcore_axis_name="s",
                                  num_cores=SC_CORES, num_subcores=SC_SUBCORES)


def _sc_worker():
    return lax.axis_index("s") * SC_CORES + lax.axis_index("c")


def _dispatch(hp, idx_chunks, n_rows):
    per_worker = hp.shape[0] // SC_CHUNK // (SC_CORES * SC_SUBCORES)

    @functools.partial(
        pl.kernel, mesh=_sc_mesh(),
        out_type=jax.ShapeDtypeStruct((n_rows, HALF), u32),
        scratch_types=[pltpu.VMEM((TOP_K, SC_CHUNK), i32), pltpu.VMEM((SC_CHUNK, HALF), u32),
                       pltpu.SemaphoreType.DMA],
        name="dispatch_sc",
    )
    def scatter_rows(hp_hbm, idx_hbm, rows_hbm, idx_v, rows_v, sem):
        first = _sc_worker() * per_worker

        @pl.loop(0, per_worker)
        def _(i):
            c = first + i
            pltpu.sync_copy(idx_hbm.at[c], idx_v)
            pltpu.sync_copy(hp_hbm.at[pl.ds(c * SC_CHUNK, SC_CHUNK)], rows_v)
            copies = [pltpu.async_copy(rows_v, rows_hbm.at[idx_v.at[k]], sem) for k in range(TOP_K)]
            for copy in copies:
                copy.wait()

    return scatter_rows(hp, idx_chunks)


def _gather(y_rows, idx_chunks, n_tok):
    chunk = idx_chunks.shape[2]
    per_worker = n_tok // chunk // (SC_CORES * SC_SUBCORES)
    row_buf = pltpu.VMEM((chunk, HALF), u32)

    @functools.partial(
        pl.kernel, mesh=_sc_mesh(),
        out_type=jax.ShapeDtypeStruct((TOP_K, n_tok, HALF), u32),
        scratch_types=[pltpu.VMEM((TOP_K, chunk), i32), row_buf, row_buf,
                       pltpu.SemaphoreType.DMA, pltpu.SemaphoreType.DMA,
                       pltpu.SemaphoreType.DMA, pltpu.SemaphoreType.DMA],
        name="gather_sc",
    )
    def gather_rows(y_hbm, idx_hbm, out_hbm, idx_v, buf_a, buf_b, ga, gb, wa, wb):
        first = _sc_worker() * per_worker
        bufs, g_sems, w_sems = (buf_a, buf_b), (ga, gb), (wa, wb)

        @pl.loop(0, per_worker)
        def _(i):
            c = first + i
            pltpu.sync_copy(idx_hbm.at[c], idx_v)
            gathers = [None] * TOP_K
            writes = [None] * TOP_K
            gathers[0] = pltpu.async_copy(y_hbm.at[idx_v.at[0]], bufs[0], g_sems[0])
            for k in range(TOP_K):
                b = k % 2
                gathers[k].wait()
                writes[k] = pltpu.async_copy(bufs[b], out_hbm.at[k, pl.ds(c * chunk, chunk)], w_sems[b])
                if k >= 1:
                    writes[k - 1].wait()
                if k + 1 < TOP_K:
                    gathers[k + 1] = pltpu.async_copy(y_hbm.at[idx_v.at[k + 1]], bufs[1 - b], g_sems[1 - b])
            writes[TOP_K - 1].wait()

    return gather_rows(y_rows, idx_chunks)


def _pair_gate_up(a0, a1, even):
    gate = jnp.where(even, a0, pltpu.roll(a1, 1, axis=1))
    up = jnp.where(even, pltpu.roll(a0, LANES - 1, axis=1), a1)
    return gate, up


def _riffle(v):
    g = v.reshape(v.shape[:-1] + (-1, 2, LANES // 2))
    return jnp.swapaxes(g, -1, -2).reshape(v.shape)


def _experts_kernel(be_ref, na_ref, nxt_ref, x_ref, wgu_hbm, bg_ref, bu_ref, wdn_hbm, bdn_ref, y_ref,
                    gu_stage, dn_stage, wgu_scr, wd_scr, riffle_scr, act_scr, gu_sem, dn_sem):
    j = pl.program_id(0)
    expert = be_ref[j]
    prev = be_ref[jnp.maximum(j - 1, 0)]
    fresh = jnp.logical_or(j == 0, expert != prev)

    def weight_copies(e):
        return (pltpu.make_async_copy(wgu_hbm.at[e], gu_stage, gu_sem),
                pltpu.make_async_copy(wdn_hbm.at[e], dn_stage, dn_sem))

    @pl.when(j == 0)
    def _():
        for c in weight_copies(expert):
            c.start()

    @pl.when(jnp.logical_and(fresh, j < na_ref[0]))
    def _():
        for c in weight_copies(expert):
            c.wait()
        for t in range(2 * D_EXPERT // GU_CHUNK):
            cols = slice(t * GU_CHUNK, (t + 1) * GU_CHUNK)
            wgu_scr[:, cols] = gu_stage[:, cols].astype(bf16)
        half = LANES // 2
        for g in range(D_EXPERT // LANES):
            rows = slice(g * LANES, (g + 1) * LANES)
            for s in range(D_MODEL // LANES):
                lanes = slice(s * LANES, (s + 1) * LANES)
                riffle_scr[s, pl.ds(0, half, stride=2), :] = dn_stage[g * LANES:g * LANES + half, lanes]
                riffle_scr[s, pl.ds(1, half, stride=2), :] = dn_stage[g * LANES + half:(g + 1) * LANES, lanes]
                wd_scr[rows, lanes] = riffle_scr[s].astype(bf16)

        @pl.when(nxt_ref[j] >= 0)
        def _():
            for c in weight_copies(nxt_ref[j]):
                c.start(priority=1)

    @pl.when(j < na_ref[0])
    def _():
        left, right = _unpack_rows(x_ref[...])
        xl = left.astype(bf16)
        xr = right.astype(bf16)
        even = (lax.broadcasted_iota(i32, (EXPERT_BLOCK, LANES), 1) & 1) == 0
        for c in range(2 * D_EXPERT // GU_CHUNK):
            cols = slice(c * GU_CHUNK, (c + 1) * GU_CHUNK)
            gu = jnp.dot(xl, wgu_scr[:HALF, cols], preferred_element_type=f32)
            gu = gu + jnp.dot(xr, wgu_scr[HALF:, cols], preferred_element_type=f32)
            for h in range(GU_CHUNK // (2 * LANES)):
                a0 = gu[:, 2 * h * LANES:(2 * h + 1) * LANES]
                a1 = gu[:, (2 * h + 1) * LANES:(2 * h + 2) * LANES]
                out = slice(c * GU_CHUNK // 2 + h * LANES, c * GU_CHUNK // 2 + (h + 1) * LANES)
                gate, up = _pair_gate_up(a0, a1, even)
                gate = jnp.minimum(gate + bg_ref[:, out], SWIGLU_LIMIT)
                up = jnp.clip(up + bu_ref[:, out], -SWIGLU_LIMIT, SWIGLU_LIMIT)
                act_scr[:, out] = ((up + 1.0) * gate * jax.nn.sigmoid(gate * SWIGLU_ALPHA)).astype(bf16)
        y = jnp.dot(act_scr[...], wd_scr[...], preferred_element_type=f32) + bdn_ref[...]
        y_ref[...] = _pack_rows(y)

    @pl.when(j >= na_ref[0])
    def _():
        y_ref[...] = jnp.zeros_like(y_ref)


def _experts(block_expert, n_active, next_expert, rows, w_gu, bg, bu, w_dn, b_dn):
    n_blocks = rows.shape[0] // EXPERT_BLOCK

    def blk(j, be, na, nxt):
        return (jnp.minimum(j, na[0] - 1), 0)

    def per_expert(shape):
        return pl.BlockSpec((None,) + shape, lambda j, be, na, nxt: (be[j], 0, 0))

    return pl.pallas_call(
        _experts_kernel,
        grid_spec=pltpu.PrefetchScalarGridSpec(
            num_scalar_prefetch=3,
            grid=(n_blocks,),
            in_specs=[
                pl.BlockSpec((EXPERT_BLOCK, HALF), blk),
                pl.BlockSpec(memory_space=pl.ANY),
                per_expert((1, D_EXPERT)),
                per_expert((1, D_EXPERT)),
                pl.BlockSpec(memory_space=pl.ANY),
                per_expert((1, D_MODEL)),
            ],
            out_specs=pl.BlockSpec((EXPERT_BLOCK, HALF), lambda j, be, na, nxt: (j, 0)),
            scratch_shapes=[
                pltpu.VMEM((D_MODEL, 2 * D_EXPERT), f32),
                pltpu.VMEM((D_EXPERT, D_MODEL), f32),
                pltpu.VMEM((D_MODEL, 2 * D_EXPERT), bf16),
                pltpu.VMEM((D_EXPERT, D_MODEL), bf16),
                pltpu.VMEM((D_MODEL // LANES, LANES, LANES), f32),
                pltpu.VMEM((EXPERT_BLOCK, D_EXPERT), bf16),
                pltpu.SemaphoreType.DMA(()),
                pltpu.SemaphoreType.DMA(()),
            ],
        ),
        out_shape=jax.ShapeDtypeStruct(rows.shape, u32),
        compiler_params=pltpu.CompilerParams(
            dimension_semantics=("arbitrary",), vmem_limit_bytes=VMEM_LIMIT),
        name="experts",
    )(block_expert, n_active, next_expert, rows, w_gu, bg, bu, w_dn, b_dn)


def _combine_kernel(h_ref, gate_ref, y_ref, g2_ref, b2_ref, *rest):
    o_ref = rest[-1]
    gates = gate_ref[...]
    left = jnp.zeros((ROW_TILE, HALF), f32)
    right = jnp.zeros((ROW_TILE, HALF), f32)
    for k in range(TOP_K):
        yl, yr = _unpack_rows(y_ref[k])
        gk = gates[:, k:k + 1]
        left = left + yl * gk
        right = right + yr * gk
    ffn = jnp.concatenate([left, right], axis=1)
    o_ref[...] = _layer_norm(DEEPNORM_ALPHA * h_ref[...] + ffn, g2_ref[...], b2_ref[...])


def _combine(h1, gates, y_part, g2, b2, part, prev_out):
    n = h1.shape[0]
    tiles = y_part.shape[1] // ROW_TILE
    first = part * tiles
    vec = pl.BlockSpec((1, D_MODEL), lambda i: (0, 0))
    chained = [] if prev_out is None else [prev_out]
    return pl.pallas_call(
        _combine_kernel,
        grid=(tiles,),
        in_specs=[
            pl.BlockSpec((ROW_TILE, D_MODEL), lambda i: (first + i, 0)),
            pl.BlockSpec((ROW_TILE, TOP_K), lambda i: (first + i, 0)),
            pl.BlockSpec((TOP_K, ROW_TILE, HALF), lambda i: (0, i, 0)),
            vec, vec,
        ] + [pl.BlockSpec(memory_space=pl.ANY)] * len(chained),
        out_specs=pl.BlockSpec((ROW_TILE, D_MODEL), lambda i: (first + i, 0)),
        out_shape=jax.ShapeDtypeStruct((n, D_MODEL), f32),
        input_output_aliases={5: 0} if chained else {},
        compiler_params=pltpu.CompilerParams(
            dimension_semantics=("arbitrary",), vmem_limit_bytes=VMEM_LIMIT),
        name="combine",
    )(h1, gates, y_part, g2, b2, *chained)


def _rope_tables(pos, scale):
    inv = 1.0 / (ROPE_THETA ** (jnp.arange(0, A_HEAD_DIM, 2, dtype=f32) / A_HEAD_DIM))
    ang = pos.astype(f32)[:, None] * inv[None, :]
    ang = jnp.concatenate([ang, ang, ang, ang], axis=-1)
    sign = jnp.where((jnp.arange(LANES) & 63) < 32, -1.0, 1.0).astype(f32)
    return jnp.cos(ang) * scale, jnp.sin(ang) * sign * scale


def kernel(x, meta_tokens, ln_emb_g, ln_emb_b, w_in, lambda_q1, lambda_k1, lambda_q2, lambda_k2,
           subln_g, hgrn_lb_table, hgrn_norm_g, w_out, ln1_g, ln1_b, w_router, b_router,
           w_gate_up, b_gate_up, w_down, b_down, ln2_g, ln2_b):
    batch, seq, _ = x.shape
    n_tok = batch * seq
    x2 = x.reshape(n_tok, D_MODEL)
    row = lambda v: v.reshape(1, -1).astype(f32)

    w_in_bf = w_in[0].astype(bf16)
    lam = (jnp.exp(jnp.sum(lambda_q1[0].astype(f32) * lambda_k1[0].astype(f32)))
           - jnp.exp(jnp.sum(lambda_q2[0].astype(f32) * lambda_k2[0].astype(f32))) + LAM_INIT).reshape(1)
    lb = jnp.cumsum(jax.nn.softmax(hgrn_lb_table.astype(f32), axis=0), axis=0)[0].reshape(B_HEADS, 1, LANES)
    q_scale = A_HEAD_DIM ** -0.5 * math.log2(math.e)
    pos_x = N_META + jnp.arange(seq)
    pos_m = jnp.arange(N_META)
    cq, sq = _rope_tables(pos_x, q_scale)
    ck, sk = _rope_tables(pos_x, 1.0)
    cqm, sqm = _rope_tables(pos_m, q_scale)
    ckm, skm = _rope_tables(pos_m, 1.0)
    wr = w_router[0].astype(f32)
    wr_hi = wr.astype(bf16)
    wr_lo = (wr - wr_hi.astype(f32)).astype(bf16)
    wr_split = jnp.pad(jnp.concatenate([wr_hi, wr_lo], axis=1), ((0, 0), (0, LANES - 2 * N_EXPERTS)))
    bg = _riffle(b_gate_up[0][:, 0::2]).reshape(N_EXPERTS, 1, D_EXPERT)
    bu = _riffle(b_gate_up[0][:, 1::2]).reshape(N_EXPERTS, 1, D_EXPERT)
    bdn = b_down[0].reshape(N_EXPERTS, 1, D_MODEL)

    g0, b0 = row(ln_emb_g), row(ln_emb_b)
    qkv, hg = _in_proj(x2, g0, b0, w_in_bf, (cq, sq, ck, sk), PROJ_TILE, seq // PROJ_TILE)
    qkv_m, hg_m = _in_proj(meta_tokens, g0, b0, w_in_bf, (cqm, sqm, ckm, skm), N_META, 1)

    ya = _attention(lam, qkv, qkv_m, subln_g[0].reshape(-1, 1).astype(f32), batch, seq)
    yb = _hgrn(hg, hg_m, lb, row(hgrn_norm_g[0]), batch, seq)

    h1, hp, top_idx, gates, rank, counts = _mix_route(
        x2, g0, b0, ya, yb, w_out[0].astype(bf16), row(ln1_g[0]), row(ln1_b[0]),
        wr_split, b_router[0].reshape(-1, 1).astype(f32))

    counts = counts.reshape(N_EXPERTS)
    blocks = (counts + EXPERT_BLOCK - 1) // EXPERT_BLOCK
    cum_blocks = jnp.cumsum(blocks)
    start_pad = (cum_blocks - blocks) * EXPERT_BLOCK
    n_blocks = n_tok * TOP_K // EXPERT_BLOCK + N_EXPERTS
    n_active = cum_blocks[-1:].astype(i32)
    jb = jnp.arange(n_blocks, dtype=i32)
    block_expert = jnp.sum((cum_blocks[None, :] <= jnp.minimum(jb, n_active - 1)[:, None]).astype(i32), axis=1)
    block_expert = jnp.minimum(block_expert, N_EXPERTS - 1).astype(i32)
    eid = jnp.arange(N_EXPERTS, dtype=i32)
    later_active = jnp.logical_and(blocks[None, :] > 0, eid[None, :] > eid[:, None])
    next_active = jnp.min(jnp.where(later_active, eid[None, :], N_EXPERTS), axis=1)
    next_active = jnp.where(next_active == N_EXPERTS, -1, next_active).astype(i32)
    next_expert = next_active[block_expert]
    dest = _dest_rows(top_idx, rank, start_pad)
    chunked = lambda c: dest.reshape(TOP_K, n_tok // c, c).transpose(1, 0, 2)
    idx_chunks = chunked(SC_CHUNK)

    x_rows = _dispatch(hp, idx_chunks, n_blocks * EXPERT_BLOCK)
    y_rows = _experts(block_expert, n_active, next_expert, x_rows, w_gate_up[0], bg, bu, w_down[0], bdn)
    idx_gather = chunked(SC_GATHER_CHUNK)
    part_tok = n_tok // COMBINE_PARTS
    part_chunks = part_tok // SC_GATHER_CHUNK
    gates_t = gates.T
    out = None
    for part in range(COMBINE_PARTS):
        y_part = _gather(y_rows, idx_gather[part * part_chunks:(part + 1) * part_chunks], part_tok)
        out = _combine(h1, gates_t, y_part, row(ln2_g[0]), row(ln2_b[0]), part, out)
    return out.reshape(batch, seq, D_MODEL)
```

```python
import functools
import math

import jax
import jax.numpy as jnp
import numpy as np
from jax import lax
from jax.experimental import pallas as pl
from jax.experimental.pallas import tpu as pltpu
from jax.experimental.pallas import tpu_sc as plsc

D_MODEL = 1024
N_META = 16
CHUNK = 64
A_HEADS = 4
A_HEAD_DIM = 64
B_HEADS = 4
B_KEY_DIM = 128
ROPE_THETA = 10000.0
A_W = 512
B_W = 512
D_IN = 3 * A_W + 4 * B_W
N_EXPERTS = 32
TOP_K = 4
D_EXPERT = 1024
SWIGLU_ALPHA = 1.702
SWIGLU_LIMIT = 7.0
EXPERT_BLOCK = 256
DEEPNORM_ALPHA = 2.0 ** 0.25
LN_EPS = 1e-5
RMS_EPS = 1e-5
LAM_INIT = 0.8 - 0.6 * math.exp(0.0)

LANES = 128
HALF = D_MODEL // 2
ROW_TILE = 256
ROUTE_TILE = 2 * ROW_TILE
PROJ_TILE = 512
ATT_TILE = 256
HG_TILE = 256
GU_CHUNK = 512
GATE_ROWS = 8
SCORES_AHEAD = 2
SC_CORES = 2
SC_SUBCORES = 16
SC_CHUNK = 128
SC_GATHER_CHUNK = 64
COMBINE_PARTS = 4
VMEM_LIMIT = 56 * 1024 * 1024

_NT = (((1,), (1,)), ((), ()))
_TN = (((0,), (0,)), ((), ()))

f32 = jnp.float32
bf16 = jnp.bfloat16
u32 = jnp.uint32
i32 = jnp.int32


def _layer_norm(x, g, b):
    mu = jnp.mean(x, axis=-1, keepdims=True)
    xc = x - mu
    var = jnp.mean(xc * xc, axis=-1, keepdims=True)
    return xc * lax.rsqrt(var + LN_EPS) * g + b


def _pack_rows(h):
    hb = h.astype(bf16).astype(f32)
    lo = lax.bitcast_convert_type(hb[:, :HALF], u32) >> 16
    hi = lax.bitcast_convert_type(hb[:, HALF:], u32)
    return hi | lo


def _unpack_rows(w):
    left = lax.bitcast_convert_type(w << 16, f32)
    right = lax.bitcast_convert_type(w & jnp.uint32(0xFFFF0000), f32)
    return left, right


def _rope(x, cos, sin_signed, first_half):
    fwd = pltpu.roll(x, 32, axis=1)
    bwd = pltpu.roll(x, x.shape[1] - 32, axis=1)
    return x * cos + jnp.where(first_half, bwd, fwd) * sin_signed


def _in_proj_kernel(x_ref, g_ref, b_ref, w_ref, cq_ref, sq_ref, ck_ref, sk_ref, qkv_ref, hg_ref):
    h = _layer_norm(x_ref[...], g_ref[...], b_ref[...]).astype(bf16)
    lane = lax.broadcasted_iota(i32, (h.shape[0], A_W), 1)
    first_half = (lane & 63) < 32
    heads = lambda t: jnp.concatenate([t[...]] * A_HEADS, axis=1)
    for c, (cos_ref, sin_ref) in enumerate(((cq_ref, sq_ref), (ck_ref, sk_ref))):
        acc = jnp.dot(h, w_ref[:, c * A_W:(c + 1) * A_W], preferred_element_type=f32)
        r = _rope(acc, heads(cos_ref), heads(sin_ref), first_half)
        qkv_ref[:, c * A_W:(c + 1) * A_W] = r.astype(bf16)
    qkv_ref[:, 2 * A_W:3 * A_W] = jnp.dot(
        h, w_ref[:, 2 * A_W:3 * A_W], preferred_element_type=f32).astype(bf16)
    for c in range(4):
        lo = 3 * A_W + c * B_W
        hg_ref[:, c * B_W:(c + 1) * B_W] = jnp.dot(h, w_ref[:, lo:lo + B_W], preferred_element_type=f32)


def _in_proj(x2, g, b, w_bf, tabs, tile, tab_blocks):
    n = x2.shape[0]
    tab_spec = pl.BlockSpec((tile, LANES), lambda i: (i % tab_blocks, 0))
    vec = pl.BlockSpec((1, D_MODEL), lambda i: (0, 0))
    return pl.pallas_call(
        _in_proj_kernel,
        grid=(n // tile,),
        in_specs=[
            pl.BlockSpec((tile, D_MODEL), lambda i: (i, 0)),
            vec, vec,
            pl.BlockSpec((D_MODEL, D_IN), lambda i: (0, 0)),
            tab_spec, tab_spec, tab_spec, tab_spec,
        ],
        out_specs=[
            pl.BlockSpec((tile, 3 * A_W), lambda i: (i, 0)),
            pl.BlockSpec((tile, 4 * B_W), lambda i: (i, 0)),
        ],
        out_shape=[
            jax.ShapeDtypeStruct((n, 3 * A_W), bf16),
            jax.ShapeDtypeStruct((n, 4 * B_W), f32),
        ],
        compiler_params=pltpu.CompilerParams(
            dimension_semantics=("arbitrary",), vmem_limit_bytes=VMEM_LIMIT),
        name="in_proj",
    )(x2, g, b, w_bf, *tabs)


def _attn_kernel(lam_ref, q_ref, k_ref, v_ref, km_ref, vm_ref, g_ref, o_ref):
    tq = ATT_TILE
    lane = lax.broadcasted_iota(i32, (tq, LANES), 1)
    key = lax.broadcasted_iota(i32, (tq, 2 * tq), 0)
    qry = lax.broadcasted_iota(i32, (tq, 2 * tq), 1)
    visible = ((qry & (tq - 1)) >> 6) >= (key >> 6)
    gain = g_ref[...] * (1.0 - LAM_INIT)
    km = km_ref[...]
    vm = vm_ref[...]
    n_q = q_ref.shape[0] // tq

    def scores(qi):
        q = q_ref[qi * tq:(qi + 1) * tq, :]
        zero = jnp.zeros_like(q)
        qs = jnp.concatenate([jnp.where(lane < A_HEAD_DIM, q, zero),
                              jnp.where(lane >= A_HEAD_DIM, q, zero)], axis=0)
        n = (qi + 1) * tq
        s = lax.dot_general(k_ref[0:n, :], qs, _NT, preferred_element_type=f32)
        diag = jnp.where(visible, s[n - tq:, :], -jnp.inf)
        s = diag if qi == 0 else jnp.concatenate([s[:n - tq, :], diag], axis=0)
        sm = lax.dot_general(km, qs, _NT, preferred_element_type=f32)
        return s, sm

    ahead = [scores(i) for i in range(min(SCORES_AHEAD, n_q))]
    for qi in range(n_q):
        s, sm = ahead.pop(0)
        if qi + SCORES_AHEAD < n_q:
            ahead.append(scores(qi + SCORES_AHEAD))
        n = (qi + 1) * tq
        m = jnp.maximum(jnp.max(s, axis=0, keepdims=True), jnp.max(sm, axis=0, keepdims=True))
        p = jnp.exp2(s - m)
        pm = jnp.exp2(sm - m)
        l = jnp.sum(p, axis=0, keepdims=True) + jnp.sum(pm, axis=0, keepdims=True)
        acc = lax.dot_general(v_ref[0:n, :], p.astype(bf16), _TN, preferred_element_type=f32)
        acc = acc + lax.dot_general(vm, pm.astype(bf16), _TN, preferred_element_type=f32)
        inv = 1.0 / l
        o = acc[:, :tq] * inv[:, :tq] - lam_ref[0] * (acc[:, tq:] * inv[:, tq:])
        ms = jnp.mean(o * o, axis=0, keepdims=True)
        o = o * lax.rsqrt(ms + RMS_EPS) * gain
        o_ref[qi * tq:(qi + 1) * tq, :] = o.T.astype(bf16)


def _attention(lam, qkv, qkv_meta, subln_g, batch, seq):
    return pl.pallas_call(
        _attn_kernel,
        grid_spec=pltpu.PrefetchScalarGridSpec(
            num_scalar_prefetch=1,
            grid=(batch, A_HEADS),
            in_specs=[
                pl.BlockSpec((seq, LANES), lambda b, h, lam: (b, h)),
                pl.BlockSpec((seq, LANES), lambda b, h, lam: (b, A_HEADS + h)),
                pl.BlockSpec((seq, LANES), lambda b, h, lam: (b, 2 * A_HEADS + h)),
                pl.BlockSpec((N_META, LANES), lambda b, h, lam: (0, A_HEADS + h)),
                pl.BlockSpec((N_META, LANES), lambda b, h, lam: (0, 2 * A_HEADS + h)),
                pl.BlockSpec((LANES, 1), lambda b, h, lam: (0, 0)),
            ],
            out_specs=pl.BlockSpec((seq, LANES), lambda b, h, lam: (b, h)),
        ),
        out_shape=jax.ShapeDtypeStruct((batch * seq, A_W), bf16),
        compiler_params=pltpu.CompilerParams(
            dimension_semantics=("arbitrary", "arbitrary"), vmem_limit_bytes=VMEM_LIMIT),
        name="attn",
    )(lam, qkv, qkv, qkv, qkv_meta, qkv_meta, subln_g)


def _split3(x):
    a = x.astype(bf16)
    r = x - a.astype(f32)
    b = r.astype(bf16)
    c = (r - b.astype(f32)).astype(bf16)
    return a, b, c


def _chunk_cumsum(tri, x):
    a, b, c = _split3(x)
    out = jnp.dot(tri, a, preferred_element_type=f32)
    out = out + jnp.dot(tri, b, preferred_element_type=f32)
    return out + jnp.dot(tri, c, preferred_element_type=f32)


def _gates(z, lb):
    sig = jax.nn.sigmoid(z)
    log_f = jnp.log(lb + (1.0 - lb) * sig)
    key = (1.0 - lb) * (1.0 - sig)
    return log_f, key


def _hgrn_kernel(q_ref, f_ref, i_ref, g_ref, fm_ref, im_ref, lb_ref, ng_ref, o_ref,
                 oin_scr, qhat_scr, ut_scr, dec_scr):
    lb = lb_ref[...]
    n_tiles = q_ref.shape[0] // HG_TILE
    per_tile = HG_TILE // CHUNK

    lfm, km = _gates(fm_ref[...], lb)
    r16 = lax.broadcasted_iota(i32, (N_META, N_META), 0)
    c16 = lax.broadcasted_iota(i32, (N_META, N_META), 1)
    bm = _chunk_cumsum((c16 <= r16).astype(bf16), lfm)
    kdm = km * jnp.exp(bm[N_META - 1:N_META, :] - bm)
    st = lax.dot_general(im_ref[...].astype(bf16), kdm.astype(bf16), _TN, preferred_element_type=f32)

    row = lax.broadcasted_iota(i32, (HG_TILE, HG_TILE), 0)
    col = lax.broadcasted_iota(i32, (HG_TILE, HG_TILE), 1)
    causal = ((row >> 6) == (col >> 6)) & (col <= row)
    tri = causal.astype(bf16)

    def decays(t):
        rows = slice(t * HG_TILE, (t + 1) * HG_TILE)
        log_f, kh = _gates(f_ref[rows, :], lb)
        return kh, _chunk_cumsum(tri, log_f)

    ahead = decays(0)
    for t in range(n_tiles):
        rows = slice(t * HG_TILE, (t + 1) * HG_TILE)
        kh, b = ahead
        if t + 1 < n_tiles:
            ahead = decays(t + 1)
        qv = q_ref[rows, :]
        qh = qv * jax.nn.sigmoid(qv) * (B_KEY_DIM ** -0.5)
        b3 = b.reshape(per_tile, CHUNK, LANES)
        b_mid = jnp.broadcast_to(b3[:, CHUNK // 2:CHUNK // 2 + 1, :], b3.shape).reshape(HG_TILE, LANES)
        b_last3 = b3[:, CHUNK - 1:CHUNK, :]
        b_last = jnp.broadcast_to(b_last3, b3.shape).reshape(HG_TILE, LANES)
        qt = (qh * jnp.exp(b - b_mid)).astype(bf16)
        kt = (kh * jnp.exp(jnp.minimum(b_mid - b, 80.0))).astype(bf16)
        a = lax.dot_general(qt, kt, _NT, preferred_element_type=f32)
        a = jnp.where(causal, a, 0.0).astype(bf16)
        vv = i_ref[rows, :].astype(bf16)
        oin_scr[rows, :] = jnp.dot(a, vv, preferred_element_type=f32)
        qhat_scr[rows, :] = (qh * jnp.exp(b)).astype(bf16)
        kd = (kh * jnp.exp(b_last - b)).astype(bf16)
        for c in range(per_tile):
            cr = slice(c * CHUNK, (c + 1) * CHUNK)
            n = t * per_tile + c
            ut_scr[n] = lax.dot_general(vv[cr], kd[cr], _TN, preferred_element_type=f32)
            dec_scr[n] = jnp.exp(b_last3[c])

    ng = ng_ref[...]
    for n in range(n_tiles * per_tile):
        rows = slice(n * CHUNK, (n + 1) * CHUNK)
        o = oin_scr[rows, :] + lax.dot_general(qhat_scr[rows, :], st.astype(bf16), _NT,
                                               preferred_element_type=f32)
        ms = jnp.mean(o * o, axis=-1, keepdims=True)
        gv = g_ref[rows, :]
        o_ref[rows, :] = (o * lax.rsqrt(ms + RMS_EPS) * ng * (gv * jax.nn.sigmoid(gv))).astype(bf16)
        st = st * dec_scr[n] + ut_scr[n]


def _hgrn(hg, hg_meta, lb, norm_g, batch, seq):
    n_chunks = seq // CHUNK

    def col(c):
        return pl.BlockSpec((seq, LANES), lambda b, h: (b, c * B_HEADS + h))

    def mcol(c):
        return pl.BlockSpec((N_META, LANES), lambda b, h: (0, c * B_HEADS + h))

    return pl.pallas_call(
        _hgrn_kernel,
        grid=(batch, B_HEADS),
        in_specs=[
            col(0), col(1), col(2), col(3), mcol(1), mcol(2),
            pl.BlockSpec((None, 1, LANES), lambda b, h: (h, 0, 0)),
            pl.BlockSpec((1, LANES), lambda b, h: (0, 0)),
        ],
        out_specs=pl.BlockSpec((seq, LANES), lambda b, h: (b, h)),
        out_shape=jax.ShapeDtypeStruct((batch * seq, B_W), bf16),
        scratch_shapes=[
            pltpu.VMEM((seq, LANES), f32),
            pltpu.VMEM((seq, LANES), bf16),
            pltpu.VMEM((n_chunks, LANES, LANES), f32),
            pltpu.VMEM((n_chunks, 1, LANES), f32),
        ],
        compiler_params=pltpu.CompilerParams(
            dimension_semantics=("arbitrary", "arbitrary"), vmem_limit_bytes=VMEM_LIMIT),
        name="hgrn",
    )(hg, hg, hg, hg, hg_meta, hg_meta, lb, norm_g)


def _mix_route_kernel(x_ref, g0_ref, b0_ref, ya_ref, yb_ref, wo_ref, g1_ref, b1_ref,
                      wr_ref, br_ref,
                      h_ref, hp_ref, idx_ref, gate_ref, rank_ref, cnt_ref, carry_scr):
    step = pl.program_id(0)

    @pl.when(step == 0)
    def _():
        carry_scr[...] = jnp.zeros_like(carry_scr)

    tm = ROW_TILE
    subs = [slice(t * tm, (t + 1) * tm) for t in range(x_ref.shape[0] // tm)]
    h0 = [_layer_norm(x_ref[r, :], g0_ref[...], b0_ref[...]) for r in subs]
    mix = [jnp.dot(ya_ref[r, :], wo_ref[:A_W, :], preferred_element_type=f32)
           + jnp.dot(yb_ref[r, :], wo_ref[A_W:, :], preferred_element_type=f32) for r in subs]
    h1 = [_layer_norm(DEEPNORM_ALPHA * a + m, g1_ref[...], b1_ref[...]) for a, m in zip(h0, mix)]
    for r, h in zip(subs, h1):
        h_ref[r, :] = h
        hp_ref[r, :] = _pack_rows(h)

    def split_logits(h):
        h_hi = h.astype(bf16)
        h_lo = (h - h_hi.astype(f32)).astype(bf16)
        return (jnp.dot(h_hi, wr_ref[...], preferred_element_type=f32)
                + jnp.dot(h_lo, wr_ref[...], preferred_element_type=f32))

    parts = [split_logits(h) for h in h1]
    eid = lax.broadcasted_iota(i32, (N_EXPERTS, tm), 0)
    r_io = lax.broadcasted_iota(i32, (tm, tm), 0)
    c_io = lax.broadcasted_iota(i32, (tm, tm), 1)
    upper = (r_io <= c_io).astype(bf16)
    carry = carry_scr[...]
    for r, part in zip(subs, parts):
        part = part.T
        work = part[:N_EXPERTS, :] + part[N_EXPERTS:2 * N_EXPERTS, :] + br_ref[...]
        vals, hots = [], []
        sel = jnp.zeros((N_EXPERTS, tm), f32)
        for k in range(TOP_K):
            m = jnp.max(work, axis=0, keepdims=True)
            first = jnp.min(jnp.where(work == m, eid, N_EXPERTS), axis=0, keepdims=True)
            hot = eid == first
            vals.append(m)
            hots.append(hot)
            idx_ref[k:k + 1, r] = first
            sel = sel + hot.astype(f32)
            work = jnp.where(hot, -jnp.inf, work)

        es = [jnp.exp(v - vals[0]) for v in vals]
        denom = es[0] + es[1] + es[2] + es[3]
        for k in range(TOP_K):
            gate_ref[k:k + 1, r] = es[k] / denom
        gate_ref[TOP_K:, r] = jnp.zeros((GATE_ROWS - TOP_K, tm), f32)

        incl = jnp.dot(sel.astype(bf16), upper, preferred_element_type=f32)
        excl = incl - sel + carry
        for k in range(TOP_K):
            rank_ref[k:k + 1, r] = jnp.sum(jnp.where(hots[k], excl, 0.0), axis=0, keepdims=True).astype(i32)
        carry = carry + incl[:, tm - 1:tm]
    carry_scr[...] = carry
    cnt_ref[...] = carry.astype(i32)


def _mix_route(x2, g0, b0, ya, yb, wo, g1, b1, wr, br):
    n = x2.shape[0]
    tm = ROUTE_TILE

    def full(shape):
        return pl.BlockSpec(shape, lambda i: (0,) * len(shape))

    vec = full((1, D_MODEL))
    small = pl.BlockSpec((TOP_K, tm), lambda i: (0, i))
    return pl.pallas_call(
        _mix_route_kernel,
        grid=(n // tm,),
        in_specs=[
            pl.BlockSpec((tm, D_MODEL), lambda i: (i, 0)), vec, vec,
            pl.BlockSpec((tm, A_W), lambda i: (i, 0)),
            pl.BlockSpec((tm, B_W), lambda i: (i, 0)),
            full((D_MODEL, D_MODEL)), vec, vec,
            full((D_MODEL, LANES)), full((N_EXPERTS, 1)),
        ],
        out_specs=[
            pl.BlockSpec((tm, D_MODEL), lambda i: (i, 0)),
            pl.BlockSpec((tm, HALF), lambda i: (i, 0)),
            small, pl.BlockSpec((GATE_ROWS, tm), lambda i: (0, i)), small,
            full((N_EXPERTS, 1)),
        ],
        out_shape=[
            jax.ShapeDtypeStruct((n, D_MODEL), f32),
            jax.ShapeDtypeStruct((n, HALF), u32),
            jax.ShapeDtypeStruct((TOP_K, n), i32),
            jax.ShapeDtypeStruct((GATE_ROWS, n), f32),
            jax.ShapeDtypeStruct((TOP_K, n), i32),
            jax.ShapeDtypeStruct((N_EXPERTS, 1), i32),
        ],
        scratch_shapes=[pltpu.VMEM((N_EXPERTS, 1), f32)],
        compiler_params=pltpu.CompilerParams(
            dimension_semantics=("arbitrary",), vmem_limit_bytes=VMEM_LIMIT),
        name="mix_route",
    )(x2, g0, b0, ya, yb, wo, g1, b1, wr, br)


def _dest_kernel(idx_ref, rank_ref, start_ref, dest_ref):
    eid = lax.broadcasted_iota(i32, (N_EXPERTS, idx_ref.shape[1]), 0)
    start = start_ref[...]
    for k in range(TOP_K):
        base = jnp.sum(jnp.where(eid == idx_ref[k:k + 1, :], start, 0), axis=0, keepdims=True)
        dest_ref[k:k + 1, :] = base + rank_ref[k:k + 1, :]


def _dest_rows(top_idx, rank, start_pad):
    n = top_idx.shape[1]
    tile = 2048
    spec = pl.BlockSpec((TOP_K, tile), lambda i: (0, i))
    return pl.pallas_call(
        _dest_kernel,
        grid=(n // tile,),
        in_specs=[spec, spec, pl.BlockSpec((N_EXPERTS, 1), lambda i: (0, 0))],
        out_specs=spec,
        out_shape=jax.ShapeDtypeStruct((TOP_K, n), i32),
        compiler_params=pltpu.CompilerParams(dimension_semantics=("arbitrary",)),
        name="dest_rows",
    )(top_idx, rank, start_pad.reshape(N_EXPERTS, 1).astype(i32))


def _sc_mesh():
    return plsc.VectorSubcoreMesh(core_axis_name="c", subcore_axis_name="s",
                                  num_cores=SC_CORES, num_subcores=SC_SUBCORES)


def _sc_worker():
    return lax.axis_index("s") * SC_CORES + lax.axis_index("c")


def _dispatch(hp, idx_chunks, n_rows):
    per_worker = hp.shape[0] // SC_CHUNK // (SC_CORES * SC_SUBCORES)

    @functools.partial(
        pl.kernel, mesh=_sc_mesh(),
        out_type=jax.ShapeDtypeStruct((n_rows, HALF), u32),
        scratch_types=[pltpu.VMEM((TOP_K, SC_CHUNK), i32), pltpu.VMEM((SC_CHUNK, HALF), u32),
                       pltpu.SemaphoreType.DMA],
        name="dispatch_sc",
    )
    def scatter_rows(hp_hbm, idx_hbm, rows_hbm, idx_v, rows_v, sem):
        first = _sc_worker() * per_worker

        @pl.loop(0, per_worker)
        def _(i):
            c = first + i
            pltpu.sync_copy(idx_hbm.at[c], idx_v)
            pltpu.sync_copy(hp_hbm.at[pl.ds(c * SC_CHUNK, SC_CHUNK)], rows_v)
            copies = [pltpu.async_copy(rows_v, rows_hbm.at[idx_v.at[k]], sem) for k in range(TOP_K)]
            for copy in copies:
                copy.wait()

    return scatter_rows(hp, idx_chunks)


def _gather(y_rows, idx_chunks, n_tok):
    chunk = idx_chunks.shape[2]
    per_worker = n_tok // chunk // (SC_CORES * SC_SUBCORES)
    row_buf = pltpu.VMEM((chunk, HALF), u32)

    @functools.partial(
        pl.kernel, mesh=_sc_mesh(),
        out_type=jax.ShapeDtypeStruct((TOP_K, n_tok, HALF), u32),
        scratch_types=[pltpu.VMEM((TOP_K, chunk), i32), row_buf, row_buf,
                       pltpu.SemaphoreType.DMA, pltpu.SemaphoreType.DMA,
                       pltpu.SemaphoreType.DMA, pltpu.SemaphoreType.DMA],
        name="gather_sc",
    )
    def gather_rows(y_hbm, idx_hbm, out_hbm, idx_v, buf_a, buf_b, ga, gb, wa, wb):
        first = _sc_worker() * per_worker
        bufs, g_sems, w_sems = (buf_a, buf_b), (ga, gb), (wa, wb)

        @pl.loop(0, per_worker)
        def _(i):
            c = first + i
            pltpu.sync_copy(idx_hbm.at[c], idx_v)
            gathers = [None] * TOP_K
            writes = [None] * TOP_K
            gathers[0] = pltpu.async_copy(y_hbm.at[idx_v.at[0]], bufs[0], g_sems[0])
            for k in range(TOP_K):
                b = k % 2
                gathers[k].wait()
                writes[k] = pltpu.async_copy(bufs[b], out_hbm.at[k, pl.ds(c * chunk, chunk)], w_sems[b])
                if k >= 1:
                    writes[k - 1].wait()
                if k + 1 < TOP_K:
                    gathers[k + 1] = pltpu.async_copy(y_hbm.at[idx_v.at[k + 1]], bufs[1 - b], g_sems[1 - b])
            writes[TOP_K - 1].wait()

    return gather_rows(y_rows, idx_chunks)


def _pair_gate_up(a0, a1, even):
    gate = jnp.where(even, a0, pltpu.roll(a1, 1, axis=1))
    up = jnp.where(even, pltpu.roll(a0, LANES - 1, axis=1), a1)
    return gate, up


def _experts_kernel(be_ref, na_ref, nxt_ref, x_ref, wgu_hbm, bgu_ref, wdn_hbm, bdn_ref, y_ref,
                    gu_stage, dn_stage, wgu_scr, wd_scr, riffle_scr, act_scr, gu_sem, dn_sem):
    j = pl.program_id(0)
    expert = be_ref[j]
    prev = be_ref[jnp.maximum(j - 1, 0)]
    fresh = jnp.logical_or(j == 0, expert != prev)

    def weight_copies(e):
        return (pltpu.make_async_copy(wgu_hbm.at[e], gu_stage, gu_sem),
                pltpu.make_async_copy(wdn_hbm.at[e], dn_stage, dn_sem))

    @pl.when(j == 0)
    def _():
        for c in weight_copies(expert):
            c.start()

    @pl.when(jnp.logical_and(fresh, j < na_ref[0]))
    def _():
        for c in weight_copies(expert):
            c.wait()
        for t in range(2 * D_EXPERT // GU_CHUNK):
            cols = slice(t * GU_CHUNK, (t + 1) * GU_CHUNK)
            wgu_scr[:, cols] = gu_stage[:, cols].astype(bf16)
        half = LANES // 2
        for g in range(D_EXPERT // LANES):
            rows = slice(g * LANES, (g + 1) * LANES)
            for s in range(D_MODEL // LANES):
                lanes = slice(s * LANES, (s + 1) * LANES)
                riffle_scr[s, pl.ds(0, half, stride=2), :] = dn_stage[g * LANES:g * LANES + half, lanes]
                riffle_scr[s, pl.ds(1, half, stride=2), :] = dn_stage[g * LANES + half:(g + 1) * LANES, lanes]
                wd_scr[rows, lanes] = riffle_scr[s].astype(bf16)

        @pl.when(nxt_ref[j] >= 0)
        def _():
            for c in weight_copies(nxt_ref[j]):
                c.start(priority=1)

    @pl.when(j < na_ref[0])
    def _():
        left, right = _unpack_rows(x_ref[...])
        xl = left.astype(bf16)
        xr = right.astype(bf16)
        even = (lax.broadcasted_iota(i32, (EXPERT_BLOCK, LANES), 1) & 1) == 0
        for c in range(2 * D_EXPERT // GU_CHUNK):
            cols = slice(c * GU_CHUNK, (c + 1) * GU_CHUNK)
            gu = jnp.dot(xl, wgu_scr[:HALF, cols], preferred_element_type=f32)
            gu = gu + jnp.dot(xr, wgu_scr[HALF:, cols], preferred_element_type=f32) + bgu_ref[:, cols]
            for h in range(GU_CHUNK // (2 * LANES)):
                a0 = gu[:, 2 * h * LANES:(2 * h + 1) * LANES]
                a1 = gu[:, (2 * h + 1) * LANES:(2 * h + 2) * LANES]
                out = slice(c * GU_CHUNK // 2 + h * LANES, c * GU_CHUNK // 2 + (h + 1) * LANES)
                gate, up = _pair_gate_up(a0, a1, even)
                gate = jnp.minimum(gate, SWIGLU_LIMIT)
                up = jnp.clip(up, -SWIGLU_LIMIT, SWIGLU_LIMIT)
                act_scr[:, out] = ((up + 1.0) * gate * jax.nn.sigmoid(gate * SWIGLU_ALPHA)).astype(bf16)
        y = jnp.dot(act_scr[...], wd_scr[...], preferred_element_type=f32) + bdn_ref[...]
        y_ref[...] = _pack_rows(y)

    @pl.when(j >= na_ref[0])
    def _():
        y_ref[...] = jnp.zeros_like(y_ref)


def _experts(block_expert, n_active, next_expert, rows, w_gu, b_gu, w_dn, b_dn):
    n_blocks = rows.shape[0] // EXPERT_BLOCK

    def blk(j, be, na, nxt):
        return (jnp.minimum(j, na[0] - 1), 0)

    def per_expert(shape):
        return pl.BlockSpec((None,) + shape, lambda j, be, na, nxt: (be[j], 0, 0))

    return pl.pallas_call(
        _experts_kernel,
        grid_spec=pltpu.PrefetchScalarGridSpec(
            num_scalar_prefetch=3,
            grid=(n_blocks,),
            in_specs=[
                pl.BlockSpec((EXPERT_BLOCK, HALF), blk),
                pl.BlockSpec(memory_space=pl.ANY),
                per_expert((1, 2 * D_EXPERT)),
                pl.BlockSpec(memory_space=pl.ANY),
                per_expert((1, D_MODEL)),
            ],
            out_specs=pl.BlockSpec((EXPERT_BLOCK, HALF), lambda j, be, na, nxt: (j, 0)),
            scratch_shapes=[
                pltpu.VMEM((D_MODEL, 2 * D_EXPERT), f32),
                pltpu.VMEM((D_EXPERT, D_MODEL), f32),
                pltpu.VMEM((D_MODEL, 2 * D_EXPERT), bf16),
                pltpu.VMEM((D_EXPERT, D_MODEL), bf16),
                pltpu.VMEM((D_MODEL // LANES, LANES, LANES), f32),
                pltpu.VMEM((EXPERT_BLOCK, D_EXPERT), bf16),
                pltpu.SemaphoreType.DMA(()),
                pltpu.SemaphoreType.DMA(()),
            ],
        ),
        out_shape=jax.ShapeDtypeStruct(rows.shape, u32),
        compiler_params=pltpu.CompilerParams(
            dimension_semantics=("arbitrary",), vmem_limit_bytes=VMEM_LIMIT),
        name="experts",
    )(block_expert, n_active, next_expert, rows, w_gu, b_gu, w_dn, b_dn)


def _combine_kernel(h_ref, gate_ref, y_ref, g2_ref, b2_ref, *rest):
    o_ref = rest[-1]
    pad = jnp.zeros((LANES - GATE_ROWS, ROW_TILE), f32)
    gates = jnp.concatenate([gate_ref[...], pad], axis=0).T
    left = jnp.zeros((ROW_TILE, HALF), f32)
    right = jnp.zeros((ROW_TILE, HALF), f32)
    for k in range(TOP_K):
        yl, yr = _unpack_rows(y_ref[k])
        gk = gates[:, k:k + 1]
        left = left + yl * gk
        right = right + yr * gk
    ffn = jnp.concatenate([left, right], axis=1)
    o_ref[...] = _layer_norm(DEEPNORM_ALPHA * h_ref[...] + ffn, g2_ref[...], b2_ref[...])


def _combine(h1, gates, y_part, g2, b2, part, prev_out):
    n = h1.shape[0]
    tiles = y_part.shape[1] // ROW_TILE
    first = part * tiles
    vec = pl.BlockSpec((1, D_MODEL), lambda i: (0, 0))
    chained = [] if prev_out is None else [prev_out]
    return pl.pallas_call(
        _combine_kernel,
        grid=(tiles,),
        in_specs=[
            pl.BlockSpec((ROW_TILE, D_MODEL), lambda i: (first + i, 0)),
            pl.BlockSpec((GATE_ROWS, ROW_TILE), lambda i: (0, first + i)),
            pl.BlockSpec((TOP_K, ROW_TILE, HALF), lambda i: (0, i, 0)),
            vec, vec,
        ] + [pl.BlockSpec(memory_space=pl.ANY)] * len(chained),
        out_specs=pl.BlockSpec((ROW_TILE, D_MODEL), lambda i: (first + i, 0)),
        out_shape=jax.ShapeDtypeStruct((n, D_MODEL), f32),
        input_output_aliases={5: 0} if chained else {},
        compiler_params=pltpu.CompilerParams(
            dimension_semantics=("arbitrary",), vmem_limit_bytes=VMEM_LIMIT),
        name="combine",
    )(h1, gates, y_part, g2, b2, *chained)


def _rope_tables(pos, scale):
    inv = (1.0 / (ROPE_THETA ** (np.arange(0, A_HEAD_DIM, 2, dtype=np.float32) / A_HEAD_DIM))).astype(np.float32)
    ang = pos.astype(np.float32)[:, None] * inv[None, :]
    ang = np.concatenate([ang, ang, ang, ang], axis=-1)
    sign = np.where((np.arange(LANES) & 63) < 32, -1.0, 1.0).astype(np.float32)
    scale = np.float32(scale)
    return jnp.asarray(np.cos(ang) * scale), jnp.asarray(np.sin(ang) * sign * scale)


def kernel(x, meta_tokens, ln_emb_g, ln_emb_b, w_in, lambda_q1, lambda_k1, lambda_q2, lambda_k2,
           subln_g, hgrn_lb_table, hgrn_norm_g, w_out, ln1_g, ln1_b, w_router, b_router,
           w_gate_up, b_gate_up, w_down, b_down, ln2_g, ln2_b):
    batch, seq, _ = x.shape
    n_tok = batch * seq
    x2 = x.reshape(n_tok, D_MODEL)
    row = lambda v: v.reshape(1, -1).astype(f32)

    w_in_bf = w_in[0].astype(bf16)
    lam = (jnp.exp(jnp.sum(lambda_q1[0].astype(f32) * lambda_k1[0].astype(f32)))
           - jnp.exp(jnp.sum(lambda_q2[0].astype(f32) * lambda_k2[0].astype(f32))) + LAM_INIT).reshape(1)
    lb = jnp.cumsum(jax.nn.softmax(hgrn_lb_table.astype(f32), axis=0), axis=0)[0].reshape(B_HEADS, 1, LANES)
    q_scale = A_HEAD_DIM ** -0.5 * math.log2(math.e)
    pos_x = N_META + np.arange(seq)
    pos_m = np.arange(N_META)
    cq, sq = _rope_tables(pos_x, q_scale)
    ck, sk = _rope_tables(pos_x, 1.0)
    cqm, sqm = _rope_tables(pos_m, q_scale)
    ckm, skm = _rope_tables(pos_m, 1.0)
    wr = w_router[0].astype(f32)
    wr_hi = wr.astype(bf16)
    wr_lo = (wr - wr_hi.astype(f32)).astype(bf16)
    wr_split = jnp.pad(jnp.concatenate([wr_hi, wr_lo], axis=1), ((0, 0), (0, LANES - 2 * N_EXPERTS)))
    bgu = b_gate_up[0].reshape(N_EXPERTS, 1, 2 * D_EXPERT)
    bdn = b_down[0].reshape(N_EXPERTS, 1, D_MODEL)

    g0, b0 = row(ln_emb_g), row(ln_emb_b)
    qkv, hg = _in_proj(x2, g0, b0, w_in_bf, (cq, sq, ck, sk), PROJ_TILE, seq // PROJ_TILE)
    qkv_m, hg_m = _in_proj(meta_tokens, g0, b0, w_in_bf, (cqm, sqm, ckm, skm), N_META, 1)

    ya = _attention(lam, qkv, qkv_m, subln_g[0].reshape(-1, 1).astype(f32), batch, seq)
    yb = _hgrn(hg, hg_m, lb, row(hgrn_norm_g[0]), batch, seq)

    h1, hp, top_idx, gates, rank, counts = _mix_route(
        x2, g0, b0, ya, yb, w_out[0].astype(bf16), row(ln1_g[0]), row(ln1_b[0]),
        wr_split, b_router[0].reshape(-1, 1).astype(f32))

    counts = counts.reshape(N_EXPERTS)
    blocks = (counts + EXPERT_BLOCK - 1) // EXPERT_BLOCK
    cum_blocks = jnp.cumsum(blocks)
    start_pad = (cum_blocks - blocks) * EXPERT_BLOCK
    n_blocks = n_tok * TOP_K // EXPERT_BLOCK + N_EXPERTS
    n_active = cum_blocks[-1:].astype(i32)
    jb = jnp.arange(n_blocks, dtype=i32)
    block_expert = jnp.sum((cum_blocks[None, :] <= jnp.minimum(jb, n_active - 1)[:, None]).astype(i32), axis=1)
    block_expert = jnp.minimum(block_expert, N_EXPERTS - 1).astype(i32)
    eid = jnp.arange(N_EXPERTS, dtype=i32)
    later_active = jnp.logical_and(blocks[None, :] > 0, eid[None, :] > eid[:, None])
    next_active = jnp.min(jnp.where(later_active, eid[None, :], N_EXPERTS), axis=1)
    next_active = jnp.where(next_active == N_EXPERTS, -1, next_active).astype(i32)
    next_expert = jnp.sum(jnp.where(block_expert[:, None] == eid[None, :], next_active[None, :], 0), axis=1)
    dest = _dest_rows(top_idx, rank, start_pad)
    chunked = lambda c: dest.reshape(TOP_K, n_tok // c, c).transpose(1, 0, 2)
    idx_chunks = chunked(SC_CHUNK)

    x_rows = _dispatch(hp, idx_chunks, n_blocks * EXPERT_BLOCK)
    y_rows = _experts(block_expert, n_active, next_expert, x_rows, w_gate_up[0], bgu, w_down[0], bdn)
    idx_gather = chunked(SC_GATHER_CHUNK)
    part_tok = n_tok // COMBINE_PARTS
    part_chunks = part_tok // SC_GATHER_CHUNK
    out = None
    for part in range(COMBINE_PARTS):
        y_part = _gather(y_rows, idx_gather[part * part_chunks:(part + 1) * part_chunks], part_tok)
        out = _combine(h1, gates, y_part, row(ln2_g[0]), row(ln2_b[0]), part, out)
    return out.reshape(batch, seq, D_MODEL)
```

```python
import functools
import math

import jax
import jax.numpy as jnp
import numpy as np
from jax import lax
from jax.experimental import pallas as pl
from jax.experimental.pallas import tpu as pltpu
from jax.experimental.pallas import tpu_sc as plsc

D_MODEL = 1024
N_META = 16
CHUNK = 64
A_HEADS = 4
A_HEAD_DIM = 64
B_HEADS = 4
B_KEY_DIM = 128
ROPE_THETA = 10000.0
A_W = 512
B_W = 512
D_IN = 3 * A_W + 4 * B_W
N_EXPERTS = 32
TOP_K = 4
D_EXPERT = 1024
SWIGLU_ALPHA = 1.702
SWIGLU_LIMIT = 7.0
EXPERT_BLOCK = 512
DEEPNORM_ALPHA = 2.0 ** 0.25
LN_EPS = 1e-5
RMS_EPS = 1e-5
LAM_INIT = 0.8 - 0.6 * math.exp(0.0)

LANES = 128
HALF = D_MODEL // 2
ROW_TILE = 256
ROUTE_TILE = 2 * ROW_TILE
PROJ_TILE = 512
ATT_TILE = 256
HG_TILE = 256
GU_CHUNK = 512
GATE_ROWS = 8
SCORES_AHEAD = 2
SC_CORES = 2
SC_SUBCORES = 16
SC_CHUNK = 64
COMBINE_PARTS = 4
VMEM_LIMIT = 56 * 1024 * 1024

_NT = (((1,), (1,)), ((), ()))
_TN = (((0,), (0,)), ((), ()))

f32 = jnp.float32
bf16 = jnp.bfloat16
u32 = jnp.uint32
i32 = jnp.int32


def _layer_norm(x, g, b):
    mu = jnp.mean(x, axis=-1, keepdims=True)
    xc = x - mu
    var = jnp.mean(xc * xc, axis=-1, keepdims=True)
    return xc * lax.rsqrt(var + LN_EPS) * g + b


def _pack_rows(h):
    hb = h.astype(bf16).astype(f32)
    lo = lax.bitcast_convert_type(hb[:, :HALF], u32) >> 16
    hi = lax.bitcast_convert_type(hb[:, HALF:], u32)
    return hi | lo


def _unpack_rows(w):
    left = lax.bitcast_convert_type(w << 16, f32)
    right = lax.bitcast_convert_type(w & jnp.uint32(0xFFFF0000), f32)
    return left, right


def _rope(x, cos, sin_signed, first_half):
    fwd = pltpu.roll(x, 32, axis=1)
    bwd = pltpu.roll(x, x.shape[1] - 32, axis=1)
    return x * cos + jnp.where(first_half, bwd, fwd) * sin_signed


def _in_proj_kernel(x_ref, g_ref, b_ref, w_ref, cq_ref, sq_ref, ck_ref, sk_ref, qkv_ref, hg_ref):
    h = _layer_norm(x_ref[...], g_ref[...], b_ref[...]).astype(bf16)
    lane = lax.broadcasted_iota(i32, (h.shape[0], A_W), 1)
    first_half = (lane & 63) < 32
    heads = lambda t: jnp.concatenate([t[...]] * A_HEADS, axis=1)
    for c, (cos_ref, sin_ref) in enumerate(((cq_ref, sq_ref), (ck_ref, sk_ref))):
        acc = jnp.dot(h, w_ref[:, c * A_W:(c + 1) * A_W], preferred_element_type=f32)
        r = _rope(acc, heads(cos_ref), heads(sin_ref), first_half)
        qkv_ref[:, c * A_W:(c + 1) * A_W] = r.astype(bf16)
    qkv_ref[:, 2 * A_W:3 * A_W] = jnp.dot(
        h, w_ref[:, 2 * A_W:3 * A_W], preferred_element_type=f32).astype(bf16)
    for c in range(4):
        lo = 3 * A_W + c * B_W
        hg_ref[:, c * B_W:(c + 1) * B_W] = jnp.dot(h, w_ref[:, lo:lo + B_W], preferred_element_type=f32)


def _in_proj(x2, g, b, w_bf, tabs, tile, tab_blocks):
    n = x2.shape[0]
    tab_spec = pl.BlockSpec((tile, LANES), lambda i: (i % tab_blocks, 0))
    vec = pl.BlockSpec((1, D_MODEL), lambda i: (0, 0))
    return pl.pallas_call(
        _in_proj_kernel,
        grid=(n // tile,),
        in_specs=[
            pl.BlockSpec((tile, D_MODEL), lambda i: (i, 0)),
            vec, vec,
            pl.BlockSpec((D_MODEL, D_IN), lambda i: (0, 0)),
            tab_spec, tab_spec, tab_spec, tab_spec,
        ],
        out_specs=[
            pl.BlockSpec((tile, 3 * A_W), lambda i: (i, 0)),
            pl.BlockSpec((tile, 4 * B_W), lambda i: (i, 0)),
        ],
        out_shape=[
            jax.ShapeDtypeStruct((n, 3 * A_W), bf16),
            jax.ShapeDtypeStruct((n, 4 * B_W), f32),
        ],
        compiler_params=pltpu.CompilerParams(
            dimension_semantics=("arbitrary",), vmem_limit_bytes=VMEM_LIMIT),
        name="in_proj",
    )(x2, g, b, w_bf, *tabs)


def _attn_kernel(lam_ref, q_ref, k_ref, v_ref, km_ref, vm_ref, g_ref, o_ref):
    tq = ATT_TILE
    lane = lax.broadcasted_iota(i32, (tq, LANES), 1)
    key = lax.broadcasted_iota(i32, (tq, 2 * tq), 0)
    qry = lax.broadcasted_iota(i32, (tq, 2 * tq), 1)
    visible = ((qry & (tq - 1)) >> 6) >= (key >> 6)
    gain = g_ref[...] * (1.0 - LAM_INIT)
    km = km_ref[...]
    vm = vm_ref[...]
    n_q = q_ref.shape[0] // tq

    def scores(qi):
        q = q_ref[qi * tq:(qi + 1) * tq, :]
        zero = jnp.zeros_like(q)
        qs = jnp.concatenate([jnp.where(lane < A_HEAD_DIM, q, zero),
                              jnp.where(lane >= A_HEAD_DIM, q, zero)], axis=0)
        n = (qi + 1) * tq
        s = lax.dot_general(k_ref[0:n, :], qs, _NT, preferred_element_type=f32)
        diag = jnp.where(visible, s[n - tq:, :], -jnp.inf)
        s = diag if qi == 0 else jnp.concatenate([s[:n - tq, :], diag], axis=0)
        sm = lax.dot_general(km, qs, _NT, preferred_element_type=f32)
        return s, sm

    ahead = [scores(i) for i in range(min(SCORES_AHEAD, n_q))]
    for qi in range(n_q):
        s, sm = ahead.pop(0)
        if qi + SCORES_AHEAD < n_q:
            ahead.append(scores(qi + SCORES_AHEAD))
        n = (qi + 1) * tq
        m = jnp.maximum(jnp.max(s, axis=0, keepdims=True), jnp.max(sm, axis=0, keepdims=True))
        p = jnp.exp2(s - m)
        pm = jnp.exp2(sm - m)
        l = jnp.sum(p, axis=0, keepdims=True) + jnp.sum(pm, axis=0, keepdims=True)
        acc = lax.dot_general(v_ref[0:n, :], p.astype(bf16), _TN, preferred_element_type=f32)
        acc = acc + lax.dot_general(vm, pm.astype(bf16), _TN, preferred_element_type=f32)
        inv = 1.0 / l
        o = acc[:, :tq] * inv[:, :tq] - lam_ref[0] * (acc[:, tq:] * inv[:, tq:])
        ms = jnp.mean(o * o, axis=0, keepdims=True)
        o = o * lax.rsqrt(ms + RMS_EPS) * gain
        o_ref[qi * tq:(qi + 1) * tq, :] = o.T.astype(bf16)


def _attention(lam, qkv, qkv_meta, subln_g, batch, seq):
    return pl.pallas_call(
        _attn_kernel,
        grid_spec=pltpu.PrefetchScalarGridSpec(
            num_scalar_prefetch=1,
            grid=(batch, A_HEADS),
            in_specs=[
                pl.BlockSpec((seq, LANES), lambda b, h, lam: (b, h)),
                pl.BlockSpec((seq, LANES), lambda b, h, lam: (b, A_HEADS + h)),
                pl.BlockSpec((seq, LANES), lambda b, h, lam: (b, 2 * A_HEADS + h)),
                pl.BlockSpec((N_META, LANES), lambda b, h, lam: (0, A_HEADS + h)),
                pl.BlockSpec((N_META, LANES), lambda b, h, lam: (0, 2 * A_HEADS + h)),
                pl.BlockSpec((LANES, 1), lambda b, h, lam: (0, 0)),
            ],
            out_specs=pl.BlockSpec((seq, LANES), lambda b, h, lam: (b, h)),
        ),
        out_shape=jax.ShapeDtypeStruct((batch * seq, A_W), bf16),
        compiler_params=pltpu.CompilerParams(
            dimension_semantics=("arbitrary", "arbitrary"), vmem_limit_bytes=VMEM_LIMIT),
        name="attn",
    )(lam, qkv, qkv, qkv, qkv_meta, qkv_meta, subln_g)


def _split3(x):
    a = x.astype(bf16)
    r = x - a.astype(f32)
    b = r.astype(bf16)
    c = (r - b.astype(f32)).astype(bf16)
    return a, b, c


def _chunk_cumsum(tri, x):
    a, b, c = _split3(x)
    out = jnp.dot(tri, a, preferred_element_type=f32)
    out = out + jnp.dot(tri, b, preferred_element_type=f32)
    return out + jnp.dot(tri, c, preferred_element_type=f32)


def _gates(z, lb):
    sig = jax.nn.sigmoid(z)
    log_f = jnp.log(lb + (1.0 - lb) * sig)
    key = (1.0 - lb) * (1.0 - sig)
    return log_f, key


def _hgrn_kernel(q_ref, f_ref, i_ref, g_ref, fm_ref, im_ref, lb_ref, ng_ref, o_ref,
                 oin_scr, qhat_scr, ut_scr, dec_scr):
    lb = lb_ref[...]
    n_tiles = q_ref.shape[0] // HG_TILE
    per_tile = HG_TILE // CHUNK

    lfm, km = _gates(fm_ref[...], lb)
    r16 = lax.broadcasted_iota(i32, (N_META, N_META), 0)
    c16 = lax.broadcasted_iota(i32, (N_META, N_META), 1)
    bm = _chunk_cumsum((c16 <= r16).astype(bf16), lfm)
    kdm = km * jnp.exp(bm[N_META - 1:N_META, :] - bm)
    st = lax.dot_general(im_ref[...].astype(bf16), kdm.astype(bf16), _TN, preferred_element_type=f32)

    row = lax.broadcasted_iota(i32, (HG_TILE, HG_TILE), 0)
    col = lax.broadcasted_iota(i32, (HG_TILE, HG_TILE), 1)
    causal = ((row >> 6) == (col >> 6)) & (col <= row)
    tri = causal.astype(bf16)

    def decays(t):
        rows = slice(t * HG_TILE, (t + 1) * HG_TILE)
        log_f, kh = _gates(f_ref[rows, :], lb)
        return kh, _chunk_cumsum(tri, log_f)

    ahead = decays(0)
    for t in range(n_tiles):
        rows = slice(t * HG_TILE, (t + 1) * HG_TILE)
        kh, b = ahead
        if t + 1 < n_tiles:
            ahead = decays(t + 1)
        qv = q_ref[rows, :]
        qh = qv * jax.nn.sigmoid(qv) * (B_KEY_DIM ** -0.5)
        b3 = b.reshape(per_tile, CHUNK, LANES)
        b_mid = jnp.broadcast_to(b3[:, CHUNK // 2:CHUNK // 2 + 1, :], b3.shape).reshape(HG_TILE, LANES)
        b_last3 = b3[:, CHUNK - 1:CHUNK, :]
        b_last = jnp.broadcast_to(b_last3, b3.shape).reshape(HG_TILE, LANES)
        qt = (qh * jnp.exp(b - b_mid)).astype(bf16)
        kt = (kh * jnp.exp(jnp.minimum(b_mid - b, 80.0))).astype(bf16)
        a = lax.dot_general(qt, kt, _NT, preferred_element_type=f32)
        a = jnp.where(causal, a, 0.0).astype(bf16)
        vv = i_ref[rows, :].astype(bf16)
        oin_scr[rows, :] = jnp.dot(a, vv, preferred_element_type=f32)
        qhat_scr[rows, :] = (qh * jnp.exp(b)).astype(bf16)
        kd = (kh * jnp.exp(b_last - b)).astype(bf16)
        for c in range(per_tile):
            cr = slice(c * CHUNK, (c + 1) * CHUNK)
            n = t * per_tile + c
            ut_scr[n] = lax.dot_general(vv[cr], kd[cr], _TN, preferred_element_type=f32)
            dec_scr[n] = jnp.exp(b_last3[c])

    ng = ng_ref[...]
    for n in range(n_tiles * per_tile):
        rows = slice(n * CHUNK, (n + 1) * CHUNK)
        o = oin_scr[rows, :] + lax.dot_general(qhat_scr[rows, :], st.astype(bf16), _NT,
                                               preferred_element_type=f32)
        ms = jnp.mean(o * o, axis=-1, keepdims=True)
        gv = g_ref[rows, :]
        o_ref[rows, :] = (o * lax.rsqrt(ms + RMS_EPS) * ng * (gv * jax.nn.sigmoid(gv))).astype(bf16)
        st = st * dec_scr[n] + ut_scr[n]


def _hgrn(hg, hg_meta, lb, norm_g, batch, seq):
    n_chunks = seq // CHUNK

    def col(c):
        return pl.BlockSpec((seq, LANES), lambda b, h: (b, c * B_HEADS + h))

    def mcol(c):
        return pl.BlockSpec((N_META, LANES), lambda b, h: (0, c * B_HEADS + h))

    return pl.pallas_call(
        _hgrn_kernel,
        grid=(batch, B_HEADS),
        in_specs=[
            col(0), col(1), col(2), col(3), mcol(1), mcol(2),
            pl.BlockSpec((None, 1, LANES), lambda b, h: (h, 0, 0)),
            pl.BlockSpec((1, LANES), lambda b, h: (0, 0)),
        ],
        out_specs=pl.BlockSpec((seq, LANES), lambda b, h: (b, h)),
        out_shape=jax.ShapeDtypeStruct((batch * seq, B_W), bf16),
        scratch_shapes=[
            pltpu.VMEM((seq, LANES), f32),
            pltpu.VMEM((seq, LANES), bf16),
            pltpu.VMEM((n_chunks, LANES, LANES), f32),
            pltpu.VMEM((n_chunks, 1, LANES), f32),
        ],
        compiler_params=pltpu.CompilerParams(
            dimension_semantics=("arbitrary", "arbitrary"), vmem_limit_bytes=VMEM_LIMIT),
        name="hgrn",
    )(hg, hg, hg, hg, hg_meta, hg_meta, lb, norm_g)


def _mix_route_kernel(x_ref, g0_ref, b0_ref, ya_ref, yb_ref, wo_ref, g1_ref, b1_ref,
                      wr_ref, br_ref,
                      h_ref, hp_ref, idx_ref, gate_ref, rank_ref, cnt_ref, carry_scr):
    step = pl.program_id(0)

    @pl.when(step == 0)
    def _():
        carry_scr[...] = jnp.zeros_like(carry_scr)

    tm = ROW_TILE
    subs = [slice(t * tm, (t + 1) * tm) for t in range(x_ref.shape[0] // tm)]
    h0 = [_layer_norm(x_ref[r, :], g0_ref[...], b0_ref[...]) for r in subs]
    mix = [jnp.dot(ya_ref[r, :], wo_ref[:A_W, :], preferred_element_type=f32)
           + jnp.dot(yb_ref[r, :], wo_ref[A_W:, :], preferred_element_type=f32) for r in subs]
    h1 = [_layer_norm(DEEPNORM_ALPHA * a + m, g1_ref[...], b1_ref[...]) for a, m in zip(h0, mix)]
    for r, h in zip(subs, h1):
        h_ref[r, :] = h
        hp_ref[r, :] = _pack_rows(h)

    def split_logits(h):
        h_hi = h.astype(bf16)
        h_lo = (h - h_hi.astype(f32)).astype(bf16)
        return (jnp.dot(h_hi, wr_ref[...], preferred_element_type=f32)
                + jnp.dot(h_lo, wr_ref[...], preferred_element_type=f32))

    parts = [split_logits(h) for h in h1]
    eid = lax.broadcasted_iota(i32, (N_EXPERTS, tm), 0)
    r_io = lax.broadcasted_iota(i32, (tm, tm), 0)
    c_io = lax.broadcasted_iota(i32, (tm, tm), 1)
    upper = (r_io <= c_io).astype(bf16)
    carry = carry_scr[...]
    for r, part in zip(subs, parts):
        part = part.T
        work = part[:N_EXPERTS, :] + part[N_EXPERTS:2 * N_EXPERTS, :] + br_ref[...]
        vals, hots = [], []
        sel = jnp.zeros((N_EXPERTS, tm), f32)
        for k in range(TOP_K):
            m = jnp.max(work, axis=0, keepdims=True)
            first = jnp.min(jnp.where(work == m, eid, N_EXPERTS), axis=0, keepdims=True)
            hot = eid == first
            vals.append(m)
            hots.append(hot)
            idx_ref[k:k + 1, r] = first
            sel = sel + hot.astype(f32)
            work = jnp.where(hot, -jnp.inf, work)

        es = [jnp.exp(v - vals[0]) for v in vals]
        denom = es[0] + es[1] + es[2] + es[3]
        for k in range(TOP_K):
            gate_ref[k:k + 1, r] = es[k] / denom
        gate_ref[TOP_K:, r] = jnp.zeros((GATE_ROWS - TOP_K, tm), f32)

        incl = jnp.dot(sel.astype(bf16), upper, preferred_element_type=f32)
        excl = incl - sel + carry
        for k in range(TOP_K):
            rank_ref[k:k + 1, r] = jnp.sum(jnp.where(hots[k], excl, 0.0), axis=0, keepdims=True).astype(i32)
        carry = carry + incl[:, tm - 1:tm]
    carry_scr[...] = carry
    cnt_ref[...] = carry.astype(i32)


def _mix_route(x2, g0, b0, ya, yb, wo, g1, b1, wr, br):
    n = x2.shape[0]
    tm = ROUTE_TILE

    def full(shape):
        return pl.BlockSpec(shape, lambda i: (0,) * len(shape))

    vec = full((1, D_MODEL))
    small = pl.BlockSpec((TOP_K, tm), lambda i: (0, i))
    return pl.pallas_call(
        _mix_route_kernel,
        grid=(n // tm,),
        in_specs=[
            pl.BlockSpec((tm, D_MODEL), lambda i: (i, 0)), vec, vec,
            pl.BlockSpec((tm, A_W), lambda i: (i, 0)),
            pl.BlockSpec((tm, B_W), lambda i: (i, 0)),
            full((D_MODEL, D_MODEL)), vec, vec,
            full((D_MODEL, LANES)), full((N_EXPERTS, 1)),
        ],
        out_specs=[
            pl.BlockSpec((tm, D_MODEL), lambda i: (i, 0)),
            pl.BlockSpec((tm, HALF), lambda i: (i, 0)),
            small, pl.BlockSpec((GATE_ROWS, tm), lambda i: (0, i)), small,
            full((N_EXPERTS, 1)),
        ],
        out_shape=[
            jax.ShapeDtypeStruct((n, D_MODEL), f32),
            jax.ShapeDtypeStruct((n, HALF), u32),
            jax.ShapeDtypeStruct((TOP_K, n), i32),
            jax.ShapeDtypeStruct((GATE_ROWS, n), f32),
            jax.ShapeDtypeStruct((TOP_K, n), i32),
            jax.ShapeDtypeStruct((N_EXPERTS, 1), i32),
        ],
        scratch_shapes=[pltpu.VMEM((N_EXPERTS, 1), f32)],
        compiler_params=pltpu.CompilerParams(
            dimension_semantics=("arbitrary",), vmem_limit_bytes=VMEM_LIMIT),
        name="mix_route",
    )(x2, g0, b0, ya, yb, wo, g1, b1, wr, br)


def _dest_kernel(idx_ref, rank_ref, start_ref, dest_ref):
    eid = lax.broadcasted_iota(i32, (N_EXPERTS, idx_ref.shape[1]), 0)
    start = start_ref[...]
    for k in range(TOP_K):
        base = jnp.sum(jnp.where(eid == idx_ref[k:k + 1, :], start, 0), axis=0, keepdims=True)
        dest_ref[k:k + 1, :] = base + rank_ref[k:k + 1, :]


def _dest_rows(top_idx, rank, start_pad):
    n = top_idx.shape[1]
    tile = 2048
    spec = pl.BlockSpec((TOP_K, tile), lambda i: (0, i))
    return pl.pallas_call(
        _dest_kernel,
        grid=(n // tile,),
        in_specs=[spec, spec, pl.BlockSpec((N_EXPERTS, 1), lambda i: (0, 0))],
        out_specs=spec,
        out_shape=jax.ShapeDtypeStruct((TOP_K, n), i32),
        compiler_params=pltpu.CompilerParams(dimension_semantics=("arbitrary",)),
        name="dest_rows",
    )(top_idx, rank, start_pad.reshape(N_EXPERTS, 1).astype(i32))


def _sc_mesh():
    return plsc.VectorSubcoreMesh(core_axis_name="c", subcore_axis_name="s",
                                  num_cores=SC_CORES, num_subcores=SC_SUBCORES)


def _sc_worker():
    return lax.axis_index("s") * SC_CORES + lax.axis_index("c")


def _dispatch(hp, idx_chunks, n_rows):
    chunk = idx_chunks.shape[2]
    per_worker = hp.shape[0] // chunk // (SC_CORES * SC_SUBCORES)
    row_buf = pltpu.VMEM((chunk, HALF), u32)
    idx_buf = pltpu.VMEM((TOP_K, chunk), i32)

    @functools.partial(
        pl.kernel, mesh=_sc_mesh(),
        out_type=jax.ShapeDtypeStruct((n_rows, HALF), u32),
        scratch_types=[idx_buf, idx_buf, row_buf, row_buf,
                       pltpu.SemaphoreType.DMA, pltpu.SemaphoreType.DMA,
                       pltpu.SemaphoreType.DMA, pltpu.SemaphoreType.DMA],
        name="dispatch_sc",
    )
    def scatter_rows(hp_hbm, idx_hbm, rows_hbm, idx_a, idx_b, buf_a, buf_b, la, lb, sa, sb):
        first = _sc_worker() * per_worker
        idxs, bufs, l_sems, s_sems = (idx_a, idx_b), (buf_a, buf_b), (la, lb), (sa, sb)

        def load(i):
            pltpu.sync_copy(idx_hbm.at[first + i], idxs[i % 2])
            return pltpu.async_copy(hp_hbm.at[pl.ds((first + i) * chunk, chunk)], bufs[i % 2], l_sems[i % 2])

        loads = [None] * per_worker
        scatters = [None] * per_worker
        loads[0] = load(0)
        for i in range(per_worker):
            b = i % 2
            loads[i].wait()
            scatters[i] = [pltpu.async_copy(bufs[b], rows_hbm.at[idxs[b].at[k]], s_sems[b]) for k in range(TOP_K)]
            if i >= 1:
                for copy in scatters[i - 1]:
                    copy.wait()
            if i + 1 < per_worker:
                loads[i + 1] = load(i + 1)
        for copy in scatters[per_worker - 1]:
            copy.wait()

    return scatter_rows(hp, idx_chunks)


def _gather(y_rows, idx_chunks, n_tok):
    chunk = idx_chunks.shape[2]
    per_worker = n_tok // chunk // (SC_CORES * SC_SUBCORES)
    row_buf = pltpu.VMEM((chunk, HALF), u32)

    @functools.partial(
        pl.kernel, mesh=_sc_mesh(),
        out_type=jax.ShapeDtypeStruct((TOP_K, n_tok, HALF), u32),
        scratch_types=[pltpu.VMEM((TOP_K, chunk), i32), row_buf, row_buf,
                       pltpu.SemaphoreType.DMA, pltpu.SemaphoreType.DMA,
                       pltpu.SemaphoreType.DMA, pltpu.SemaphoreType.DMA],
        name="gather_sc",
    )
    def gather_rows(y_hbm, idx_hbm, out_hbm, idx_v, buf_a, buf_b, ga, gb, wa, wb):
        first = _sc_worker() * per_worker
        bufs, g_sems, w_sems = (buf_a, buf_b), (ga, gb), (wa, wb)

        @pl.loop(0, per_worker)
        def _(i):
            c = first + i
            pltpu.sync_copy(idx_hbm.at[c], idx_v)
            gathers = [None] * TOP_K
            writes = [None] * TOP_K
            gathers[0] = pltpu.async_copy(y_hbm.at[idx_v.at[0]], bufs[0], g_sems[0])
            for k in range(TOP_K):
                b = k % 2
                gathers[k].wait()
                writes[k] = pltpu.async_copy(bufs[b], out_hbm.at[k, pl.ds(c * chunk, chunk)], w_sems[b])
                if k >= 1:
                    writes[k - 1].wait()
                if k + 1 < TOP_K:
                    gathers[k + 1] = pltpu.async_copy(y_hbm.at[idx_v.at[k + 1]], bufs[1 - b], g_sems[1 - b])
            writes[TOP_K - 1].wait()

    return gather_rows(y_rows, idx_chunks)


def _pair_gate_up(a0, a1, even):
    gate = jnp.where(even, a0, pltpu.roll(a1, 1, axis=1))
    up = jnp.where(even, pltpu.roll(a0, LANES - 1, axis=1), a1)
    return gate, up


def _experts_kernel(be_ref, na_ref, nxt_ref, full_ref, x_ref, wgu_hbm, bgu_ref, wdn_hbm, bdn_ref, y_ref,
                    gu_stage, dn_stage, wgu_scr, wd_scr, riffle_scr, act_scr, gu_sem, dn_sem):
    j = pl.program_id(0)
    expert = be_ref[j]
    prev = be_ref[jnp.maximum(j - 1, 0)]
    fresh = jnp.logical_or(j == 0, expert != prev)

    def weight_copies(e):
        return (pltpu.make_async_copy(wgu_hbm.at[e], gu_stage, gu_sem),
                pltpu.make_async_copy(wdn_hbm.at[e], dn_stage, dn_sem))

    @pl.when(j == 0)
    def _():
        for c in weight_copies(expert):
            c.start()

    @pl.when(jnp.logical_and(fresh, j < na_ref[0]))
    def _():
        for c in weight_copies(expert):
            c.wait()
        for t in range(2 * D_EXPERT // GU_CHUNK):
            cols = slice(t * GU_CHUNK, (t + 1) * GU_CHUNK)
            wgu_scr[:, cols] = gu_stage[:, cols].astype(bf16)
        half = LANES // 2
        for g in range(D_EXPERT // LANES):
            rows = slice(g * LANES, (g + 1) * LANES)
            for s in range(D_MODEL // LANES):
                lanes = slice(s * LANES, (s + 1) * LANES)
                riffle_scr[s, pl.ds(0, half, stride=2), :] = dn_stage[g * LANES:g * LANES + half, lanes]
                riffle_scr[s, pl.ds(1, half, stride=2), :] = dn_stage[g * LANES + half:(g + 1) * LANES, lanes]
                wd_scr[rows, lanes] = riffle_scr[s].astype(bf16)

        @pl.when(nxt_ref[j] >= 0)
        def _():
            for c in weight_copies(nxt_ref[j]):
                c.start(priority=1)

    def run(m):
        left, right = _unpack_rows(x_ref[:m, :])
        xl = left.astype(bf16)
        xr = right.astype(bf16)
        even = (lax.broadcasted_iota(i32, (m, LANES), 1) & 1) == 0
        for c in range(2 * D_EXPERT // GU_CHUNK):
            cols = slice(c * GU_CHUNK, (c + 1) * GU_CHUNK)
            gu = jnp.dot(xl, wgu_scr[:HALF, cols], preferred_element_type=f32)
            gu = gu + jnp.dot(xr, wgu_scr[HALF:, cols], preferred_element_type=f32) + bgu_ref[:, cols]
            for h in range(GU_CHUNK // (2 * LANES)):
                a0 = gu[:, 2 * h * LANES:(2 * h + 1) * LANES]
                a1 = gu[:, (2 * h + 1) * LANES:(2 * h + 2) * LANES]
                out = slice(c * GU_CHUNK // 2 + h * LANES, c * GU_CHUNK // 2 + (h + 1) * LANES)
                gate, up = _pair_gate_up(a0, a1, even)
                gate = jnp.minimum(gate, SWIGLU_LIMIT)
                up = jnp.clip(up, -SWIGLU_LIMIT, SWIGLU_LIMIT)
                act_scr[:m, out] = ((up + 1.0) * gate * jax.nn.sigmoid(gate * SWIGLU_ALPHA)).astype(bf16)
        y = jnp.dot(act_scr[:m, :], wd_scr[...], preferred_element_type=f32) + bdn_ref[...]
        y_ref[:m, :] = _pack_rows(y)
        if m < EXPERT_BLOCK:
            y_ref[m:, :] = jnp.zeros((EXPERT_BLOCK - m, HALF), u32)

    active = j < na_ref[0]
    pl.when(jnp.logical_and(active, full_ref[j] == 1))(lambda: run(EXPERT_BLOCK))
    pl.when(jnp.logical_and(active, full_ref[j] == 0))(lambda: run(EXPERT_BLOCK // 2))

    @pl.when(j >= na_ref[0])
    def _():
        y_ref[...] = jnp.zeros_like(y_ref)


def _experts(block_expert, n_active, next_expert, block_full, rows, w_gu, b_gu, w_dn, b_dn):
    n_blocks = rows.shape[0] // EXPERT_BLOCK

    def blk(j, be, na, nxt, full):
        return (jnp.minimum(j, na[0] - 1), 0)

    def per_expert(shape):
        return pl.BlockSpec((None,) + shape, lambda j, be, na, nxt, full: (be[j], 0, 0))

    return pl.pallas_call(
        _experts_kernel,
        grid_spec=pltpu.PrefetchScalarGridSpec(
            num_scalar_prefetch=4,
            grid=(n_blocks,),
            in_specs=[
                pl.BlockSpec((EXPERT_BLOCK, HALF), blk),
                pl.BlockSpec(memory_space=pl.ANY),
                per_expert((1, 2 * D_EXPERT)),
                pl.BlockSpec(memory_space=pl.ANY),
                per_expert((1, D_MODEL)),
            ],
            out_specs=pl.BlockSpec((EXPERT_BLOCK, HALF), lambda j, be, na, nxt, full: (j, 0)),
            scratch_shapes=[
                pltpu.VMEM((D_MODEL, 2 * D_EXPERT), f32),
                pltpu.VMEM((D_EXPERT, D_MODEL), f32),
                pltpu.VMEM((D_MODEL, 2 * D_EXPERT), bf16),
                pltpu.VMEM((D_EXPERT, D_MODEL), bf16),
                pltpu.VMEM((D_MODEL // LANES, LANES, LANES), f32),
                pltpu.VMEM((EXPERT_BLOCK, D_EXPERT), bf16),
                pltpu.SemaphoreType.DMA(()),
                pltpu.SemaphoreType.DMA(()),
            ],
        ),
        out_shape=jax.ShapeDtypeStruct(rows.shape, u32),
        compiler_params=pltpu.CompilerParams(
            dimension_semantics=("arbitrary",), vmem_limit_bytes=VMEM_LIMIT),
        name="experts",
    )(block_expert, n_active, next_expert, block_full, rows, w_gu, b_gu, w_dn, b_dn)


def _combine_kernel(h_ref, gate_ref, y_ref, g2_ref, b2_ref, *rest):
    o_ref = rest[-1]
    pad = jnp.zeros((LANES - GATE_ROWS, ROW_TILE), f32)
    gates = jnp.concatenate([gate_ref[...], pad], axis=0).T
    left = jnp.zeros((ROW_TILE, HALF), f32)
    right = jnp.zeros((ROW_TILE, HALF), f32)
    for k in range(TOP_K):
        yl, yr = _unpack_rows(y_ref[k])
        gk = gates[:, k:k + 1]
        left = left + yl * gk
        right = right + yr * gk
    ffn = jnp.concatenate([left, right], axis=1)
    o_ref[...] = _layer_norm(DEEPNORM_ALPHA * h_ref[...] + ffn, g2_ref[...], b2_ref[...])


def _combine(h1, gates, y_part, g2, b2, part, prev_out):
    n = h1.shape[0]
    tiles = y_part.shape[1] // ROW_TILE
    first = part * tiles
    vec = pl.BlockSpec((1, D_MODEL), lambda i: (0, 0))
    chained = [] if prev_out is None else [prev_out]
    return pl.pallas_call(
        _combine_kernel,
        grid=(tiles,),
        in_specs=[
            pl.BlockSpec((ROW_TILE, D_MODEL), lambda i: (first + i, 0)),
            pl.BlockSpec((GATE_ROWS, ROW_TILE), lambda i: (0, first + i)),
            pl.BlockSpec((TOP_K, ROW_TILE, HALF), lambda i: (0, i, 0)),
            vec, vec,
        ] + [pl.BlockSpec(memory_space=pl.ANY)] * len(chained),
        out_specs=pl.BlockSpec((ROW_TILE, D_MODEL), lambda i: (first + i, 0)),
        out_shape=jax.ShapeDtypeStruct((n, D_MODEL), f32),
        input_output_aliases={5: 0} if chained else {},
        compiler_params=pltpu.CompilerParams(
            dimension_semantics=("arbitrary",), vmem_limit_bytes=VMEM_LIMIT),
        name="combine",
    )(h1, gates, y_part, g2, b2, *chained)


def _rope_tables(pos, scale):
    inv = (1.0 / (ROPE_THETA ** (np.arange(0, A_HEAD_DIM, 2, dtype=np.float32) / A_HEAD_DIM))).astype(np.float32)
    ang = pos.astype(np.float32)[:, None] * inv[None, :]
    ang = np.concatenate([ang, ang, ang, ang], axis=-1)
    sign = np.where((np.arange(LANES) & 63) < 32, -1.0, 1.0).astype(np.float32)
    scale = np.float32(scale)
    return jnp.asarray(np.cos(ang) * scale), jnp.asarray(np.sin(ang) * sign * scale)


def kernel(x, meta_tokens, ln_emb_g, ln_emb_b, w_in, lambda_q1, lambda_k1, lambda_q2, lambda_k2,
           subln_g, hgrn_lb_table, hgrn_norm_g, w_out, ln1_g, ln1_b, w_router, b_router,
           w_gate_up, b_gate_up, w_down, b_down, ln2_g, ln2_b):
    batch, seq, _ = x.shape
    n_tok = batch * seq
    x2 = x.reshape(n_tok, D_MODEL)
    row = lambda v: v.reshape(1, -1).astype(f32)

    w_in_bf = w_in[0].astype(bf16)
    lam = (jnp.exp(jnp.sum(lambda_q1[0].astype(f32) * lambda_k1[0].astype(f32)))
           - jnp.exp(jnp.sum(lambda_q2[0].astype(f32) * lambda_k2[0].astype(f32))) + LAM_INIT).reshape(1)
    lb = jnp.cumsum(jax.nn.softmax(hgrn_lb_table.astype(f32), axis=0), axis=0)[0].reshape(B_HEADS, 1, LANES)
    q_scale = A_HEAD_DIM ** -0.5 * math.log2(math.e)
    pos_x = N_META + np.arange(seq)
    pos_m = np.arange(N_META)
    cq, sq = _rope_tables(pos_x, q_scale)
    ck, sk = _rope_tables(pos_x, 1.0)
    cqm, sqm = _rope_tables(pos_m, q_scale)
    ckm, skm = _rope_tables(pos_m, 1.0)
    wr = w_router[0].astype(f32)
    wr_hi = wr.astype(bf16)
    wr_lo = (wr - wr_hi.astype(f32)).astype(bf16)
    wr_split = jnp.pad(jnp.concatenate([wr_hi, wr_lo], axis=1), ((0, 0), (0, LANES - 2 * N_EXPERTS)))
    bgu = b_gate_up[0].reshape(N_EXPERTS, 1, 2 * D_EXPERT)
    bdn = b_down[0].reshape(N_EXPERTS, 1, D_MODEL)

    g0, b0 = row(ln_emb_g), row(ln_emb_b)
    qkv, hg = _in_proj(x2, g0, b0, w_in_bf, (cq, sq, ck, sk), PROJ_TILE, seq // PROJ_TILE)
    qkv_m, hg_m = _in_proj(meta_tokens, g0, b0, w_in_bf, (cqm, sqm, ckm, skm), N_META, 1)

    ya = _attention(lam, qkv, qkv_m, subln_g[0].reshape(-1, 1).astype(f32), batch, seq)
    yb = _hgrn(hg, hg_m, lb, row(hgrn_norm_g[0]), batch, seq)

    h1, hp, top_idx, gates, rank, counts = _mix_route(
        x2, g0, b0, ya, yb, w_out[0].astype(bf16), row(ln1_g[0]), row(ln1_b[0]),
        wr_split, b_router[0].reshape(-1, 1).astype(f32))

    counts = counts.reshape(N_EXPERTS)
    blocks = (counts + EXPERT_BLOCK - 1) // EXPERT_BLOCK
    cum_blocks = jnp.cumsum(blocks)
    start_pad = (cum_blocks - blocks) * EXPERT_BLOCK
    n_blocks = n_tok * TOP_K // EXPERT_BLOCK + N_EXPERTS
    n_active = cum_blocks[-1:].astype(i32)
    jb = jnp.arange(n_blocks, dtype=i32)
    block_expert = jnp.sum((cum_blocks[None, :] <= jnp.minimum(jb, n_active - 1)[:, None]).astype(i32), axis=1)
    block_expert = jnp.minimum(block_expert, N_EXPERTS - 1).astype(i32)
    eid = jnp.arange(N_EXPERTS, dtype=i32)
    later_active = jnp.logical_and(blocks[None, :] > 0, eid[None, :] > eid[:, None])
    next_active = jnp.min(jnp.where(later_active, eid[None, :], N_EXPERTS), axis=1)
    next_active = jnp.where(next_active == N_EXPERTS, -1, next_active).astype(i32)
    last_rows = counts - (blocks - 1) * EXPERT_BLOCK
    half_last = jnp.logical_and(blocks > 0, last_rows <= EXPERT_BLOCK // 2)
    is_last = jnp.sum(((cum_blocks - 1)[None, :] == jb[:, None]).astype(i32) * half_last[None, :].astype(i32), axis=1)
    block_full = (1 - is_last).astype(i32)
    next_expert = jnp.sum(jnp.where(block_expert[:, None] == eid[None, :], next_active[None, :], 0), axis=1)
    dest = _dest_rows(top_idx, rank, start_pad)
    idx_chunks = dest.reshape(TOP_K, n_tok // SC_CHUNK, SC_CHUNK).transpose(1, 0, 2)

    x_rows = _dispatch(hp, idx_chunks, n_blocks * EXPERT_BLOCK)
    y_rows = _experts(block_expert, n_active, next_expert, block_full, x_rows, w_gate_up[0], bgu, w_down[0], bdn)
    part_tok = n_tok // COMBINE_PARTS
    part_chunks = part_tok // SC_CHUNK
    out = None
    for part in range(COMBINE_PARTS):
        y_part = _gather(y_rows, idx_chunks[part * part_chunks:(part + 1) * part_chunks], part_tok)
        out = _combine(h1, gates, y_part, row(ln2_g[0]), row(ln2_b[0]), part, out)
    return out.reshape(batch, seq, D_MODEL)
```

```python
import functools
import math

import jax
import jax.numpy as jnp
import numpy as np
from jax import lax
from jax.experimental import pallas as pl
from jax.experimental.pallas import tpu as pltpu
from jax.experimental.pallas import tpu_sc as plsc

D_MODEL = 1024
N_META = 16
CHUNK = 64
A_HEADS = 4
A_HEAD_DIM = 64
B_HEADS = 4
B_KEY_DIM = 128
ROPE_THETA = 10000.0
A_W = 512
B_W = 512
D_IN = 3 * A_W + 4 * B_W
N_EXPERTS = 32
TOP_K = 4
D_EXPERT = 1024
SWIGLU_ALPHA = 1.702
SWIGLU_LIMIT = 7.0
EXPERT_BLOCK = 1024
EXPERT_SUB = 256
DEEPNORM_ALPHA = 2.0 ** 0.25
LN_EPS = 1e-5
RMS_EPS = 1e-5
LAM_INIT = 0.8 - 0.6 * math.exp(0.0)

LANES = 128
HALF = D_MODEL // 2
ROW_TILE = 256
ROUTE_TILE = 2 * ROW_TILE
PROJ_TILE = 512
ATT_TILE = 256
HG_TILE = 256
GU_CHUNK = 512
GATE_ROWS = 8
SCORES_AHEAD = 2
SC_CORES = 2
SC_SUBCORES = 16
SC_CHUNK = 64
COMBINE_PARTS = 4
VMEM_LIMIT = 56 * 1024 * 1024

_NT = (((1,), (1,)), ((), ()))
_TN = (((0,), (0,)), ((), ()))

f32 = jnp.float32
bf16 = jnp.bfloat16
u32 = jnp.uint32
i32 = jnp.int32


def _layer_norm(x, g, b):
    mu = jnp.mean(x, axis=-1, keepdims=True)
    xc = x - mu
    var = jnp.mean(xc * xc, axis=-1, keepdims=True)
    return xc * lax.rsqrt(var + LN_EPS) * g + b


def _pack_rows(h):
    hb = h.astype(bf16).astype(f32)
    lo = lax.bitcast_convert_type(hb[:, :HALF], u32) >> 16
    hi = lax.bitcast_convert_type(hb[:, HALF:], u32)
    return hi | lo


def _unpack_rows(w):
    left = lax.bitcast_convert_type(w << 16, f32)
    right = lax.bitcast_convert_type(w & jnp.uint32(0xFFFF0000), f32)
    return left, right


def _rope(x, cos, sin_signed, first_half):
    fwd = pltpu.roll(x, 32, axis=1)
    bwd = pltpu.roll(x, x.shape[1] - 32, axis=1)
    return x * cos + jnp.where(first_half, bwd, fwd) * sin_signed


def _in_proj_kernel(x_ref, g_ref, b_ref, w_ref, cq_ref, sq_ref, ck_ref, sk_ref, qkv_ref, hg_ref):
    h = _layer_norm(x_ref[...], g_ref[...], b_ref[...]).astype(bf16)
    lane = lax.broadcasted_iota(i32, (h.shape[0], A_W), 1)
    first_half = (lane & 63) < 32
    heads = lambda t: jnp.concatenate([t[...]] * A_HEADS, axis=1)
    for c, (cos_ref, sin_ref) in enumerate(((cq_ref, sq_ref), (ck_ref, sk_ref))):
        acc = jnp.dot(h, w_ref[:, c * A_W:(c + 1) * A_W], preferred_element_type=f32)
        r = _rope(acc, heads(cos_ref), heads(sin_ref), first_half)
        qkv_ref[:, c * A_W:(c + 1) * A_W] = r.astype(bf16)
    qkv_ref[:, 2 * A_W:3 * A_W] = jnp.dot(
        h, w_ref[:, 2 * A_W:3 * A_W], preferred_element_type=f32).astype(bf16)
    for c in range(4):
        lo = 3 * A_W + c * B_W
        hg_ref[:, c * B_W:(c + 1) * B_W] = jnp.dot(h, w_ref[:, lo:lo + B_W], preferred_element_type=f32)


def _in_proj(x2, g, b, w_bf, tabs, tile, tab_blocks):
    n = x2.shape[0]
    tab_spec = pl.BlockSpec((tile, LANES), lambda i: (i % tab_blocks, 0))
    vec = pl.BlockSpec((1, D_MODEL), lambda i: (0, 0))
    return pl.pallas_call(
        _in_proj_kernel,
        grid=(n // tile,),
        in_specs=[
            pl.BlockSpec((tile, D_MODEL), lambda i: (i, 0)),
            vec, vec,
            pl.BlockSpec((D_MODEL, D_IN), lambda i: (0, 0)),
            tab_spec, tab_spec, tab_spec, tab_spec,
        ],
        out_specs=[
            pl.BlockSpec((tile, 3 * A_W), lambda i: (i, 0)),
            pl.BlockSpec((tile, 4 * B_W), lambda i: (i, 0)),
        ],
        out_shape=[
            jax.ShapeDtypeStruct((n, 3 * A_W), bf16),
            jax.ShapeDtypeStruct((n, 4 * B_W), f32),
        ],
        compiler_params=pltpu.CompilerParams(
            dimension_semantics=("arbitrary",), vmem_limit_bytes=VMEM_LIMIT),
        name="in_proj",
    )(x2, g, b, w_bf, *tabs)


def _attn_kernel(lam_ref, q_ref, k_ref, v_ref, km_ref, vm_ref, g_ref, o_ref):
    tq = ATT_TILE
    lane = lax.broadcasted_iota(i32, (tq, LANES), 1)
    key = lax.broadcasted_iota(i32, (tq, 2 * tq), 0)
    qry = lax.broadcasted_iota(i32, (tq, 2 * tq), 1)
    visible = ((qry & (tq - 1)) >> 6) >= (key >> 6)
    gain = g_ref[...] * (1.0 - LAM_INIT)
    km = km_ref[...]
    vm = vm_ref[...]
    n_q = q_ref.shape[0] // tq

    def scores(qi):
        q = q_ref[qi * tq:(qi + 1) * tq, :]
        zero = jnp.zeros_like(q)
        qs = jnp.concatenate([jnp.where(lane < A_HEAD_DIM, q, zero),
                              jnp.where(lane >= A_HEAD_DIM, q, zero)], axis=0)
        n = (qi + 1) * tq
        s = lax.dot_general(k_ref[0:n, :], qs, _NT, preferred_element_type=f32)
        diag = jnp.where(visible, s[n - tq:, :], -jnp.inf)
        s = diag if qi == 0 else jnp.concatenate([s[:n - tq, :], diag], axis=0)
        sm = lax.dot_general(km, qs, _NT, preferred_element_type=f32)
        return s, sm

    ahead = [scores(i) for i in range(min(SCORES_AHEAD, n_q))]
    for qi in range(n_q):
        s, sm = ahead.pop(0)
        if qi + SCORES_AHEAD < n_q:
            ahead.append(scores(qi + SCORES_AHEAD))
        n = (qi + 1) * tq
        m = jnp.maximum(jnp.max(s, axis=0, keepdims=True), jnp.max(sm, axis=0, keepdims=True))
        p = jnp.exp2(s - m)
        pm = jnp.exp2(sm - m)
        l = jnp.sum(p, axis=0, keepdims=True) + jnp.sum(pm, axis=0, keepdims=True)
        acc = lax.dot_general(v_ref[0:n, :], p.astype(bf16), _TN, preferred_element_type=f32)
        acc = acc + lax.dot_general(vm, pm.astype(bf16), _TN, preferred_element_type=f32)
        inv = 1.0 / l
        o = acc[:, :tq] * inv[:, :tq] - lam_ref[0] * (acc[:, tq:] * inv[:, tq:])
        ms = jnp.mean(o * o, axis=0, keepdims=True)
        o = o * lax.rsqrt(ms + RMS_EPS) * gain
        o_ref[qi * tq:(qi + 1) * tq, :] = o.T.astype(bf16)


def _attention(lam, qkv, qkv_meta, subln_g, batch, seq):
    return pl.pallas_call(
        _attn_kernel,
        grid_spec=pltpu.PrefetchScalarGridSpec(
            num_scalar_prefetch=1,
            grid=(batch, A_HEADS),
            in_specs=[
                pl.BlockSpec((seq, LANES), lambda b, h, lam: (b, h)),
                pl.BlockSpec((seq, LANES), lambda b, h, lam: (b, A_HEADS + h)),
                pl.BlockSpec((seq, LANES), lambda b, h, lam: (b, 2 * A_HEADS + h)),
                pl.BlockSpec((N_META, LANES), lambda b, h, lam: (0, A_HEADS + h)),
                pl.BlockSpec((N_META, LANES), lambda b, h, lam: (0, 2 * A_HEADS + h)),
                pl.BlockSpec((LANES, 1), lambda b, h, lam: (0, 0)),
            ],
            out_specs=pl.BlockSpec((seq, LANES), lambda b, h, lam: (b, h)),
        ),
        out_shape=jax.ShapeDtypeStruct((batch * seq, A_W), bf16),
        compiler_params=pltpu.CompilerParams(
            dimension_semantics=("arbitrary", "arbitrary"), vmem_limit_bytes=VMEM_LIMIT),
        name="attn",
    )(lam, qkv, qkv, qkv, qkv_meta, qkv_meta, subln_g)


def _split3(x):
    a = x.astype(bf16)
    r = x - a.astype(f32)
    b = r.astype(bf16)
    c = (r - b.astype(f32)).astype(bf16)
    return a, b, c


def _chunk_cumsum(tri, x):
    a, b, c = _split3(x)
    out = jnp.dot(tri, a, preferred_element_type=f32)
    out = out + jnp.dot(tri, b, preferred_element_type=f32)
    return out + jnp.dot(tri, c, preferred_element_type=f32)


def _gates(z, lb):
    sig = jax.nn.sigmoid(z)
    log_f = jnp.log(lb + (1.0 - lb) * sig)
    key = (1.0 - lb) * (1.0 - sig)
    return log_f, key


def _hgrn_kernel(q_ref, f_ref, i_ref, g_ref, fm_ref, im_ref, lb_ref, ng_ref, o_ref,
                 oin_scr, qhat_scr, ut_scr, dec_scr):
    lb = lb_ref[...]
    n_tiles = q_ref.shape[0] // HG_TILE
    per_tile = HG_TILE // CHUNK

    lfm, km = _gates(fm_ref[...], lb)
    r16 = lax.broadcasted_iota(i32, (N_META, N_META), 0)
    c16 = lax.broadcasted_iota(i32, (N_META, N_META), 1)
    bm = _chunk_cumsum((c16 <= r16).astype(bf16), lfm)
    kdm = km * jnp.exp(bm[N_META - 1:N_META, :] - bm)
    st = lax.dot_general(im_ref[...].astype(bf16), kdm.astype(bf16), _TN, preferred_element_type=f32)

    row = lax.broadcasted_iota(i32, (HG_TILE, HG_TILE), 0)
    col = lax.broadcasted_iota(i32, (HG_TILE, HG_TILE), 1)
    causal = ((row >> 6) == (col >> 6)) & (col <= row)
    tri = causal.astype(bf16)

    def decays(t):
        rows = slice(t * HG_TILE, (t + 1) * HG_TILE)
        log_f, kh = _gates(f_ref[rows, :], lb)
        return kh, _chunk_cumsum(tri, log_f)

    ahead = decays(0)
    for t in range(n_tiles):
        rows = slice(t * HG_TILE, (t + 1) * HG_TILE)
        kh, b = ahead
        if t + 1 < n_tiles:
            ahead = decays(t + 1)
        qv = q_ref[rows, :]
        qh = qv * jax.nn.sigmoid(qv) * (B_KEY_DIM ** -0.5)
        b3 = b.reshape(per_tile, CHUNK, LANES)
        b_mid = jnp.broadcast_to(b3[:, CHUNK // 2:CHUNK // 2 + 1, :], b3.shape).reshape(HG_TILE, LANES)
        b_last3 = b3[:, CHUNK - 1:CHUNK, :]
        b_last = jnp.broadcast_to(b_last3, b3.shape).reshape(HG_TILE, LANES)
        qt = (qh * jnp.exp(b - b_mid)).astype(bf16)
        kt = (kh * jnp.exp(jnp.minimum(b_mid - b, 80.0))).astype(bf16)
        a = lax.dot_general(qt, kt, _NT, preferred_element_type=f32)
        a = jnp.where(causal, a, 0.0).astype(bf16)
        vv = i_ref[rows, :].astype(bf16)
        oin_scr[rows, :] = jnp.dot(a, vv, preferred_element_type=f32)
        qhat_scr[rows, :] = (qh * jnp.exp(b)).astype(bf16)
        kd = (kh * jnp.exp(b_last - b)).astype(bf16)
        for c in range(per_tile):
            cr = slice(c * CHUNK, (c + 1) * CHUNK)
            n = t * per_tile + c
            ut_scr[n] = lax.dot_general(vv[cr], kd[cr], _TN, preferred_element_type=f32)
            dec_scr[n] = jnp.exp(b_last3[c])

    ng = ng_ref[...]
    for n in range(n_tiles * per_tile):
        rows = slice(n * CHUNK, (n + 1) * CHUNK)
        o = oin_scr[rows, :] + lax.dot_general(qhat_scr[rows, :], st.astype(bf16), _NT,
                                               preferred_element_type=f32)
        ms = jnp.mean(o * o, axis=-1, keepdims=True)
        gv = g_ref[rows, :]
        o_ref[rows, :] = (o * lax.rsqrt(ms + RMS_EPS) * ng * (gv * jax.nn.sigmoid(gv))).astype(bf16)
        st = st * dec_scr[n] + ut_scr[n]


def _hgrn(hg, hg_meta, lb, norm_g, batch, seq):
    n_chunks = seq // CHUNK

    def col(c):
        return pl.BlockSpec((seq, LANES), lambda b, h: (b, c * B_HEADS + h))

    def mcol(c):
        return pl.BlockSpec((N_META, LANES), lambda b, h: (0, c * B_HEADS + h))

    return pl.pallas_call(
        _hgrn_kernel,
        grid=(batch, B_HEADS),
        in_specs=[
            col(0), col(1), col(2), col(3), mcol(1), mcol(2),
            pl.BlockSpec((None, 1, LANES), lambda b, h: (h, 0, 0)),
            pl.BlockSpec((1, LANES), lambda b, h: (0, 0)),
        ],
        out_specs=pl.BlockSpec((seq, LANES), lambda b, h: (b, h)),
        out_shape=jax.ShapeDtypeStruct((batch * seq, B_W), bf16),
        scratch_shapes=[
            pltpu.VMEM((seq, LANES), f32),
            pltpu.VMEM((seq, LANES), bf16),
            pltpu.VMEM((n_chunks, LANES, LANES), f32),
            pltpu.VMEM((n_chunks, 1, LANES), f32),
        ],
        compiler_params=pltpu.CompilerParams(
            dimension_semantics=("arbitrary", "arbitrary"), vmem_limit_bytes=VMEM_LIMIT),
        name="hgrn",
    )(hg, hg, hg, hg, hg_meta, hg_meta, lb, norm_g)


def _mix_route_kernel(x_ref, g0_ref, b0_ref, ya_ref, yb_ref, wo_ref, g1_ref, b1_ref,
                      wr_ref, br_ref,
                      h_ref, hp_ref, idx_ref, gate_ref, rank_ref, cnt_ref, carry_scr):
    step = pl.program_id(0)

    @pl.when(step == 0)
    def _():
        carry_scr[...] = jnp.zeros_like(carry_scr)

    tm = ROW_TILE
    subs = [slice(t * tm, (t + 1) * tm) for t in range(x_ref.shape[0] // tm)]
    h0 = [_layer_norm(x_ref[r, :], g0_ref[...], b0_ref[...]) for r in subs]
    mix = [jnp.dot(ya_ref[r, :], wo_ref[:A_W, :], preferred_element_type=f32)
           + jnp.dot(yb_ref[r, :], wo_ref[A_W:, :], preferred_element_type=f32) for r in subs]
    h1 = [_layer_norm(DEEPNORM_ALPHA * a + m, g1_ref[...], b1_ref[...]) for a, m in zip(h0, mix)]
    for r, h in zip(subs, h1):
        h_ref[r, :] = h
        hp_ref[r, :] = _pack_rows(h)

    def split_logits(h):
        h_hi = h.astype(bf16)
        h_lo = (h - h_hi.astype(f32)).astype(bf16)
        return (jnp.dot(h_hi, wr_ref[...], preferred_element_type=f32)
                + jnp.dot(h_lo, wr_ref[...], preferred_element_type=f32))

    parts = [split_logits(h) for h in h1]
    eid = lax.broadcasted_iota(i32, (N_EXPERTS, tm), 0)
    r_io = lax.broadcasted_iota(i32, (tm, tm), 0)
    c_io = lax.broadcasted_iota(i32, (tm, tm), 1)
    upper = (r_io <= c_io).astype(bf16)
    carry = carry_scr[...]
    for r, part in zip(subs, parts):
        part = part.T
        work = part[:N_EXPERTS, :] + part[N_EXPERTS:2 * N_EXPERTS, :] + br_ref[...]
        vals, hots = [], []
        sel = jnp.zeros((N_EXPERTS, tm), f32)
        for k in range(TOP_K):
            m = jnp.max(work, axis=0, keepdims=True)
            first = jnp.min(jnp.where(work == m, eid, N_EXPERTS), axis=0, keepdims=True)
            hot = eid == first
            vals.append(m)
            hots.append(hot)
            idx_ref[k:k + 1, r] = first
            sel = sel + hot.astype(f32)
            work = jnp.where(hot, -jnp.inf, work)

        es = [jnp.exp(v - vals[0]) for v in vals]
        denom = es[0] + es[1] + es[2] + es[3]
        for k in range(TOP_K):
            gate_ref[k:k + 1, r] = es[k] / denom
        gate_ref[TOP_K:, r] = jnp.zeros((GATE_ROWS - TOP_K, tm), f32)

        incl = jnp.dot(sel.astype(bf16), upper, preferred_element_type=f32)
        excl = incl - sel + carry
        for k in range(TOP_K):
            rank_ref[k:k + 1, r] = jnp.sum(jnp.where(hots[k], excl, 0.0), axis=0, keepdims=True).astype(i32)
        carry = carry + incl[:, tm - 1:tm]
    carry_scr[...] = carry
    cnt_ref[...] = carry.astype(i32)


def _mix_route(x2, g0, b0, ya, yb, wo, g1, b1, wr, br):
    n = x2.shape[0]
    tm = ROUTE_TILE

    def full(shape):
        return pl.BlockSpec(shape, lambda i: (0,) * len(shape))

    vec = full((1, D_MODEL))
    small = pl.BlockSpec((TOP_K, tm), lambda i: (0, i))
    return pl.pallas_call(
        _mix_route_kernel,
        grid=(n // tm,),
        in_specs=[
            pl.BlockSpec((tm, D_MODEL), lambda i: (i, 0)), vec, vec,
            pl.BlockSpec((tm, A_W), lambda i: (i, 0)),
            pl.BlockSpec((tm, B_W), lambda i: (i, 0)),
            full((D_MODEL, D_MODEL)), vec, vec,
            full((D_MODEL, LANES)), full((N_EXPERTS, 1)),
        ],
        out_specs=[
            pl.BlockSpec((tm, D_MODEL), lambda i: (i, 0)),
            pl.BlockSpec((tm, HALF), lambda i: (i, 0)),
            small, pl.BlockSpec((GATE_ROWS, tm), lambda i: (0, i)), small,
            full((N_EXPERTS, 1)),
        ],
        out_shape=[
            jax.ShapeDtypeStruct((n, D_MODEL), f32),
            jax.ShapeDtypeStruct((n, HALF), u32),
            jax.ShapeDtypeStruct((TOP_K, n), i32),
            jax.ShapeDtypeStruct((GATE_ROWS, n), f32),
            jax.ShapeDtypeStruct((TOP_K, n), i32),
            jax.ShapeDtypeStruct((N_EXPERTS, 1), i32),
        ],
        scratch_shapes=[pltpu.VMEM((N_EXPERTS, 1), f32)],
        compiler_params=pltpu.CompilerParams(
            dimension_semantics=("arbitrary",), vmem_limit_bytes=VMEM_LIMIT),
        name="mix_route",
    )(x2, g0, b0, ya, yb, wo, g1, b1, wr, br)


def _dest_kernel(idx_ref, rank_ref, start_ref, dest_ref):
    eid = lax.broadcasted_iota(i32, (N_EXPERTS, idx_ref.shape[1]), 0)
    start = start_ref[...]
    for k in range(TOP_K):
        base = jnp.sum(jnp.where(eid == idx_ref[k:k + 1, :], start, 0), axis=0, keepdims=True)
        dest_ref[k:k + 1, :] = base + rank_ref[k:k + 1, :]


def _dest_rows(top_idx, rank, start_pad):
    n = top_idx.shape[1]
    tile = 2048
    spec = pl.BlockSpec((TOP_K, tile), lambda i: (0, i))
    return pl.pallas_call(
        _dest_kernel,
        grid=(n // tile,),
        in_specs=[spec, spec, pl.BlockSpec((N_EXPERTS, 1), lambda i: (0, 0))],
        out_specs=spec,
        out_shape=jax.ShapeDtypeStruct((TOP_K, n), i32),
        compiler_params=pltpu.CompilerParams(dimension_semantics=("arbitrary",)),
        name="dest_rows",
    )(top_idx, rank, start_pad.reshape(N_EXPERTS, 1).astype(i32))


def _sc_mesh():
    return plsc.VectorSubcoreMesh(core_axis_name="c", subcore_axis_name="s",
                                  num_cores=SC_CORES, num_subcores=SC_SUBCORES)


def _sc_worker():
    return lax.axis_index("s") * SC_CORES + lax.axis_index("c")


def _dispatch(hp, idx_chunks, n_rows):
    chunk = idx_chunks.shape[2]
    per_worker = hp.shape[0] // chunk // (SC_CORES * SC_SUBCORES)
    row_buf = pltpu.VMEM((chunk, HALF), u32)
    idx_buf = pltpu.VMEM((TOP_K, chunk), i32)

    @functools.partial(
        pl.kernel, mesh=_sc_mesh(),
        out_type=jax.ShapeDtypeStruct((n_rows, HALF), u32),
        scratch_types=[idx_buf, idx_buf, row_buf, row_buf,
                       pltpu.SemaphoreType.DMA, pltpu.SemaphoreType.DMA,
                       pltpu.SemaphoreType.DMA, pltpu.SemaphoreType.DMA],
        name="dispatch_sc",
    )
    def scatter_rows(hp_hbm, idx_hbm, rows_hbm, idx_a, idx_b, buf_a, buf_b, la, lb, sa, sb):
        first = _sc_worker() * per_worker
        idxs, bufs, l_sems, s_sems = (idx_a, idx_b), (buf_a, buf_b), (la, lb), (sa, sb)

        def load(i):
            pltpu.sync_copy(idx_hbm.at[first + i], idxs[i % 2])
            return pltpu.async_copy(hp_hbm.at[pl.ds((first + i) * chunk, chunk)], bufs[i % 2], l_sems[i % 2])

        loads = [None] * per_worker
        scatters = [None] * per_worker
        loads[0] = load(0)
        for i in range(per_worker):
            b = i % 2
            loads[i].wait()
            scatters[i] = [pltpu.async_copy(bufs[b], rows_hbm.at[idxs[b].at[k]], s_sems[b]) for k in range(TOP_K)]
            if i >= 1:
                for copy in scatters[i - 1]:
                    copy.wait()
            if i + 1 < per_worker:
                loads[i + 1] = load(i + 1)
        for copy in scatters[per_worker - 1]:
            copy.wait()

    return scatter_rows(hp, idx_chunks)


def _gather(y_rows, idx_chunks, n_tok):
    chunk = idx_chunks.shape[2]
    per_worker = n_tok // chunk // (SC_CORES * SC_SUBCORES)
    row_buf = pltpu.VMEM((chunk, HALF), u32)

    @functools.partial(
        pl.kernel, mesh=_sc_mesh(),
        out_type=jax.ShapeDtypeStruct((TOP_K, n_tok, HALF), u32),
        scratch_types=[pltpu.VMEM((TOP_K, chunk), i32), row_buf, row_buf,
                       pltpu.SemaphoreType.DMA, pltpu.SemaphoreType.DMA,
                       pltpu.SemaphoreType.DMA, pltpu.SemaphoreType.DMA],
        name="gather_sc",
    )
    def gather_rows(y_hbm, idx_hbm, out_hbm, idx_v, buf_a, buf_b, ga, gb, wa, wb):
        first = _sc_worker() * per_worker
        bufs, g_sems, w_sems = (buf_a, buf_b), (ga, gb), (wa, wb)

        @pl.loop(0, per_worker)
        def _(i):
            c = first + i
            pltpu.sync_copy(idx_hbm.at[c], idx_v)
            gathers = [None] * TOP_K
            writes = [None] * TOP_K
            gathers[0] = pltpu.async_copy(y_hbm.at[idx_v.at[0]], bufs[0], g_sems[0])
            for k in range(TOP_K):
                b = k % 2
                gathers[k].wait()
                writes[k] = pltpu.async_copy(bufs[b], out_hbm.at[k, pl.ds(c * chunk, chunk)], w_sems[b])
                if k >= 1:
                    writes[k - 1].wait()
                if k + 1 < TOP_K:
                    gathers[k + 1] = pltpu.async_copy(y_hbm.at[idx_v.at[k + 1]], bufs[1 - b], g_sems[1 - b])
            writes[TOP_K - 1].wait()

    return gather_rows(y_rows, idx_chunks)


def _pair_gate_up(a0, a1, even):
    gate = jnp.where(even, a0, pltpu.roll(a1, 1, axis=1))
    up = jnp.where(even, pltpu.roll(a0, LANES - 1, axis=1), a1)
    return gate, up


def _experts_kernel(be_ref, na_ref, nxt_ref, full_ref, x_ref, wgu_hbm, bgu_ref, wdn_hbm, bdn_ref, y_ref,
                    gu_stage, dn_stage, wgu_scr, wd_scr, riffle_scr, act_scr, gu_sem, dn_sem):
    j = pl.program_id(0)
    expert = be_ref[j]
    prev = be_ref[jnp.maximum(j - 1, 0)]
    fresh = jnp.logical_or(j == 0, expert != prev)

    def weight_copies(e):
        return (pltpu.make_async_copy(wgu_hbm.at[e], gu_stage, gu_sem),
                pltpu.make_async_copy(wdn_hbm.at[e], dn_stage, dn_sem))

    @pl.when(j == 0)
    def _():
        for c in weight_copies(expert):
            c.start()

    @pl.when(jnp.logical_and(fresh, j < na_ref[0]))
    def _():
        for c in weight_copies(expert):
            c.wait()
        for t in range(2 * D_EXPERT // GU_CHUNK):
            cols = slice(t * GU_CHUNK, (t + 1) * GU_CHUNK)
            wgu_scr[:, cols] = gu_stage[:, cols].astype(bf16)
        half = LANES // 2
        for g in range(D_EXPERT // LANES):
            rows = slice(g * LANES, (g + 1) * LANES)
            for s in range(D_MODEL // LANES):
                lanes = slice(s * LANES, (s + 1) * LANES)
                riffle_scr[s, pl.ds(0, half, stride=2), :] = dn_stage[g * LANES:g * LANES + half, lanes]
                riffle_scr[s, pl.ds(1, half, stride=2), :] = dn_stage[g * LANES + half:(g + 1) * LANES, lanes]
                wd_scr[rows, lanes] = riffle_scr[s].astype(bf16)

        @pl.when(nxt_ref[j] >= 0)
        def _():
            for c in weight_copies(nxt_ref[j]):
                c.start(priority=1)

    def run(m):
        left, right = _unpack_rows(x_ref[:m, :])
        xl = left.astype(bf16)
        xr = right.astype(bf16)
        even = (lax.broadcasted_iota(i32, (m, LANES), 1) & 1) == 0
        for c in range(2 * D_EXPERT // GU_CHUNK):
            cols = slice(c * GU_CHUNK, (c + 1) * GU_CHUNK)
            gu = jnp.dot(xl, wgu_scr[:HALF, cols], preferred_element_type=f32)
            gu = gu + jnp.dot(xr, wgu_scr[HALF:, cols], preferred_element_type=f32) + bgu_ref[:, cols]
            for h in range(GU_CHUNK // (2 * LANES)):
                a0 = gu[:, 2 * h * LANES:(2 * h + 1) * LANES]
                a1 = gu[:, (2 * h + 1) * LANES:(2 * h + 2) * LANES]
                out = slice(c * GU_CHUNK // 2 + h * LANES, c * GU_CHUNK // 2 + (h + 1) * LANES)
                gate, up = _pair_gate_up(a0, a1, even)
                gate = jnp.minimum(gate, SWIGLU_LIMIT)
                up = jnp.clip(up, -SWIGLU_LIMIT, SWIGLU_LIMIT)
                act_scr[:m, out] = ((up + 1.0) * gate * jax.nn.sigmoid(gate * SWIGLU_ALPHA)).astype(bf16)
        y = jnp.dot(act_scr[:m, :], wd_scr[...], preferred_element_type=f32) + bdn_ref[...]
        y_ref[:m, :] = _pack_rows(y)
        if m < EXPERT_BLOCK:
            y_ref[m:, :] = jnp.zeros((EXPERT_BLOCK - m, HALF), u32)

    active = j < na_ref[0]
    for used in range(1, EXPERT_BLOCK // EXPERT_SUB + 1):
        pl.when(jnp.logical_and(active, full_ref[j] == used))(functools.partial(run, used * EXPERT_SUB))

    @pl.when(j >= na_ref[0])
    def _():
        y_ref[...] = jnp.zeros_like(y_ref)


def _experts(block_expert, n_active, next_expert, block_full, rows, w_gu, b_gu, w_dn, b_dn):
    n_blocks = rows.shape[0] // EXPERT_BLOCK

    def blk(j, be, na, nxt, full):
        return (jnp.minimum(j, na[0] - 1), 0)

    def per_expert(shape):
        return pl.BlockSpec((None,) + shape, lambda j, be, na, nxt, full: (be[j], 0, 0))

    return pl.pallas_call(
        _experts_kernel,
        grid_spec=pltpu.PrefetchScalarGridSpec(
            num_scalar_prefetch=4,
            grid=(n_blocks,),
            in_specs=[
                pl.BlockSpec((EXPERT_BLOCK, HALF), blk),
                pl.BlockSpec(memory_space=pl.ANY),
                per_expert((1, 2 * D_EXPERT)),
                pl.BlockSpec(memory_space=pl.ANY),
                per_expert((1, D_MODEL)),
            ],
            out_specs=pl.BlockSpec((EXPERT_BLOCK, HALF), lambda j, be, na, nxt, full: (j, 0)),
            scratch_shapes=[
                pltpu.VMEM((D_MODEL, 2 * D_EXPERT), f32),
                pltpu.VMEM((D_EXPERT, D_MODEL), f32),
                pltpu.VMEM((D_MODEL, 2 * D_EXPERT), bf16),
                pltpu.VMEM((D_EXPERT, D_MODEL), bf16),
                pltpu.VMEM((D_MODEL // LANES, LANES, LANES), f32),
                pltpu.VMEM((EXPERT_BLOCK, D_EXPERT), bf16),
                pltpu.SemaphoreType.DMA(()),
                pltpu.SemaphoreType.DMA(()),
            ],
        ),
        out_shape=jax.ShapeDtypeStruct(rows.shape, u32),
        compiler_params=pltpu.CompilerParams(
            dimension_semantics=("arbitrary",), vmem_limit_bytes=VMEM_LIMIT),
        name="experts",
    )(block_expert, n_active, next_expert, block_full, rows, w_gu, b_gu, w_dn, b_dn)


def _combine_kernel(h_ref, gate_ref, y_ref, g2_ref, b2_ref, *rest):
    o_ref = rest[-1]
    pad = jnp.zeros((LANES - GATE_ROWS, ROW_TILE), f32)
    gates = jnp.concatenate([gate_ref[...], pad], axis=0).T
    left = jnp.zeros((ROW_TILE, HALF), f32)
    right = jnp.zeros((ROW_TILE, HALF), f32)
    for k in range(TOP_K):
        yl, yr = _unpack_rows(y_ref[k])
        gk = gates[:, k:k + 1]
        left = left + yl * gk
        right = right + yr * gk
    ffn = jnp.concatenate([left, right], axis=1)
    o_ref[...] = _layer_norm(DEEPNORM_ALPHA * h_ref[...] + ffn, g2_ref[...], b2_ref[...])


def _combine(h1, gates, y_part, g2, b2, part, prev_out):
    n = h1.shape[0]
    tiles = y_part.shape[1] // ROW_TILE
    first = part * tiles
    vec = pl.BlockSpec((1, D_MODEL), lambda i: (0, 0))
    chained = [] if prev_out is None else [prev_out]
    return pl.pallas_call(
        _combine_kernel,
        grid=(tiles,),
        in_specs=[
            pl.BlockSpec((ROW_TILE, D_MODEL), lambda i: (first + i, 0)),
            pl.BlockSpec((GATE_ROWS, ROW_TILE), lambda i: (0, first + i)),
            pl.BlockSpec((TOP_K, ROW_TILE, HALF), lambda i: (0, i, 0)),
            vec, vec,
        ] + [pl.BlockSpec(memory_space=pl.ANY)] * len(chained),
        out_specs=pl.BlockSpec((ROW_TILE, D_MODEL), lambda i: (first + i, 0)),
        out_shape=jax.ShapeDtypeStruct((n, D_MODEL), f32),
        input_output_aliases={5: 0} if chained else {},
        compiler_params=pltpu.CompilerParams(
            dimension_semantics=("arbitrary",), vmem_limit_bytes=VMEM_LIMIT),
        name="combine",
    )(h1, gates, y_part, g2, b2, *chained)


def _rope_tables(pos, scale):
    inv = (1.0 / (ROPE_THETA ** (np.arange(0, A_HEAD_DIM, 2, dtype=np.float32) / A_HEAD_DIM))).astype(np.float32)
    ang = pos.astype(np.float32)[:, None] * inv[None, :]
    ang = np.concatenate([ang, ang, ang, ang], axis=-1)
    sign = np.where((np.arange(LANES) & 63) < 32, -1.0, 1.0).astype(np.float32)
    scale = np.float32(scale)
    return jnp.asarray(np.cos(ang) * scale), jnp.asarray(np.sin(ang) * sign * scale)


def kernel(x, meta_tokens, ln_emb_g, ln_emb_b, w_in, lambda_q1, lambda_k1, lambda_q2, lambda_k2,
           subln_g, hgrn_lb_table, hgrn_norm_g, w_out, ln1_g, ln1_b, w_router, b_router,
           w_gate_up, b_gate_up, w_down, b_down, ln2_g, ln2_b):
    batch, seq, _ = x.shape
    n_tok = batch * seq
    x2 = x.reshape(n_tok, D_MODEL)
    row = lambda v: v.reshape(1, -1).astype(f32)

    w_in_bf = w_in[0].astype(bf16)
    lam = (jnp.exp(jnp.sum(lambda_q1[0].astype(f32) * lambda_k1[0].astype(f32)))
           - jnp.exp(jnp.sum(lambda_q2[0].astype(f32) * lambda_k2[0].astype(f32))) + LAM_INIT).reshape(1)
    lb = jnp.cumsum(jax.nn.softmax(hgrn_lb_table.astype(f32), axis=0), axis=0)[0].reshape(B_HEADS, 1, LANES)
    q_scale = A_HEAD_DIM ** -0.5 * math.log2(math.e)
    pos_x = N_META + np.arange(seq)
    pos_m = np.arange(N_META)
    cq, sq = _rope_tables(pos_x, q_scale)
    ck, sk = _rope_tables(pos_x, 1.0)
    cqm, sqm = _rope_tables(pos_m, q_scale)
    ckm, skm = _rope_tables(pos_m, 1.0)
    wr = w_router[0].astype(f32)
    wr_hi = wr.astype(bf16)
    wr_lo = (wr - wr_hi.astype(f32)).astype(bf16)
    wr_split = jnp.pad(jnp.concatenate([wr_hi, wr_lo], axis=1), ((0, 0), (0, LANES - 2 * N_EXPERTS)))
    bgu = b_gate_up[0].reshape(N_EXPERTS, 1, 2 * D_EXPERT)
    bdn = b_down[0].reshape(N_EXPERTS, 1, D_MODEL)

    g0, b0 = row(ln_emb_g), row(ln_emb_b)
    qkv, hg = _in_proj(x2, g0, b0, w_in_bf, (cq, sq, ck, sk), PROJ_TILE, seq // PROJ_TILE)
    qkv_m, hg_m = _in_proj(meta_tokens, g0, b0, w_in_bf, (cqm, sqm, ckm, skm), N_META, 1)

    ya = _attention(lam, qkv, qkv_m, subln_g[0].reshape(-1, 1).astype(f32), batch, seq)
    yb = _hgrn(hg, hg_m, lb, row(hgrn_norm_g[0]), batch, seq)

    h1, hp, top_idx, gates, rank, counts = _mix_route(
        x2, g0, b0, ya, yb, w_out[0].astype(bf16), row(ln1_g[0]), row(ln1_b[0]),
        wr_split, b_router[0].reshape(-1, 1).astype(f32))

    counts = counts.reshape(N_EXPERTS)
    blocks = (counts + EXPERT_BLOCK - 1) // EXPERT_BLOCK
    cum_blocks = jnp.cumsum(blocks)
    start_pad = (cum_blocks - blocks) * EXPERT_BLOCK
    n_blocks = n_tok * TOP_K // EXPERT_BLOCK + N_EXPERTS
    n_active = cum_blocks[-1:].astype(i32)
    jb = jnp.arange(n_blocks, dtype=i32)
    block_expert = jnp.sum((cum_blocks[None, :] <= jnp.minimum(jb, n_active - 1)[:, None]).astype(i32), axis=1)
    block_expert = jnp.minimum(block_expert, N_EXPERTS - 1).astype(i32)
    eid = jnp.arange(N_EXPERTS, dtype=i32)
    later_active = jnp.logical_and(blocks[None, :] > 0, eid[None, :] > eid[:, None])
    next_active = jnp.min(jnp.where(later_active, eid[None, :], N_EXPERTS), axis=1)
    next_active = jnp.where(next_active == N_EXPERTS, -1, next_active).astype(i32)
    per_block = EXPERT_BLOCK // EXPERT_SUB
    last_used = (counts - (blocks - 1) * EXPERT_BLOCK + EXPERT_SUB - 1) // EXPERT_SUB
    is_last = jnp.logical_and((cum_blocks - 1)[None, :] == jb[:, None], blocks[None, :] > 0)
    block_full = (per_block - jnp.sum(jnp.where(is_last, per_block - last_used[None, :], 0), axis=1)).astype(i32)
    next_expert = jnp.sum(jnp.where(block_expert[:, None] == eid[None, :], next_active[None, :], 0), axis=1)
    dest = _dest_rows(top_idx, rank, start_pad)
    idx_chunks = dest.reshape(TOP_K, n_tok // SC_CHUNK, SC_CHUNK).transpose(1, 0, 2)

    x_rows = _dispatch(hp, idx_chunks, n_blocks * EXPERT_BLOCK)
    y_rows = _experts(block_expert, n_active, next_expert, block_full, x_rows, w_gate_up[0], bgu, w_down[0], bdn)
    part_tok = n_tok // COMBINE_PARTS
    part_chunks = part_tok // SC_CHUNK
    out = None
    for part in range(COMBINE_PARTS):
        y_part = _gather(y_rows, idx_chunks[part * part_chunks:(part + 1) * part_chunks], part_tok)
        out = _combine(h1, gates, y_part, row(ln2_g[0]), row(ln2_b[0]), part, out)
    return out.reshape(batch, seq, D_MODEL)
```

```python
import functools
import math

import jax
import jax.numpy as jnp
import numpy as np
from jax import lax
from jax.experimental import pallas as pl
from jax.experimental.pallas import tpu as pltpu
from jax.experimental.pallas import tpu_sc as plsc

D_MODEL = 1024
N_META = 16
CHUNK = 64
A_HEADS = 4
A_HEAD_DIM = 64
B_HEADS = 4
B_KEY_DIM = 128
ROPE_THETA = 10000.0
A_W = 512
B_W = 512
D_IN = 3 * A_W + 4 * B_W
N_EXPERTS = 32
TOP_K = 4
D_EXPERT = 1024
SWIGLU_ALPHA = 1.702
SWIGLU_LIMIT = 7.0
EXPERT_BLOCK = 1024
EXPERT_SUB = 256
DEEPNORM_ALPHA = 2.0 ** 0.25
LN_EPS = 1e-5
RMS_EPS = 1e-5
LAM_INIT = 0.8 - 0.6 * math.exp(0.0)

LANES = 128
HALF = D_MODEL // 2
ROW_TILE = 256
ROUTE_TILE = 4 * ROW_TILE
PROJ_TILE = 1024
ATT_TILE = 256
HG_TILE = 256
GU_CHUNK = 512
GATE_ROWS = 8
SCORES_AHEAD = 2
SC_CORES = 2
SC_SUBCORES = 16
SC_CHUNK = 64
SC_SCATTER_CHUNK = 128
COMBINE_PARTS = (1, 2, 2, 2, 1)
VMEM_LIMIT = 56 * 1024 * 1024

_NT = (((1,), (1,)), ((), ()))
_TN = (((0,), (0,)), ((), ()))

f32 = jnp.float32
bf16 = jnp.bfloat16
u32 = jnp.uint32
i32 = jnp.int32


def _layer_norm(x, g, b):
    mu = jnp.mean(x, axis=-1, keepdims=True)
    xc = x - mu
    var = jnp.mean(xc * xc, axis=-1, keepdims=True)
    return xc * lax.rsqrt(var + LN_EPS) * g + b


def _pack_rows(h):
    hb = h.astype(bf16).astype(f32)
    lo = lax.bitcast_convert_type(hb[:, :HALF], u32) >> 16
    hi = lax.bitcast_convert_type(hb[:, HALF:], u32)
    return hi | lo


def _unpack_rows(w):
    left = lax.bitcast_convert_type(w << 16, f32)
    right = lax.bitcast_convert_type(w & jnp.uint32(0xFFFF0000), f32)
    return left, right


def _rope(x, cos, sin_signed, first_half):
    fwd = pltpu.roll(x, 32, axis=1)
    bwd = pltpu.roll(x, x.shape[1] - 32, axis=1)
    return x * cos + jnp.where(first_half, bwd, fwd) * sin_signed


def _in_proj_kernel(x_ref, g_ref, b_ref, w_ref, cq_ref, sq_ref, ck_ref, sk_ref, qkv_ref, hg_ref):
    h = _layer_norm(x_ref[...], g_ref[...], b_ref[...]).astype(bf16)
    lane = lax.broadcasted_iota(i32, (h.shape[0], A_W), 1)
    first_half = (lane & 63) < 32
    heads = lambda t: jnp.concatenate([t[...]] * A_HEADS, axis=1)
    for c, (cos_ref, sin_ref) in enumerate(((cq_ref, sq_ref), (ck_ref, sk_ref))):
        acc = jnp.dot(h, w_ref[:, c * A_W:(c + 1) * A_W], preferred_element_type=f32)
        r = _rope(acc, heads(cos_ref), heads(sin_ref), first_half)
        qkv_ref[:, c * A_W:(c + 1) * A_W] = r.astype(bf16)
    qkv_ref[:, 2 * A_W:3 * A_W] = jnp.dot(
        h, w_ref[:, 2 * A_W:3 * A_W], preferred_element_type=f32).astype(bf16)
    for c in range(4):
        lo = 3 * A_W + c * B_W
        hg_ref[:, c * B_W:(c + 1) * B_W] = jnp.dot(h, w_ref[:, lo:lo + B_W], preferred_element_type=f32)


def _in_proj(x2, g, b, w_bf, tabs, tile, tab_blocks):
    n = x2.shape[0]
    tab_spec = pl.BlockSpec((tile, LANES), lambda i: (i % tab_blocks, 0))
    vec = pl.BlockSpec((1, D_MODEL), lambda i: (0, 0))
    return pl.pallas_call(
        _in_proj_kernel,
        grid=(n // tile,),
        in_specs=[
            pl.BlockSpec((tile, D_MODEL), lambda i: (i, 0)),
            vec, vec,
            pl.BlockSpec((D_MODEL, D_IN), lambda i: (0, 0)),
            tab_spec, tab_spec, tab_spec, tab_spec,
        ],
        out_specs=[
            pl.BlockSpec((tile, 3 * A_W), lambda i: (i, 0)),
            pl.BlockSpec((tile, 4 * B_W), lambda i: (i, 0)),
        ],
        out_shape=[
            jax.ShapeDtypeStruct((n, 3 * A_W), bf16),
            jax.ShapeDtypeStruct((n, 4 * B_W), f32),
        ],
        compiler_params=pltpu.CompilerParams(
            dimension_semantics=("arbitrary",), vmem_limit_bytes=VMEM_LIMIT),
        name="in_proj",
    )(x2, g, b, w_bf, *tabs)


def _attn_kernel(lam_ref, q_ref, k_ref, v_ref, km_ref, vm_ref, g_ref, o_ref):
    tq = ATT_TILE
    lane = lax.broadcasted_iota(i32, (tq, LANES), 1)
    key = lax.broadcasted_iota(i32, (tq, 2 * tq), 0)
    qry = lax.broadcasted_iota(i32, (tq, 2 * tq), 1)
    visible = ((qry & (tq - 1)) >> 6) >= (key >> 6)
    gain = g_ref[...] * (1.0 - LAM_INIT)
    km = km_ref[...]
    vm = vm_ref[...]
    n_q = q_ref.shape[0] // tq

    def scores(qi):
        q = q_ref[qi * tq:(qi + 1) * tq, :]
        zero = jnp.zeros_like(q)
        qs = jnp.concatenate([jnp.where(lane < A_HEAD_DIM, q, zero),
                              jnp.where(lane >= A_HEAD_DIM, q, zero)], axis=0)
        n = (qi + 1) * tq
        s = lax.dot_general(k_ref[0:n, :], qs, _NT, preferred_element_type=f32)
        diag = jnp.where(visible, s[n - tq:, :], -jnp.inf)
        s = diag if qi == 0 else jnp.concatenate([s[:n - tq, :], diag], axis=0)
        sm = lax.dot_general(km, qs, _NT, preferred_element_type=f32)
        return s, sm

    ahead = [scores(i) for i in range(min(SCORES_AHEAD, n_q))]
    for qi in range(n_q):
        s, sm = ahead.pop(0)
        if qi + SCORES_AHEAD < n_q:
            ahead.append(scores(qi + SCORES_AHEAD))
        n = (qi + 1) * tq
        m = jnp.maximum(jnp.max(s, axis=0, keepdims=True), jnp.max(sm, axis=0, keepdims=True))
        p = jnp.exp2(s - m)
        pm = jnp.exp2(sm - m)
        l = jnp.sum(p, axis=0, keepdims=True) + jnp.sum(pm, axis=0, keepdims=True)
        acc = lax.dot_general(v_ref[0:n, :], p.astype(bf16), _TN, preferred_element_type=f32)
        acc = acc + lax.dot_general(vm, pm.astype(bf16), _TN, preferred_element_type=f32)
        inv = 1.0 / l
        o = acc[:, :tq] * inv[:, :tq] - lam_ref[0] * (acc[:, tq:] * inv[:, tq:])
        ms = jnp.mean(o * o, axis=0, keepdims=True)
        o = o * lax.rsqrt(ms + RMS_EPS) * gain
        o_ref[qi * tq:(qi + 1) * tq, :] = o.T.astype(bf16)


def _attention(lam, qkv, qkv_meta, subln_g, batch, seq):
    return pl.pallas_call(
        _attn_kernel,
        grid_spec=pltpu.PrefetchScalarGridSpec(
            num_scalar_prefetch=1,
            grid=(batch, A_HEADS),
            in_specs=[
                pl.BlockSpec((seq, LANES), lambda b, h, lam: (b, h)),
                pl.BlockSpec((seq, LANES), lambda b, h, lam: (b, A_HEADS + h)),
                pl.BlockSpec((seq, LANES), lambda b, h, lam: (b, 2 * A_HEADS + h)),
                pl.BlockSpec((N_META, LANES), lambda b, h, lam: (0, A_HEADS + h)),
                pl.BlockSpec((N_META, LANES), lambda b, h, lam: (0, 2 * A_HEADS + h)),
                pl.BlockSpec((LANES, 1), lambda b, h, lam: (0, 0)),
            ],
            out_specs=pl.BlockSpec((seq, LANES), lambda b, h, lam: (b, h)),
        ),
        out_shape=jax.ShapeDtypeStruct((batch * seq, A_W), bf16),
        compiler_params=pltpu.CompilerParams(
            dimension_semantics=("arbitrary", "arbitrary"), vmem_limit_bytes=VMEM_LIMIT),
        name="attn",
    )(lam, qkv, qkv, qkv, qkv_meta, qkv_meta, subln_g)


def _split3(x):
    a = x.astype(bf16)
    r = x - a.astype(f32)
    b = r.astype(bf16)
    c = (r - b.astype(f32)).astype(bf16)
    return a, b, c


def _chunk_cumsum(tri, x):
    a, b, c = _split3(x)
    out = jnp.dot(tri, a, preferred_element_type=f32)
    out = out + jnp.dot(tri, b, preferred_element_type=f32)
    return out + jnp.dot(tri, c, preferred_element_type=f32)


def _gates(z, lb):
    sig = jax.nn.sigmoid(z)
    log_f = jnp.log(lb + (1.0 - lb) * sig)
    key = (1.0 - lb) * (1.0 - sig)
    return log_f, key


def _hgrn_kernel(q_ref, f_ref, i_ref, g_ref, fm_ref, im_ref, lb_ref, ng_ref, o_ref,
                 oin_scr, qhat_scr, ut_scr, dec_scr):
    lb = lb_ref[...]
    n_tiles = q_ref.shape[0] // HG_TILE
    per_tile = HG_TILE // CHUNK

    lfm, km = _gates(fm_ref[...], lb)
    r16 = lax.broadcasted_iota(i32, (N_META, N_META), 0)
    c16 = lax.broadcasted_iota(i32, (N_META, N_META), 1)
    bm = _chunk_cumsum((c16 <= r16).astype(bf16), lfm)
    kdm = km * jnp.exp(bm[N_META - 1:N_META, :] - bm)
    st = lax.dot_general(im_ref[...].astype(bf16), kdm.astype(bf16), _TN, preferred_element_type=f32)

    row = lax.broadcasted_iota(i32, (HG_TILE, HG_TILE), 0)
    col = lax.broadcasted_iota(i32, (HG_TILE, HG_TILE), 1)
    causal = ((row >> 6) == (col >> 6)) & (col <= row)
    tri = causal.astype(bf16)

    def decays(t):
        rows = slice(t * HG_TILE, (t + 1) * HG_TILE)
        log_f, kh = _gates(f_ref[rows, :], lb)
        return kh, _chunk_cumsum(tri, log_f)

    ahead = decays(0)
    for t in range(n_tiles):
        rows = slice(t * HG_TILE, (t + 1) * HG_TILE)
        kh, b = ahead
        if t + 1 < n_tiles:
            ahead = decays(t + 1)
        qv = q_ref[rows, :]
        qh = qv * jax.nn.sigmoid(qv) * (B_KEY_DIM ** -0.5)
        b3 = b.reshape(per_tile, CHUNK, LANES)
        b_mid = jnp.broadcast_to(b3[:, CHUNK // 2:CHUNK // 2 + 1, :], b3.shape).reshape(HG_TILE, LANES)
        b_last3 = b3[:, CHUNK - 1:CHUNK, :]
        b_last = jnp.broadcast_to(b_last3, b3.shape).reshape(HG_TILE, LANES)
        qt = (qh * jnp.exp(b - b_mid)).astype(bf16)
        kt = (kh * jnp.exp(jnp.minimum(b_mid - b, 80.0))).astype(bf16)
        a = lax.dot_general(qt, kt, _NT, preferred_element_type=f32)
        a = jnp.where(causal, a, 0.0).astype(bf16)
        vv = i_ref[rows, :].astype(bf16)
        oin_scr[rows, :] = jnp.dot(a, vv, preferred_element_type=f32)
        qhat_scr[rows, :] = (qh * jnp.exp(b)).astype(bf16)
        kd = (kh * jnp.exp(b_last - b)).astype(bf16)
        for c in range(per_tile):
            cr = slice(c * CHUNK, (c + 1) * CHUNK)
            n = t * per_tile + c
            ut_scr[n] = lax.dot_general(vv[cr], kd[cr], _TN, preferred_element_type=f32)
            dec_scr[n] = jnp.exp(b_last3[c])

    ng = ng_ref[...]
    for n in range(n_tiles * per_tile):
        rows = slice(n * CHUNK, (n + 1) * CHUNK)
        o = oin_scr[rows, :] + lax.dot_general(qhat_scr[rows, :], st.astype(bf16), _NT,
                                               preferred_element_type=f32)
        ms = jnp.mean(o * o, axis=-1, keepdims=True)
        gv = g_ref[rows, :]
        o_ref[rows, :] = (o * lax.rsqrt(ms + RMS_EPS) * ng * (gv * jax.nn.sigmoid(gv))).astype(bf16)
        st = st * dec_scr[n] + ut_scr[n]


def _hgrn(hg, hg_meta, lb, norm_g, batch, seq):
    n_chunks = seq // CHUNK

    def col(c):
        return pl.BlockSpec((seq, LANES), lambda b, h: (b, c * B_HEADS + h))

    def mcol(c):
        return pl.BlockSpec((N_META, LANES), lambda b, h: (0, c * B_HEADS + h))

    return pl.pallas_call(
        _hgrn_kernel,
        grid=(batch, B_HEADS),
        in_specs=[
            col(0), col(1), col(2), col(3), mcol(1), mcol(2),
            pl.BlockSpec((None, 1, LANES), lambda b, h: (h, 0, 0)),
            pl.BlockSpec((1, LANES), lambda b, h: (0, 0)),
        ],
        out_specs=pl.BlockSpec((seq, LANES), lambda b, h: (b, h)),
        out_shape=jax.ShapeDtypeStruct((batch * seq, B_W), bf16),
        scratch_shapes=[
            pltpu.VMEM((seq, LANES), f32),
            pltpu.VMEM((seq, LANES), bf16),
            pltpu.VMEM((n_chunks, LANES, LANES), f32),
            pltpu.VMEM((n_chunks, 1, LANES), f32),
        ],
        compiler_params=pltpu.CompilerParams(
            dimension_semantics=("arbitrary", "arbitrary"), vmem_limit_bytes=VMEM_LIMIT),
        name="hgrn",
    )(hg, hg, hg, hg, hg_meta, hg_meta, lb, norm_g)


def _mix_route_kernel(x_ref, g0_ref, b0_ref, ya_ref, yb_ref, wo_ref, g1_ref, b1_ref,
                      wr_ref, br_ref,
                      h_ref, hp_ref, idx_ref, gate_ref, rank_ref, cnt_ref, carry_scr):
    step = pl.program_id(0)

    @pl.when(step == 0)
    def _():
        carry_scr[...] = jnp.zeros_like(carry_scr)

    tm = ROW_TILE
    subs = [slice(t * tm, (t + 1) * tm) for t in range(x_ref.shape[0] // tm)]
    h0 = [_layer_norm(x_ref[r, :], g0_ref[...], b0_ref[...]) for r in subs]
    mix = [jnp.dot(ya_ref[r, :], wo_ref[:A_W, :], preferred_element_type=f32)
           + jnp.dot(yb_ref[r, :], wo_ref[A_W:, :], preferred_element_type=f32) for r in subs]
    h1 = [_layer_norm(DEEPNORM_ALPHA * a + m, g1_ref[...], b1_ref[...]) for a, m in zip(h0, mix)]
    for r, h in zip(subs, h1):
        h_ref[r, :] = h
        hp_ref[r, :] = _pack_rows(h)

    def split_logits(h):
        h_hi = h.astype(bf16)
        h_lo = (h - h_hi.astype(f32)).astype(bf16)
        return (jnp.dot(h_hi, wr_ref[...], preferred_element_type=f32)
                + jnp.dot(h_lo, wr_ref[...], preferred_element_type=f32))

    parts = [split_logits(h) for h in h1]
    eid = lax.broadcasted_iota(i32, (N_EXPERTS, tm), 0)
    r_io = lax.broadcasted_iota(i32, (tm, tm), 0)
    c_io = lax.broadcasted_iota(i32, (tm, tm), 1)
    upper = (r_io <= c_io).astype(bf16)
    carry = carry_scr[...]
    for r, part in zip(subs, parts):
        part = part.T
        work = part[:N_EXPERTS, :] + part[N_EXPERTS:2 * N_EXPERTS, :] + br_ref[...]
        vals, hots = [], []
        sel = jnp.zeros((N_EXPERTS, tm), f32)
        for k in range(TOP_K):
            m = jnp.max(work, axis=0, keepdims=True)
            first = jnp.min(jnp.where(work == m, eid, N_EXPERTS), axis=0, keepdims=True)
            hot = eid == first
            vals.append(m)
            hots.append(hot)
            idx_ref[k:k + 1, r] = first
            sel = sel + hot.astype(f32)
            work = jnp.where(hot, -jnp.inf, work)

        es = [jnp.exp(v - vals[0]) for v in vals]
        denom = es[0] + es[1] + es[2] + es[3]
        for k in range(TOP_K):
            gate_ref[k:k + 1, r] = es[k] / denom
        gate_ref[TOP_K:, r] = jnp.zeros((GATE_ROWS - TOP_K, tm), f32)

        incl = jnp.dot(sel.astype(bf16), upper, preferred_element_type=f32)
        excl = incl - sel + carry
        for k in range(TOP_K):
            rank_ref[k:k + 1, r] = jnp.sum(jnp.where(hots[k], excl, 0.0), axis=0, keepdims=True).astype(i32)
        carry = carry + incl[:, tm - 1:tm]
    carry_scr[...] = carry
    cnt_ref[...] = carry.astype(i32)


def _mix_route(x2, g0, b0, ya, yb, wo, g1, b1, wr, br):
    n = x2.shape[0]
    tm = ROUTE_TILE

    def full(shape):
        return pl.BlockSpec(shape, lambda i: (0,) * len(shape))

    vec = full((1, D_MODEL))
    small = pl.BlockSpec((TOP_K, tm), lambda i: (0, i))
    return pl.pallas_call(
        _mix_route_kernel,
        grid=(n // tm,),
        in_specs=[
            pl.BlockSpec((tm, D_MODEL), lambda i: (i, 0)), vec, vec,
            pl.BlockSpec((tm, A_W), lambda i: (i, 0)),
            pl.BlockSpec((tm, B_W), lambda i: (i, 0)),
            full((D_MODEL, D_MODEL)), vec, vec,
            full((D_MODEL, LANES)), full((N_EXPERTS, 1)),
        ],
        out_specs=[
            pl.BlockSpec((tm, D_MODEL), lambda i: (i, 0)),
            pl.BlockSpec((tm, HALF), lambda i: (i, 0)),
            small, pl.BlockSpec((GATE_ROWS, tm), lambda i: (0, i)), small,
            full((N_EXPERTS, 1)),
        ],
        out_shape=[
            jax.ShapeDtypeStruct((n, D_MODEL), f32),
            jax.ShapeDtypeStruct((n, HALF), u32),
            jax.ShapeDtypeStruct((TOP_K, n), i32),
            jax.ShapeDtypeStruct((GATE_ROWS, n), f32),
            jax.ShapeDtypeStruct((TOP_K, n), i32),
            jax.ShapeDtypeStruct((N_EXPERTS, 1), i32),
        ],
        scratch_shapes=[pltpu.VMEM((N_EXPERTS, 1), f32)],
        compiler_params=pltpu.CompilerParams(
            dimension_semantics=("arbitrary",), vmem_limit_bytes=VMEM_LIMIT),
        name="mix_route",
    )(x2, g0, b0, ya, yb, wo, g1, b1, wr, br)


def _dest_kernel(idx_ref, rank_ref, start_ref, dest_ref):
    eid = lax.broadcasted_iota(i32, (N_EXPERTS, idx_ref.shape[1]), 0)
    start = start_ref[...]
    for k in range(TOP_K):
        base = jnp.sum(jnp.where(eid == idx_ref[k:k + 1, :], start, 0), axis=0, keepdims=True)
        dest_ref[k:k + 1, :] = base + rank_ref[k:k + 1, :]


def _dest_rows(top_idx, rank, start_pad):
    n = top_idx.shape[1]
    tile = 2048
    spec = pl.BlockSpec((TOP_K, tile), lambda i: (0, i))
    return pl.pallas_call(
        _dest_kernel,
        grid=(n // tile,),
        in_specs=[spec, spec, pl.BlockSpec((N_EXPERTS, 1), lambda i: (0, 0))],
        out_specs=spec,
        out_shape=jax.ShapeDtypeStruct((TOP_K, n), i32),
        compiler_params=pltpu.CompilerParams(dimension_semantics=("arbitrary",)),
        name="dest_rows",
    )(top_idx, rank, start_pad.reshape(N_EXPERTS, 1).astype(i32))


def _sc_mesh():
    return plsc.VectorSubcoreMesh(core_axis_name="c", subcore_axis_name="s",
                                  num_cores=SC_CORES, num_subcores=SC_SUBCORES)


def _sc_worker():
    return lax.axis_index("s") * SC_CORES + lax.axis_index("c")


def _dispatch(hp, idx_chunks, n_rows):
    chunk = idx_chunks.shape[2]
    per_worker = hp.shape[0] // chunk // (SC_CORES * SC_SUBCORES)

    @functools.partial(
        pl.kernel, mesh=_sc_mesh(),
        out_type=jax.ShapeDtypeStruct((n_rows, HALF), u32),
        scratch_types=[pltpu.VMEM((TOP_K, chunk), i32), pltpu.VMEM((chunk, HALF), u32),
                       pltpu.SemaphoreType.DMA],
        name="dispatch_sc",
    )
    def scatter_rows(hp_hbm, idx_hbm, rows_hbm, idx_v, rows_v, sem):
        first = _sc_worker() * per_worker

        @pl.loop(0, per_worker)
        def _(i):
            c = first + i
            pltpu.sync_copy(idx_hbm.at[c], idx_v)
            pltpu.sync_copy(hp_hbm.at[pl.ds(c * chunk, chunk)], rows_v)
            copies = [pltpu.async_copy(rows_v, rows_hbm.at[idx_v.at[k]], sem) for k in range(TOP_K)]
            for copy in copies:
                copy.wait()

    return scatter_rows(hp, idx_chunks)


def _gather(y_rows, idx_chunks, n_tok):
    chunk = idx_chunks.shape[2]
    per_worker = n_tok // chunk // (SC_CORES * SC_SUBCORES)
    row_buf = pltpu.VMEM((chunk, HALF), u32)

    @functools.partial(
        pl.kernel, mesh=_sc_mesh(),
        out_type=jax.ShapeDtypeStruct((TOP_K, n_tok, HALF), u32),
        scratch_types=[pltpu.VMEM((TOP_K, chunk), i32), row_buf, row_buf,
                       pltpu.SemaphoreType.DMA, pltpu.SemaphoreType.DMA,
                       pltpu.SemaphoreType.DMA, pltpu.SemaphoreType.DMA],
        name="gather_sc",
    )
    def gather_rows(y_hbm, idx_hbm, out_hbm, idx_v, buf_a, buf_b, ga, gb, wa, wb):
        first = _sc_worker() * per_worker
        bufs, g_sems, w_sems = (buf_a, buf_b), (ga, gb), (wa, wb)

        @pl.loop(0, per_worker)
        def _(i):
            c = first + i
            pltpu.sync_copy(idx_hbm.at[c], idx_v)
            gathers = [None] * TOP_K
            writes = [None] * TOP_K
            gathers[0] = pltpu.async_copy(y_hbm.at[idx_v.at[0]], bufs[0], g_sems[0])
            for k in range(TOP_K):
                b = k % 2
                gathers[k].wait()
                writes[k] = pltpu.async_copy(bufs[b], out_hbm.at[k, pl.ds(c * chunk, chunk)], w_sems[b])
                if k >= 1:
                    writes[k - 1].wait()
                if k + 1 < TOP_K:
                    gathers[k + 1] = pltpu.async_copy(y_hbm.at[idx_v.at[k + 1]], bufs[1 - b], g_sems[1 - b])
            writes[TOP_K - 1].wait()

    return gather_rows(y_rows, idx_chunks)


def _pair_gate_up(a0, a1, even):
    gate = jnp.where(even, a0, pltpu.roll(a1, 1, axis=1))
    up = jnp.where(even, pltpu.roll(a0, LANES - 1, axis=1), a1)
    return gate, up


def _experts_kernel(be_ref, na_ref, nxt_ref, full_ref, x_ref, wgu_hbm, bgu_ref, wdn_hbm, bdn_ref, y_ref,
                    gu_stage, dn_stage, wgu_scr, wd_scr, riffle_scr, act_scr, gu_sem, dn_sem):
    j = pl.program_id(0)
    expert = be_ref[j]
    prev = be_ref[jnp.maximum(j - 1, 0)]
    fresh = jnp.logical_or(j == 0, expert != prev)

    def weight_copies(e):
        return (pltpu.make_async_copy(wgu_hbm.at[e], gu_stage, gu_sem),
                pltpu.make_async_copy(wdn_hbm.at[e], dn_stage, dn_sem))

    @pl.when(j == 0)
    def _():
        for c in weight_copies(expert):
            c.start()

    @pl.when(jnp.logical_and(fresh, j < na_ref[0]))
    def _():
        for c in weight_copies(expert):
            c.wait()
        for t in range(2 * D_EXPERT // GU_CHUNK):
            cols = slice(t * GU_CHUNK, (t + 1) * GU_CHUNK)
            wgu_scr[:, cols] = gu_stage[:, cols].astype(bf16)
        half = LANES // 2
        for g in range(D_EXPERT // LANES):
            rows = slice(g * LANES, (g + 1) * LANES)
            for s in range(D_MODEL // LANES):
                lanes = slice(s * LANES, (s + 1) * LANES)
                riffle_scr[s, pl.ds(0, half, stride=2), :] = dn_stage[g * LANES:g * LANES + half, lanes]
                riffle_scr[s, pl.ds(1, half, stride=2), :] = dn_stage[g * LANES + half:(g + 1) * LANES, lanes]
                wd_scr[rows, lanes] = riffle_scr[s].astype(bf16)

        @pl.when(nxt_ref[j] >= 0)
        def _():
            for c in weight_copies(nxt_ref[j]):
                c.start(priority=1)

    def run(m):
        left, right = _unpack_rows(x_ref[:m, :])
        x = jnp.concatenate([left.astype(bf16), right.astype(bf16)], axis=1)
        even = (lax.broadcasted_iota(i32, (m, LANES), 1) & 1) == 0
        for c in range(2 * D_EXPERT // GU_CHUNK):
            cols = slice(c * GU_CHUNK, (c + 1) * GU_CHUNK)
            gu = jnp.dot(x, wgu_scr[:, cols], preferred_element_type=f32) + bgu_ref[:, cols]
            for h in range(GU_CHUNK // (2 * LANES)):
                a0 = gu[:, 2 * h * LANES:(2 * h + 1) * LANES]
                a1 = gu[:, (2 * h + 1) * LANES:(2 * h + 2) * LANES]
                out = slice(c * GU_CHUNK // 2 + h * LANES, c * GU_CHUNK // 2 + (h + 1) * LANES)
                gate, up = _pair_gate_up(a0, a1, even)
                gate = jnp.minimum(gate, SWIGLU_LIMIT)
                up = jnp.clip(up, -SWIGLU_LIMIT, SWIGLU_LIMIT)
                act_scr[:m, out] = ((up + 1.0) * gate * jax.nn.sigmoid(gate * SWIGLU_ALPHA)).astype(bf16)
        y = jnp.dot(act_scr[:m, :], wd_scr[...], preferred_element_type=f32) + bdn_ref[...]
        y_ref[:m, :] = _pack_rows(y)
        if m < EXPERT_BLOCK:
            y_ref[m:, :] = jnp.zeros((EXPERT_BLOCK - m, HALF), u32)

    active = j < na_ref[0]
    for used in range(1, EXPERT_BLOCK // EXPERT_SUB + 1):
        pl.when(jnp.logical_and(active, full_ref[j] == used))(functools.partial(run, used * EXPERT_SUB))


def _experts(block_expert, n_active, next_expert, block_full, rows, w_gu, b_gu, w_dn, b_dn):
    n_blocks = rows.shape[0] // EXPERT_BLOCK

    def blk(j, be, na, nxt, full):
        return (jnp.minimum(j, na[0] - 1), 0)

    def per_expert(shape):
        return pl.BlockSpec((None,) + shape, lambda j, be, na, nxt, full: (be[j], 0, 0))

    return pl.pallas_call(
        _experts_kernel,
        grid_spec=pltpu.PrefetchScalarGridSpec(
            num_scalar_prefetch=4,
            grid=(n_blocks,),
            in_specs=[
                pl.BlockSpec((EXPERT_BLOCK, HALF), blk),
                pl.BlockSpec(memory_space=pl.ANY),
                per_expert((1, 2 * D_EXPERT)),
                pl.BlockSpec(memory_space=pl.ANY),
                per_expert((1, D_MODEL)),
            ],
            out_specs=pl.BlockSpec((EXPERT_BLOCK, HALF), blk),
            scratch_shapes=[
                pltpu.VMEM((D_MODEL, 2 * D_EXPERT), f32),
                pltpu.VMEM((D_EXPERT, D_MODEL), f32),
                pltpu.VMEM((D_MODEL, 2 * D_EXPERT), bf16),
                pltpu.VMEM((D_EXPERT, D_MODEL), bf16),
                pltpu.VMEM((D_MODEL // LANES, LANES, LANES), f32),
                pltpu.VMEM((EXPERT_BLOCK, D_EXPERT), bf16),
                pltpu.SemaphoreType.DMA(()),
                pltpu.SemaphoreType.DMA(()),
            ],
        ),
        out_shape=jax.ShapeDtypeStruct(rows.shape, u32),
        compiler_params=pltpu.CompilerParams(
            dimension_semantics=("arbitrary",), vmem_limit_bytes=VMEM_LIMIT),
        name="experts",
    )(block_expert, n_active, next_expert, block_full, rows, w_gu, b_gu, w_dn, b_dn)


def _combine_kernel(h_ref, gate_ref, y_ref, g2_ref, b2_ref, *rest):
    o_ref = rest[-1]
    pad = jnp.zeros((LANES - GATE_ROWS, ROW_TILE), f32)
    gates = jnp.concatenate([gate_ref[...], pad], axis=0).T
    left = jnp.zeros((ROW_TILE, HALF), f32)
    right = jnp.zeros((ROW_TILE, HALF), f32)
    for k in range(TOP_K):
        yl, yr = _unpack_rows(y_ref[k])
        gk = gates[:, k:k + 1]
        left = left + yl * gk
        right = right + yr * gk
    ffn = jnp.concatenate([left, right], axis=1)
    o_ref[...] = _layer_norm(DEEPNORM_ALPHA * h_ref[...] + ffn, g2_ref[...], b2_ref[...])


def _combine(h1, gates, y_part, g2, b2, first_tok, prev_out):
    n = h1.shape[0]
    tiles = y_part.shape[1] // ROW_TILE
    first = first_tok // ROW_TILE
    vec = pl.BlockSpec((1, D_MODEL), lambda i: (0, 0))
    chained = [] if prev_out is None else [prev_out]
    return pl.pallas_call(
        _combine_kernel,
        grid=(tiles,),
        in_specs=[
            pl.BlockSpec((ROW_TILE, D_MODEL), lambda i: (first + i, 0)),
            pl.BlockSpec((GATE_ROWS, ROW_TILE), lambda i: (0, first + i)),
            pl.BlockSpec((TOP_K, ROW_TILE, HALF), lambda i: (0, i, 0)),
            vec, vec,
        ] + [pl.BlockSpec(memory_space=pl.ANY)] * len(chained),
        out_specs=pl.BlockSpec((ROW_TILE, D_MODEL), lambda i: (first + i, 0)),
        out_shape=jax.ShapeDtypeStruct((n, D_MODEL), f32),
        input_output_aliases={5: 0} if chained else {},
        compiler_params=pltpu.CompilerParams(
            dimension_semantics=("arbitrary",), vmem_limit_bytes=VMEM_LIMIT),
        name="combine",
    )(h1, gates, y_part, g2, b2, *chained)


def _rope_tables(pos, scale):
    inv = (1.0 / (ROPE_THETA ** (np.arange(0, A_HEAD_DIM, 2, dtype=np.float32) / A_HEAD_DIM))).astype(np.float32)
    ang = pos.astype(np.float32)[:, None] * inv[None, :]
    ang = np.concatenate([ang, ang, ang, ang], axis=-1)
    sign = np.where((np.arange(LANES) & 63) < 32, -1.0, 1.0).astype(np.float32)
    scale = np.float32(scale)
    return jnp.asarray(np.cos(ang) * scale), jnp.asarray(np.sin(ang) * sign * scale)


def kernel(x, meta_tokens, ln_emb_g, ln_emb_b, w_in, lambda_q1, lambda_k1, lambda_q2, lambda_k2,
           subln_g, hgrn_lb_table, hgrn_norm_g, w_out, ln1_g, ln1_b, w_router, b_router,
           w_gate_up, b_gate_up, w_down, b_down, ln2_g, ln2_b):
    batch, seq, _ = x.shape
    n_tok = batch * seq
    x2 = x.reshape(n_tok, D_MODEL)
    row = lambda v: v.reshape(1, -1).astype(f32)

    w_in_bf = w_in[0].astype(bf16)
    lam = (jnp.exp(jnp.sum(lambda_q1[0].astype(f32) * lambda_k1[0].astype(f32)))
           - jnp.exp(jnp.sum(lambda_q2[0].astype(f32) * lambda_k2[0].astype(f32))) + LAM_INIT).reshape(1)
    lb = jnp.cumsum(jax.nn.softmax(hgrn_lb_table.astype(f32), axis=0), axis=0)[0].reshape(B_HEADS, 1, LANES)
    q_scale = A_HEAD_DIM ** -0.5 * math.log2(math.e)
    pos_x = N_META + np.arange(seq)
    pos_m = np.arange(N_META)
    cq, sq = _rope_tables(pos_x, q_scale)
    ck, sk = _rope_tables(pos_x, 1.0)
    cqm, sqm = _rope_tables(pos_m, q_scale)
    ckm, skm = _rope_tables(pos_m, 1.0)
    wr = w_router[0].astype(f32)
    wr_hi = wr.astype(bf16)
    wr_lo = (wr - wr_hi.astype(f32)).astype(bf16)
    wr_split = jnp.pad(jnp.concatenate([wr_hi, wr_lo], axis=1), ((0, 0), (0, LANES - 2 * N_EXPERTS)))
    bgu = b_gate_up[0].reshape(N_EXPERTS, 1, 2 * D_EXPERT)
    bdn = b_down[0].reshape(N_EXPERTS, 1, D_MODEL)

    g0, b0 = row(ln_emb_g), row(ln_emb_b)
    qkv, hg = _in_proj(x2, g0, b0, w_in_bf, (cq, sq, ck, sk), PROJ_TILE, seq // PROJ_TILE)
    qkv_m, hg_m = _in_proj(meta_tokens, g0, b0, w_in_bf, (cqm, sqm, ckm, skm), N_META, 1)

    ya = _attention(lam, qkv, qkv_m, subln_g[0].reshape(-1, 1).astype(f32), batch, seq)
    yb = _hgrn(hg, hg_m, lb, row(hgrn_norm_g[0]), batch, seq)

    h1, hp, top_idx, gates, rank, counts = _mix_route(
        x2, g0, b0, ya, yb, w_out[0].astype(bf16), row(ln1_g[0]), row(ln1_b[0]),
        wr_split, b_router[0].reshape(-1, 1).astype(f32))

    counts = counts.reshape(N_EXPERTS)
    blocks = (counts + EXPERT_BLOCK - 1) // EXPERT_BLOCK
    cum_blocks = jnp.cumsum(blocks)
    start_pad = (cum_blocks - blocks) * EXPERT_BLOCK
    n_blocks = n_tok * TOP_K // EXPERT_BLOCK + N_EXPERTS
    n_active = cum_blocks[-1:].astype(i32)
    jb = jnp.arange(n_blocks, dtype=i32)
    block_expert = jnp.sum((cum_blocks[None, :] <= jnp.minimum(jb, n_active - 1)[:, None]).astype(i32), axis=1)
    block_expert = jnp.minimum(block_expert, N_EXPERTS - 1).astype(i32)
    eid = jnp.arange(N_EXPERTS, dtype=i32)
    later_active = jnp.logical_and(blocks[None, :] > 0, eid[None, :] > eid[:, None])
    next_active = jnp.min(jnp.where(later_active, eid[None, :], N_EXPERTS), axis=1)
    next_active = jnp.where(next_active == N_EXPERTS, -1, next_active).astype(i32)
    per_block = EXPERT_BLOCK // EXPERT_SUB
    last_used = (counts - (blocks - 1) * EXPERT_BLOCK + EXPERT_SUB - 1) // EXPERT_SUB
    is_last = jnp.logical_and((cum_blocks - 1)[None, :] == jb[:, None], blocks[None, :] > 0)
    block_full = (per_block - jnp.sum(jnp.where(is_last, per_block - last_used[None, :], 0), axis=1)).astype(i32)
    next_expert = jnp.sum(jnp.where(block_expert[:, None] == eid[None, :], next_active[None, :], 0), axis=1)
    dest = _dest_rows(top_idx, rank, start_pad)
    chunked = lambda c: dest.reshape(TOP_K, n_tok // c, c).transpose(1, 0, 2)
    idx_chunks = chunked(SC_CHUNK)

    x_rows = _dispatch(hp, chunked(SC_SCATTER_CHUNK), n_blocks * EXPERT_BLOCK)
    y_rows = _experts(block_expert, n_active, next_expert, block_full, x_rows, w_gate_up[0], bgu, w_down[0], bdn)
    unit = n_tok // sum(COMBINE_PARTS)
    out, first_tok = None, 0
    for share in COMBINE_PARTS:
        part_tok = share * unit
        chunks = idx_chunks[first_tok // SC_CHUNK:(first_tok + part_tok) // SC_CHUNK]
        y_part = _gather(y_rows, chunks, part_tok)
        out = _combine(h1, gates, y_part, row(ln2_g[0]), row(ln2_b[0]), first_tok, out)
        first_tok += part_tok
    return out.reshape(batch, seq, D_MODEL)
```
